```python
import jax
import jax.numpy as jnp
from jax import lax
import numpy as np

D_MODEL = 1024
BATCH = 4
SEQ = 4096
DEPTH = 2

GRID_W = 64
CTX_LEN = 256
HEAD_DIM = 64
DN_HEADS = 6
ATTN_HEADS = 6
ATTN_KV_HEADS = 2
ATTN_GROUP = ATTN_HEADS // ATTN_KV_HEADS
MLP_GROUPS = 4
DN_WIDTH = DN_HEADS * HEAD_DIM
ATTN_WIDTH = ATTN_HEADS * HEAD_DIM
ATTN_KV_WIDTH = ATTN_KV_HEADS * HEAD_DIM
MLP_WIDTH = MLP_GROUPS * HEAD_DIM
D_MIX = DN_WIDTH + ATTN_WIDTH + MLP_WIDTH
SPLIT_SIZES = (3 * DN_WIDTH, DN_WIDTH, 2 * DN_HEADS, 2 * DN_HEADS, ATTN_WIDTH, ATTN_KV_WIDTH, ATTN_KV_WIDTH, MLP_WIDTH, MLP_WIDTH)
IN_COLS = 3 * DN_WIDTH + DN_WIDTH + 4 * DN_HEADS + ATTN_WIDTH + 2 * ATTN_KV_WIDTH + 2 * MLP_WIDTH
CONV_K = 3
DN_CHUNK = 64
ATTN_BLOCK = 128
ATTN_SCALE = HEAD_DIM ** -0.5
MLP_CHUNK = 128
ROPE_THETA = 10000.0
ROPE_AXIS_DIM = HEAD_DIM // 2
ROPE_PAIRS = ROPE_AXIS_DIM // 2
D_FF = 2816
MOE_EXPERTS = 8
MOE_TOP_K = 2
MOE_D_FF = 3584
MOE_BLOCK = 256
N_DENSE = (DEPTH + 1) // 2
N_MOE = DEPTH // 2
NORM_EPS = 1e-6

kernel_name = 'hybrid_dit_deltanet_gqa_sgu_moe'


def rms_norm(x, gain):
    xf = x.astype(jnp.float32)
    y = xf * lax.rsqrt(jnp.mean(xf * xf, axis=-1, keepdims=True) + NORM_EPS)
    return (y * gain.astype(jnp.float32)).astype(x.dtype)


def ada_rms_norm(x, gain, shift, scale):
    xf = x.astype(jnp.float32)
    y = xf * lax.rsqrt(jnp.mean(xf * xf, axis=-1, keepdims=True) + NORM_EPS) * gain.astype(jnp.float32)
    return (y * (1.0 + scale.astype(jnp.float32)) + shift.astype(jnp.float32)).astype(x.dtype)


def l2_normalize(x):
    return x * lax.rsqrt(jnp.sum(x * x, axis=-1, keepdims=True) + NORM_EPS)


def split_columns(z):
    idx = np.cumsum(SPLIT_SIZES)[:-1].tolist()
    return jnp.split(z, idx, axis=-1)


def _rev(t, flag):
    return jnp.flip(t, axis=1) if flag else t


def centred_depthwise_conv(x, w):
    return lax.conv_general_dilated(x, w[:, None, :].astype(x.dtype), window_strides=(1,),
                                    padding=[(CONV_K // 2, CONV_K // 2)],
                                    dimension_numbers=('NWC', 'WIO', 'NWC'),
                                    feature_group_count=x.shape[-1])


def axial_rope_tables(n_tokens):
    rows = n_tokens // GRID_W
    row = jnp.repeat(jnp.arange(rows, dtype=jnp.float32), GRID_W)
    col = jnp.tile(jnp.arange(GRID_W, dtype=jnp.float32), rows)
    inv = ROPE_THETA ** (-2.0 * jnp.arange(ROPE_PAIRS, dtype=jnp.float32) / ROPE_AXIS_DIM)
    ang = jnp.stack([row[:, None] * inv, col[:, None] * inv], axis=1)
    return jnp.cos(ang), jnp.sin(ang)


def apply_axial_rope(x, cos, sin):
    xr = x.reshape(x.shape[:-1] + (2, 2, ROPE_PAIRS))
    a, b = xr[..., 0, :], xr[..., 1, :]
    c = cos[None, :, None].astype(x.dtype)
    s = sin[None, :, None].astype(x.dtype)
    out = jnp.stack([a * c - b * s, a * s + b * c], axis=-2)
    return out.reshape(x.shape)


def chunk_gated_delta(q, k, v, g, beta, state):
    B, T, H, Dk = q.shape
    Dv = v.shape[-1]
    n = T // DN_CHUNK

    def chunks(t):
        t = t.reshape((B, n, DN_CHUNK, H) + t.shape[3:])
        return jnp.moveaxis(jnp.moveaxis(t, 3, 2), 1, 0)

    qc, kc, vc = chunks(q), chunks(k), chunks(v)
    gc = jnp.cumsum(chunks(g), axis=-1)
    bc = chunks(beta)[..., None]
    incl = jnp.tril(jnp.ones((DN_CHUNK, DN_CHUNK), bool))
    strict = jnp.tril(jnp.ones((DN_CHUNK, DN_CHUNK), bool), -1)
    decay = jnp.exp(jnp.where(incl, gc[..., :, None] - gc[..., None, :], -jnp.inf))
    kb = kc * bc
    a_mat = jnp.where(strict, jnp.einsum('nbhid,nbhjd->nbhij', kb, kc) * decay, 0.0)
    rhs = jnp.concatenate([vc * bc, kb * jnp.exp(gc)[..., None]], axis=-1)
    sol = lax.linalg.triangular_solve(a_mat, rhs, left_side=True, lower=True, unit_diagonal=True)
    u, w = sol[..., :Dv], sol[..., Dv:]
    attn = jnp.einsum('nbhid,nbhjd->nbhij', qc, kc) * decay

    def step(S, xs):
        q_i, k_i, u_i, w_i, g_i, a_i = xs
        v_new = u_i - jnp.einsum('bhcd,bhde->bhce', w_i, S)
        o_i = (jnp.einsum('bhcd,bhde->bhce', q_i * jnp.exp(g_i)[..., None], S)
               + jnp.einsum('bhij,bhje->bhie', a_i, v_new))
        g_last = g_i[..., -1:]
        S = (S * jnp.exp(g_last)[..., None]
             + jnp.einsum('bhcd,bhce->bhde', k_i * jnp.exp(g_last - g_i)[..., None], v_new))
        return S, o_i

    S, o = lax.scan(step, state, (qc, kc, u, w, gc, attn))
    o = jnp.swapaxes(jnp.moveaxis(o, 0, 1), 2, 3).reshape(B, T, H, Dv)
    return o, S


def dn_inputs(z_qkv, z_beta, z_alpha, conv_w, a_log, dt_bias):
    B, T, _ = z_qkv.shape
    qkv = jax.nn.silu(centred_depthwise_conv(z_qkv, conv_w)).astype(jnp.float32)
    qkv = qkv.reshape(B, T, 3, DN_HEADS, HEAD_DIM)
    q = l2_normalize(qkv[:, :, 0]) * (HEAD_DIM ** -0.5)
    k = l2_normalize(qkv[:, :, 1])
    v = qkv[:, :, 2]
    beta = jax.nn.sigmoid(z_beta.astype(jnp.float32)).reshape(B, T, 2, DN_HEADS)
    g = -jnp.exp(a_log.astype(jnp.float32)) * jax.nn.softplus(
        z_alpha.astype(jnp.float32).reshape(B, T, 2, DN_HEADS) + dt_bias.astype(jnp.float32))
    return q, k, v, g, beta


def dn_bidirectional(ctx_in, lat_in):
    qc, kc, vc, gc, bc = ctx_in
    ql, kl, vl, gl, bl = lat_in
    B, _, H, Dk = qc.shape
    o_ctx, o_lat = [], []
    for d in range(2):
        rev = d == 1
        s0 = jnp.zeros((B, H, Dk, vc.shape[-1]), jnp.float32)
        oc, s_ctx = chunk_gated_delta(_rev(qc, rev), _rev(kc, rev), _rev(vc, rev),
                                      _rev(gc[:, :, d], rev), _rev(bc[:, :, d], rev), s0)
        ol, _ = chunk_gated_delta(_rev(ql, rev), _rev(kl, rev), _rev(vl, rev),
                                  _rev(gl[:, :, d], rev), _rev(bl[:, :, d], rev), s_ctx)
        o_ctx.append(_rev(oc, rev))
        o_lat.append(_rev(ol, rev))
    return o_ctx[0] + o_ctx[1], o_lat[0] + o_lat[1]


def dn_output(o, z_gate, norm_g):
    B, T, H, D = o.shape
    gate = jax.nn.silu(z_gate.astype(jnp.float32)).reshape(B, T, H, D)
    return (rms_norm(o, norm_g) * gate).reshape(B, T, H * D).astype(z_gate.dtype)


def attn_heads(zq, zk, zv, q_gain, k_gain):
    B, T, _ = zq.shape
    q = rms_norm(zq.reshape(B, T, ATTN_HEADS, HEAD_DIM), q_gain)
    k = rms_norm(zk.reshape(B, T, ATTN_KV_HEADS, HEAD_DIM), k_gain)
    v = zv.reshape(B, T, ATTN_KV_HEADS, HEAD_DIM)
    return q, k, v


def context_attention(q, k, v):
    B, L, H, D = q.shape
    qg = q.reshape(B, L, ATTN_KV_HEADS, ATTN_GROUP, D)
    s = jnp.einsum('bqhgd,bkhd->bhgqk', qg, k, preferred_element_type=jnp.float32) * ATTN_SCALE
    p = jax.nn.softmax(s, axis=-1).astype(v.dtype)
    return jnp.einsum('bhgqk,bkhd->bqhgd', p, v).reshape(B, L, H * D)


def latent_attention(q, k, v, k_ctx, v_ctx):
    B, S, H, D = q.shape
    k_all = jnp.concatenate([k_ctx, k], axis=1)
    v_all = jnp.concatenate([v_ctx, v], axis=1)
    qb = jnp.moveaxis(q.reshape(B, S // ATTN_BLOCK, ATTN_BLOCK, ATTN_KV_HEADS, ATTN_GROUP, D), 1, 0)

    def attend(q_blk):
        s = jnp.einsum('bqhgd,bkhd->bhgqk', q_blk, k_all, preferred_element_type=jnp.float32) * ATTN_SCALE
        p = jax.nn.softmax(s, axis=-1).astype(v_all.dtype)
        return jnp.einsum('bhgqk,bkhd->bqhgd', p, v_all)

    o = lax.map(attend, qb)
    return jnp.moveaxis(o, 0, 1).reshape(B, S, H * D)


def spatial_gating(z_u, z_v, norm_g, w_s, b_s):
    B, T, _ = z_u.shape
    gd = MLP_WIDTH // MLP_GROUPS
    u = jax.nn.gelu(z_u)
    v = rms_norm(jax.nn.gelu(z_v).reshape(B, T, MLP_GROUPS, gd), norm_g.reshape(MLP_GROUPS, gd))
    v = v.reshape(B, T // MLP_CHUNK, MLP_CHUNK, MLP_GROUPS, gd)
    mixed = jnp.einsum('gij,bnjgd->bnigd', w_s.astype(v.dtype), v) + b_s.T[None, None, :, :, None].astype(v.dtype)
    return u * mixed.reshape(B, T, MLP_WIDTH)


def token_mixer(h, hc, with_ctx_out, w_in, conv_w, dn_a_log, dn_dt_bias, dn_norm_g, q_norm_g, k_norm_g,
                sgu_norm_g, sgu_w, sgu_b, w_out, rope_cos, rope_sin):
    z = split_columns(h @ w_in)
    zc = split_columns(hc @ w_in)
    dn_lat = dn_inputs(z[0], z[2], z[3], conv_w, dn_a_log, dn_dt_bias)
    dn_ctx = dn_inputs(zc[0], zc[2], zc[3], conv_w, dn_a_log, dn_dt_bias)
    o_ctx, o_lat = dn_bidirectional(dn_ctx, dn_lat)
    q, k, v = attn_heads(z[4], z[5], z[6], q_norm_g, k_norm_g)
    q = apply_axial_rope(q, rope_cos, rope_sin)
    k = apply_axial_rope(k, rope_cos, rope_sin)
    q_c, k_c, v_c = attn_heads(zc[4], zc[5], zc[6], q_norm_g, k_norm_g)
    mixed = jnp.concatenate([dn_output(o_lat, z[1], dn_norm_g),
                             latent_attention(q, k, v, k_c, v_c),
                             spatial_gating(z[7], z[8], sgu_norm_g, sgu_w, sgu_b)], axis=-1)
    y = mixed @ w_out
    if not with_ctx_out:
        return y, None
    mixed_c = jnp.concatenate([dn_output(o_ctx, zc[1], dn_norm_g),
                               context_attention(q_c, k_c, v_c),
                               spatial_gating(zc[7], zc[8], sgu_norm_g, sgu_w, sgu_b)], axis=-1)
    return y, mixed_c @ w_out


def swiglu(h, w1, w3, w2):
    return (jax.nn.silu(h @ w1) * (h @ w3)) @ w2


def moe_swiglu(h, router_w, router_b, w1, w3, w2):
    shape = h.shape
    t = h.reshape(-1, shape[-1])
    n = t.shape[0]
    logits = jnp.einsum('nd,de->ne', t, router_w, preferred_element_type=jnp.float32) + router_b.astype(jnp.float32)
    top_logit, top_idx = lax.top_k(logits, MOE_TOP_K)
    gates = jax.nn.softmax(top_logit, axis=-1)
    n_assign = n * MOE_TOP_K
    expert = top_idx.reshape(-1)
    token = jnp.repeat(jnp.arange(n, dtype=jnp.int32), MOE_TOP_K)
    order = jnp.argsort(expert)
    e_sorted, tok_sorted, gate_sorted = expert[order], token[order], gates.reshape(-1)[order]
    counts = jnp.bincount(expert, length=MOE_EXPERTS)
    padded = (counts + MOE_BLOCK - 1) // MOE_BLOCK * MOE_BLOCK
    starts = jnp.cumsum(counts) - counts
    pad_ends = jnp.cumsum(padded)
    pad_starts = pad_ends - padded
    dest = pad_starts[e_sorted] + jnp.arange(n_assign, dtype=jnp.int32) - starts[e_sorted]
    n_blocks = -(-(n_assign + MOE_EXPERTS * (MOE_BLOCK - 1)) // MOE_BLOCK)
    slot_token = jnp.full((n_blocks * MOE_BLOCK,), n, jnp.int32).at[dest].set(tok_sorted)
    block_expert = jnp.minimum(jnp.searchsorted(pad_ends, jnp.arange(n_blocks) * MOE_BLOCK, side='right'),
                               MOE_EXPERTS - 1)
    t_pad = jnp.concatenate([t, jnp.zeros((1, t.shape[1]), t.dtype)], axis=0)
    xb = t_pad[slot_token].reshape(n_blocks, MOE_BLOCK, t.shape[1])

    def expert_block(args):
        xi, e = args
        return swiglu(xi, w1[e], w3[e], w2[e])

    yb = lax.map(expert_block, (xb, block_expert)).reshape(n_blocks * MOE_BLOCK, t.shape[1])
    y = jnp.zeros_like(t).at[tok_sorted].add(yb[dest] * gate_sorted[:, None].astype(t.dtype))
    return y.reshape(shape)


def channel_mixer(h, layer, ffn_w1, ffn_w3, ffn_w2, router_w, router_b, moe_w1, moe_w3, moe_w2):
    i = layer // 2
    if layer % 2 == 0:
        return swiglu(h, ffn_w1[i], ffn_w3[i], ffn_w2[i])
    return moe_swiglu(h, router_w[i], router_b[i], moe_w1[i], moe_w3[i], moe_w2[i])


def setup_inputs(seed: int = 0) -> dict:
    key = jax.random.key(seed)
    ks = iter(jax.random.split(key, 32))

    def nrm(shape, scale):
        return jax.random.normal(next(ks), shape, jnp.float32) * scale

    def gain(shape):
        return 1.0 + nrm(shape, 0.02)

    dt = jnp.exp(jax.random.uniform(next(ks), (DEPTH, 2, DN_HEADS), jnp.float32, np.log(1e-3), np.log(1e-1)))
    return {
        'x': nrm((BATCH, SEQ, D_MODEL), 1.0),
        'c': nrm((BATCH, D_MODEL), 1.0),
        'ctx': nrm((BATCH, CTX_LEN, D_MODEL), 1.0),
        'c_ctx': nrm((D_MODEL,), 1.0),
        'mod_w': nrm((DEPTH, D_MODEL, 6 * D_MODEL), 0.5 * D_MODEL ** -0.5),
        'mod_b': nrm((DEPTH, 6 * D_MODEL), 0.02),
        'norm1_g': gain((DEPTH, D_MODEL)),
        'norm2_g': gain((DEPTH, D_MODEL)),
        'w_in': nrm((DEPTH, D_MODEL, IN_COLS), D_MODEL ** -0.5),
        'conv_w': nrm((DEPTH, CONV_K, 3 * DN_WIDTH), CONV_K ** -0.5),
        'dn_a_log': jnp.log(jax.random.uniform(next(ks), (DEPTH, 2, DN_HEADS), jnp.float32, 1.0, 16.0)),
        'dn_dt_bias': dt + jnp.log(-jnp.expm1(-dt)),
        'dn_norm_g': gain((DEPTH, HEAD_DIM)),
        'q_norm_g': gain((DEPTH, HEAD_DIM)),
        'k_norm_g': gain((DEPTH, HEAD_DIM)),
        'sgu_norm_g': gain((DEPTH, MLP_WIDTH)),
        'sgu_w': nrm((DEPTH, MLP_GROUPS, MLP_CHUNK, MLP_CHUNK), MLP_CHUNK ** -0.5),
        'sgu_b': gain((DEPTH, MLP_GROUPS, MLP_CHUNK)),
        'w_out': nrm((DEPTH, D_MIX, D_MODEL), D_MIX ** -0.5),
        'ffn_w1': nrm((N_DENSE, D_MODEL, D_FF), D_MODEL ** -0.5),
        'ffn_w3': nrm((N_DENSE, D_MODEL, D_FF), D_MODEL ** -0.5),
        'ffn_w2': nrm((N_DENSE, D_FF, D_MODEL), D_FF ** -0.5),
        'router_w': nrm((N_MOE, D_MODEL, MOE_EXPERTS), D_MODEL ** -0.5),
        'router_b': nrm((N_MOE, MOE_EXPERTS), 0.01),
        'moe_w1': nrm((N_MOE, MOE_EXPERTS, D_MODEL, MOE_D_FF), D_MODEL ** -0.5),
        'moe_w3': nrm((N_MOE, MOE_EXPERTS, D_MODEL, MOE_D_FF), D_MODEL ** -0.5),
        'moe_w2': nrm((N_MOE, MOE_EXPERTS, MOE_D_FF, D_MODEL), MOE_D_FF ** -0.5),
        'final_norm_g': gain((D_MODEL,)),
    }


def reference(x, c, ctx, c_ctx, mod_w, mod_b, norm1_g, norm2_g, w_in, conv_w, dn_a_log, dn_dt_bias, dn_norm_g,
              q_norm_g, k_norm_g, sgu_norm_g, sgu_w, sgu_b, w_out, ffn_w1, ffn_w3, ffn_w2, router_w, router_b,
              moe_w1, moe_w3, moe_w2, final_norm_g):
    rope_cos, rope_sin = axial_rope_tables(x.shape[1])
    cond = jax.nn.silu(c)[:, None, :]
    cond_ctx = jax.nn.silu(c_ctx)[None, None, :]
    xc = ctx
    for layer in range(DEPTH):
        last = layer == DEPTH - 1
        mod = jnp.split(cond @ mod_w[layer] + mod_b[layer], 6, axis=-1)
        mod_c = jnp.split(cond_ctx @ mod_w[layer] + mod_b[layer], 6, axis=-1)
        h = ada_rms_norm(x, norm1_g[layer], mod[0], mod[1])
        hc = ada_rms_norm(xc, norm1_g[layer], mod_c[0], mod_c[1])
        y, yc = token_mixer(h, hc, not last, w_in[layer], conv_w[layer], dn_a_log[layer], dn_dt_bias[layer],
                            dn_norm_g[layer], q_norm_g[layer], k_norm_g[layer], sgu_norm_g[layer], sgu_w[layer],
                            sgu_b[layer], w_out[layer], rope_cos, rope_sin)
        x = x + mod[2] * y
        h = ada_rms_norm(x, norm2_g[layer], mod[3], mod[4])
        x = x + mod[5] * channel_mixer(h, layer, ffn_w1, ffn_w3, ffn_w2, router_w, router_b, moe_w1, moe_w3, moe_w2)
        if not last:
            xc = xc + mod_c[2] * yc
            hc = ada_rms_norm(xc, norm2_g[layer], mod_c[3], mod_c[4])
            xc = xc + mod_c[5] * channel_mixer(hc, layer, ffn_w1, ffn_w3, ffn_w2, router_w, router_b,
                                               moe_w1, moe_w3, moe_w2)
    return rms_norm(x, final_norm_g)
```

```python
import functools

import jax
import jax.numpy as jnp
import numpy as np
from jax import lax
from jax.experimental import pallas as pl
from jax.experimental.pallas import tpu as pltpu

D_MODEL = 1024
BATCH = 4
SEQ = 4096
DEPTH = 2
GRID_W = 64
CTX_LEN = 256
HEAD_DIM = 64
DN_HEADS = 6
ATTN_HEADS = 6
ATTN_KV_HEADS = 2
ATTN_GROUP = ATTN_HEADS // ATTN_KV_HEADS
MLP_GROUPS = 4
DN_WIDTH = DN_HEADS * HEAD_DIM
ATTN_WIDTH = ATTN_HEADS * HEAD_DIM
ATTN_KV_WIDTH = ATTN_KV_HEADS * HEAD_DIM
MLP_WIDTH = MLP_GROUPS * HEAD_DIM
D_MIX = DN_WIDTH + ATTN_WIDTH + MLP_WIDTH
CONV_K = 3
DN_CHUNK = 64
ATTN_SCALE = HEAD_DIM ** -0.5
MLP_CHUNK = 128
ROPE_THETA = 10000.0
ROPE_AXIS_DIM = HEAD_DIM // 2
ROPE_PAIRS = ROPE_AXIS_DIM // 2
D_FF = 2816
MOE_EXPERTS = 8
MOE_TOP_K = 2
MOE_D_FF = 3584
MOE_BLOCK = 256
NORM_EPS = 1e-6

LANES = 128
TT = CTX_LEN + SEQ
TM = 256
N_TILES = TT // TM
CTX_ROW = BATCH
N_DN_STEPS = TT // DN_CHUNK
N_CTX_CHUNKS = CTX_LEN // DN_CHUNK
IN_PAD = 3 * DN_WIDTH + DN_WIDTH + ATTN_WIDTH + 2 * ATTN_KV_WIDTH + 2 * MLP_WIDTH + LANES
N_LAT = BATCH * SEQ
N_ASSIGN = N_LAT * MOE_TOP_K
MOE_N_BLOCKS = -(-(N_ASSIGN + MOE_EXPERTS * (MOE_BLOCK - 1)) // MOE_BLOCK)
MOE_ROWS = MOE_N_BLOCKS * MOE_BLOCK
MOE_FF_SPLIT = 2
MOE_FF_BLK = MOE_D_FF // MOE_FF_SPLIT
ROUTE_TILE = 512
VMEM_LIMIT = 56 * 2 ** 20

F32 = jnp.float32
BF16 = jnp.bfloat16
HI = lax.Precision.HIGHEST


def _cparams(*sem):
    return pltpu.CompilerParams(dimension_semantics=sem, vmem_limit_bytes=VMEM_LIMIT)


def _bdot(a, b):
    return jnp.dot(a.astype(BF16), b.astype(BF16), preferred_element_type=F32)


def _bdot_nt(a, b):
    return lax.dot_general(a.astype(BF16), b.astype(BF16), (((1,), (1,)), ((), ())), preferred_element_type=F32)


def _bdot_tn(a, b):
    return lax.dot_general(a.astype(BF16), b.astype(BF16), (((0,), (0,)), ((), ())), preferred_element_type=F32)


def _seg_ones():
    r = lax.shift_right_logical(lax.broadcasted_iota(jnp.int32, (LANES, LANES), 0), 6)
    c = lax.shift_right_logical(lax.broadcasted_iota(jnp.int32, (LANES, LANES), 1), 6)
    return (r == c).astype(F32)


def _seg_sum(y, ones):
    parts = [jnp.dot(y[:, i:i + LANES], ones, precision=HI, preferred_element_type=F32)
             for i in range(0, y.shape[-1], LANES)]
    return parts[0] if len(parts) == 1 else jnp.concatenate(parts, axis=-1)


def _softplus(x):
    return jnp.maximum(x, 0.0) + jnp.log1p(jnp.exp(-jnp.abs(x)))


def _ada_norm(x, gain, shift, scale):
    y = x * lax.rsqrt(jnp.mean(x * x, axis=-1, keepdims=True) + NORM_EPS) * gain
    return y * (1.0 + scale) + shift


def _mod_index(b, i):
    return (jnp.where(i == 0, CTX_ROW, b), 0, 0)


def _mod_kernel(c_ref, w_ref, b_ref, o_ref):
    cond = jax.nn.silu(c_ref[...])
    o_ref[0] = jnp.dot(cond, w_ref[0], precision=HI, preferred_element_type=F32) + b_ref[0]


def _modulation(cond8, mod_w, mod_b):
    nblk = 4
    bn = 6 * D_MODEL // nblk
    out = pl.pallas_call(
        _mod_kernel,
        grid=(DEPTH, nblk),
        in_specs=[pl.BlockSpec((8, D_MODEL), lambda l, j: (0, 0)),
                  pl.BlockSpec((1, D_MODEL, bn), lambda l, j: (l, 0, j)),
                  pl.BlockSpec((1, 1, bn), lambda l, j: (l, 0, j))],
        out_specs=pl.BlockSpec((1, 8, bn), lambda l, j: (l, 0, j)),
        out_shape=jax.ShapeDtypeStruct((DEPTH, 8, 6 * D_MODEL), F32),
        compiler_params=_cparams("arbitrary", "arbitrary"),
        name="modulation",
    )(cond8, mod_w, mod_b.reshape(DEPTH, 1, 6 * D_MODEL))
    return out.reshape(DEPTH, 8, 6, D_MODEL)


_IN_SPLITS = (3 * DN_WIDTH, DN_WIDTH, ATTN_WIDTH, ATTN_KV_WIDTH, ATTN_KV_WIDTH, MLP_WIDTH, MLP_WIDTH, LANES)


def _in_proj_kernel(x_ref, mod_ref, g_ref, w_ref, oqkv, ogate, oq, ok, ov, ou, ov2, oba):
    m = mod_ref[0]
    h = _ada_norm(x_ref[0], g_ref[...], m[0:1], m[1:2]).astype(BF16)
    z = jnp.dot(h, w_ref[...], preferred_element_type=F32)
    off = 0
    for ref, width in zip((oqkv, ogate, oq, ok, ov, ou, ov2, oba), _IN_SPLITS):
        ref[0] = z[:, off:off + width].astype(ref.dtype)
        off += width


def _in_proj(x_all, mod_l, gain, w_in_r):
    dts = (BF16,) * 7 + (F32,)
    row = lambda b, i: (b, i, 0)
    return pl.pallas_call(
        _in_proj_kernel,
        grid=(BATCH, N_TILES),
        in_specs=[pl.BlockSpec((1, TM, D_MODEL), row),
                  pl.BlockSpec((1, 6, D_MODEL), _mod_index),
                  pl.BlockSpec((1, D_MODEL), lambda b, i: (0, 0)),
                  pl.BlockSpec((D_MODEL, IN_PAD), lambda b, i: (0, 0))],
        out_specs=[pl.BlockSpec((1, TM, w), row) for w in _IN_SPLITS],
        out_shape=[jax.ShapeDtypeStruct((BATCH, TT, w), dt) for w, dt in zip(_IN_SPLITS, dts)],
        compiler_params=_cparams("parallel", "arbitrary"),
        name="in_proj",
    )(x_all, mod_l, gain.reshape(1, D_MODEL), w_in_r)


def _dn_prep_kernel(z_ref, w_ref, o_ref):
    j = pl.program_id(1)
    z = z_ref[0].astype(F32)
    w = w_ref[...]
    row = lax.broadcasted_iota(jnp.int32, (TT, 1), 0)
    first = (row == 0) | (row == CTX_LEN)
    last = (row == CTX_LEN - 1) | (row == TT - 1)
    zp = jnp.where(first, 0.0, pltpu.roll(z, 1, 0))
    zn = jnp.where(last, 0.0, pltpu.roll(z, TT - 1, 0))
    y = jax.nn.silu(w[0:1] * zp + w[1:2] * z + w[2:3] * zn)
    n_qk = 2 * DN_WIDTH // LANES
    n_q = DN_WIDTH // LANES

    @pl.when(j < n_qk)
    def _():
        inv = lax.rsqrt(_seg_sum(y * y, _seg_ones()) + NORM_EPS)
        o_ref[0] = (y * inv * jnp.where(j < n_q, HEAD_DIM ** -0.5, 1.0)).astype(o_ref.dtype)

    @pl.when(j >= n_qk)
    def _():
        o_ref[0] = y.astype(o_ref.dtype)


def _dn_prep(zqkv, conv_w):
    nb = 3 * DN_WIDTH // LANES
    return pl.pallas_call(
        _dn_prep_kernel,
        grid=(BATCH, nb),
        in_specs=[pl.BlockSpec((1, TT, LANES), lambda b, j: (b, 0, j)),
                  pl.BlockSpec((CONV_K, LANES), lambda b, j: (0, j))],
        out_specs=pl.BlockSpec((1, TT, LANES), lambda b, j: (b, 0, j)),
        out_shape=jax.ShapeDtypeStruct((BATCH, TT, 3 * DN_WIDTH), BF16),
        compiler_params=_cparams("parallel", "arbitrary"),
        name="dn_prep",
    )(zqkv, conv_w)


def _dn_kernel(qkv_ref, zba_ref, a_ref, dt_ref, o_ref, s_scr):
    C = DN_CHUNK
    o_ref[...] = jnp.zeros_like(o_ref)
    s_scr[...] = jnp.zeros_like(s_scr)
    ii = lax.broadcasted_iota(jnp.int32, (C, C), 0)
    jj = lax.broadcasted_iota(jnp.int32, (C, C), 1)
    neg_decay_rate = -jnp.exp(a_ref[...])
    dt_bias = dt_ref[...]

    def step(s, carry):
        chunk_of = (s, jnp.where(s < N_CTX_CHUNKS, N_CTX_CHUNKS - 1 - s, N_DN_STEPS + N_CTX_CHUNKS - 1 - s))
        rows, gcs = [], []
        for d in range(2):
            r0 = pl.multiple_of(chunk_of[d] * C, C)
            zba = zba_ref[0, pl.ds(r0, C), :]
            g_all = neg_decay_rate * _softplus(zba + dt_bias)
            incl = (ii >= jj) if d == 0 else (ii <= jj)
            gcs.append(jnp.dot(incl.astype(F32), g_all, precision=HI, preferred_element_type=F32))
            rows.append(r0)
        gc_t = jnp.concatenate(gcs, axis=0).T
        for d in range(2):
            r0 = rows[d]
            qkv = qkv_ref[0, pl.ds(r0, C), :]
            beta_all = jax.nn.sigmoid(zba_ref[0, pl.ds(r0, C), :])
            incl = (ii >= jj) if d == 0 else (ii <= jj)
            strict = (ii > jj) if d == 0 else (ii < jj)
            gc_all = gcs[d]
            gc_all_t = gc_t[:, d * C:(d + 1) * C]
            last_row = C - 1 if d == 0 else 0
            outs = []
            for h in range(DN_HEADS):
                lb = d * DN_HEADS + h
                la = 2 * DN_HEADS + lb
                q = qkv[:, h * HEAD_DIM:(h + 1) * HEAD_DIM]
                k = qkv[:, DN_WIDTH + h * HEAD_DIM:DN_WIDTH + (h + 1) * HEAD_DIM]
                v = qkv[:, 2 * DN_WIDTH + h * HEAD_DIM:2 * DN_WIDTH + (h + 1) * HEAD_DIM]
                qf, kf, vf = q.astype(F32), k.astype(F32), v.astype(F32)
                bcol = beta_all[:, lb:lb + 1]
                gcol = gc_all[:, la:la + 1]
                grow = gc_all_t[la:la + 1, :]
                g_last = gc_all[last_row:last_row + 1, la:la + 1]
                e_diff = jnp.exp(gcol - grow)
                dec_incl = jnp.where(incl, e_diff, 0.0)
                dec_strict = jnp.where(strict, e_diff, 0.0)
                kb = kf * bcol
                nb_mat = -(_bdot_nt(kb, k) * dec_strict)
                attn = _bdot_nt(q, k) * dec_incl
                x = jnp.concatenate([vf * bcol, kb * jnp.exp(gcol)], axis=1)
                p = nb_mat
                for it in range(6):
                    if it < 5:
                        r = jnp.dot(p, jnp.concatenate([x, p], axis=1), precision=HI, preferred_element_type=F32)
                        x = x + r[:, :2 * HEAD_DIM]
                        p = r[:, 2 * HEAD_DIM:]
                    else:
                        x = x + jnp.dot(p, x, precision=HI, preferred_element_type=F32)
                u, w = x[:, :HEAD_DIM], x[:, HEAD_DIM:]
                S = s_scr[lb]
                ws = _bdot(jnp.concatenate([w, qf * jnp.exp(gcol)], axis=0), S)
                v_new = u - ws[:C]
                outs.append(ws[C:] + _bdot(attn, v_new))
                kd = kf * jnp.exp(g_last - gcol)
                s_scr[lb] = S * jnp.exp(g_last) + _bdot_tn(kd, v_new)
            o_ref[0, pl.ds(r0, C), :] += jnp.concatenate(outs, axis=1)
        return carry

    lax.fori_loop(0, N_DN_STEPS, step, 0)


def _deltanet(qkv, zba, a_log, dt_bias):
    pad = lambda t: jnp.zeros((1, LANES), F32).at[0, 2 * DN_HEADS:4 * DN_HEADS].set(t.reshape(-1))
    return pl.pallas_call(
        _dn_kernel,
        grid=(BATCH,),
        in_specs=[pl.BlockSpec((1, TT, 3 * DN_WIDTH), lambda b: (b, 0, 0)),
                  pl.BlockSpec((1, TT, LANES), lambda b: (b, 0, 0)),
                  pl.BlockSpec((1, LANES), lambda b: (0, 0)),
                  pl.BlockSpec((1, LANES), lambda b: (0, 0))],
        out_specs=pl.BlockSpec((1, TT, DN_WIDTH), lambda b: (b, 0, 0)),
        out_shape=jax.ShapeDtypeStruct((BATCH, TT, DN_WIDTH), F32),
        scratch_shapes=[pltpu.VMEM((2 * DN_HEADS, HEAD_DIM, HEAD_DIM), F32)],
        compiler_params=_cparams("parallel"),
        name="deltanet",
    )(qkv, zba, pad(a_log), pad(dt_bias))


def _rope_tables():
    rows = SEQ // GRID_W
    row = jnp.repeat(jnp.arange(rows, dtype=F32), GRID_W)
    col = jnp.tile(jnp.arange(GRID_W, dtype=F32), rows)
    inv = ROPE_THETA ** (-2.0 * jnp.arange(ROPE_PAIRS, dtype=F32) / ROPE_AXIS_DIM)
    ang = jnp.stack([row[:, None] * inv, col[:, None] * inv], axis=1)
    cos, sin = jnp.cos(ang), jnp.sin(ang)
    c = jnp.concatenate([cos[:, 0], cos[:, 0], cos[:, 1], cos[:, 1]], axis=-1)
    s = jnp.concatenate([-sin[:, 0], sin[:, 0], -sin[:, 1], sin[:, 1]], axis=-1)
    c = jnp.concatenate([jnp.ones((CTX_LEN, HEAD_DIM), F32), c], axis=0)
    s = jnp.concatenate([jnp.zeros((CTX_LEN, HEAD_DIM), F32), s], axis=0)
    return jnp.tile(c, (1, 2)), jnp.tile(s, (1, 2))


def _attn_prep_kernel(zq_ref, zk_ref, zv_ref, gq_ref, gk_ref, c_ref, s_ref, oq, ok, ov):
    ones = _seg_ones()
    cs, sn = c_ref[...], s_ref[...]
    lane = lax.broadcasted_iota(jnp.int32, (1, LANES), 1)
    first_half = (lane & (2 * ROPE_PAIRS - 1)) < ROPE_PAIRS

    def norm_rope(x, gain):
        y = x * lax.rsqrt(_seg_sum(x * x, ones) * (1.0 / HEAD_DIM) + NORM_EPS) * gain
        partner = jnp.where(first_half, pltpu.roll(y, LANES - ROPE_PAIRS, 1), pltpu.roll(y, ROPE_PAIRS, 1))
        return y * cs + partner * sn

    zq = zq_ref[0].astype(F32)
    for pair in range(ATTN_HEADS // 2):
        q2 = norm_rope(zq[:, pair * LANES:(pair + 1) * LANES], gq_ref[...]) * ATTN_SCALE
        oq[0, 2 * pair] = q2[:, :HEAD_DIM].astype(oq.dtype)
        oq[0, 2 * pair + 1] = q2[:, HEAD_DIM:].astype(oq.dtype)
    k2 = norm_rope(zk_ref[0].astype(F32), gk_ref[...])
    v2 = zv_ref[0]
    for h in range(ATTN_KV_HEADS):
        ok[0, h] = k2[:, h * HEAD_DIM:(h + 1) * HEAD_DIM].astype(ok.dtype)
        ov[0, h] = v2[:, h * HEAD_DIM:(h + 1) * HEAD_DIM]


def _attn_prep(zq, zk, zv, q_gain, k_gain, rope_c, rope_s):
    row = lambda b, i: (b, i, 0)
    hrow = lambda b, i: (b, 0, i, 0)
    tile2 = lambda g: jnp.tile(g.reshape(1, HEAD_DIM), (1, 2))
    return pl.pallas_call(
        _attn_prep_kernel,
        grid=(BATCH, N_TILES),
        in_specs=[pl.BlockSpec((1, TM, ATTN_WIDTH), row),
                  pl.BlockSpec((1, TM, ATTN_KV_WIDTH), row),
                  pl.BlockSpec((1, TM, ATTN_KV_WIDTH), row),
                  pl.BlockSpec((1, LANES), lambda b, i: (0, 0)),
                  pl.BlockSpec((1, LANES), lambda b, i: (0, 0)),
                  pl.BlockSpec((TM, LANES), lambda b, i: (i, 0)),
                  pl.BlockSpec((TM, LANES), lambda b, i: (i, 0))],
        out_specs=[pl.BlockSpec((1, ATTN_HEADS, TM, HEAD_DIM), hrow),
                   pl.BlockSpec((1, ATTN_KV_HEADS, TM, HEAD_DIM), hrow),
                   pl.BlockSpec((1, ATTN_KV_HEADS, TM, HEAD_DIM), hrow)],
        out_shape=[jax.ShapeDtypeStruct((BATCH, ATTN_HEADS, TT, HEAD_DIM), BF16),
                   jax.ShapeDtypeStruct((BATCH, ATTN_KV_HEADS, TT, HEAD_DIM), BF16),
                   jax.ShapeDtypeStruct((BATCH, ATTN_KV_HEADS, TT, HEAD_DIM), BF16)],
        compiler_params=_cparams("parallel", "arbitrary"),
        name="attn_prep",
    )(zq, zk, zv, tile2(q_gain), tile2(k_gain), rope_c, rope_s)


ATTN_TQ = 128
N_CTX_QTILES = CTX_LEN // ATTN_TQ


def _attn_kernel(q_ref, k_ref, v_ref, o_ref, *, tile_offset):
    q = q_ref[0].reshape(ATTN_GROUP * ATTN_TQ, HEAD_DIM)

    def attend(k, v):
        s = lax.dot_general(q, k, (((1,), (1,)), ((), ())), preferred_element_type=F32)
        p = jnp.exp(s - jnp.max(s, axis=-1, keepdims=True))
        o = jnp.dot(p.astype(BF16), v, preferred_element_type=F32) / jnp.sum(p, axis=-1, keepdims=True)
        o_ref[0] = o.reshape(ATTN_GROUP, ATTN_TQ, HEAD_DIM).astype(o_ref.dtype)

    if tile_offset == 0:
        i = pl.program_id(2)

        @pl.when(i < N_CTX_QTILES)
        def _():
            attend(k_ref[0, 0, :CTX_LEN, :], v_ref[0, 0, :CTX_LEN, :])

        @pl.when(i >= N_CTX_QTILES)
        def _():
            attend(k_ref[0, 0], v_ref[0, 0])
    else:
        attend(k_ref[0, 0], v_ref[0, 0])


def _attention(qh, kh, vh, with_ctx):
    off = 0 if with_ctx else N_CTX_QTILES
    nq = TT // ATTN_TQ - off
    return pl.pallas_call(
        functools.partial(_attn_kernel, tile_offset=off),
        grid=(BATCH, ATTN_KV_HEADS, nq),
        in_specs=[pl.BlockSpec((1, ATTN_GROUP, ATTN_TQ, HEAD_DIM), lambda b, g, i: (b, g, i + off, 0)),
                  pl.BlockSpec((1, 1, TT, HEAD_DIM), lambda b, g, i: (b, g, 0, 0)),
                  pl.BlockSpec((1, 1, TT, HEAD_DIM), lambda b, g, i: (b, g, 0, 0))],
        out_specs=pl.BlockSpec((1, ATTN_GROUP, ATTN_TQ, HEAD_DIM), lambda b, g, i: (b, g, i, 0)),
        out_shape=jax.ShapeDtypeStruct((BATCH, ATTN_HEADS, nq * ATTN_TQ, HEAD_DIM), BF16),
        compiler_params=_cparams("parallel", "parallel", "arbitrary"),
        name="attention",
    )(qh, kh, vh)


def _mix_kernel(x_ref, mod_ref, o_dn_ref, zgate_ref, attn_ref, zu_ref, zv_ref, dn_g_ref, sgu_g_ref, sgu_w_ref,
                sgu_b_ref, w_out_ref, g2_ref, *rest, moe):
    if moe:
        rw_ref, rb_ref, ox, oh, ologit = rest
    else:
        ox, oh = rest
    ones = _seg_ones()
    m = mod_ref[0]
    o = o_dn_ref[0]
    dn = o * lax.rsqrt(_seg_sum(o * o, ones) * (1.0 / HEAD_DIM) + NORM_EPS) * dn_g_ref[...]
    dn = dn * jax.nn.silu(zgate_ref[0].astype(F32))
    at = jnp.concatenate([attn_ref[0, h] for h in range(ATTN_HEADS)], axis=1)
    u = jax.nn.gelu(zu_ref[0].astype(F32))
    v = jax.nn.gelu(zv_ref[0].astype(F32))
    v = (v * lax.rsqrt(_seg_sum(v * v, ones) * (1.0 / HEAD_DIM) + NORM_EPS) * sgu_g_ref[...]).astype(BF16)
    gd = MLP_WIDTH // MLP_GROUPS
    chunks = []
    for ci in range(TM // MLP_CHUNK):
        vc = v[ci * MLP_CHUNK:(ci + 1) * MLP_CHUNK]
        mixed = jnp.concatenate(
            [jnp.dot(sgu_w_ref[g], vc[:, g * gd:(g + 1) * gd], preferred_element_type=F32)
             for g in range(MLP_GROUPS)], axis=1)
        chunks.append(mixed + sgu_b_ref[...])
    sg = u * jnp.concatenate(chunks, axis=0)
    mixed_all = jnp.concatenate([dn.astype(BF16), at, sg.astype(BF16)], axis=1)
    y = jnp.dot(mixed_all, w_out_ref[...], preferred_element_type=F32)
    x = x_ref[0] + m[2:3] * y
    ox[0] = x
    h = _ada_norm(x, g2_ref[...], m[3:4], m[4:5])
    oh[0] = h.astype(oh.dtype)
    if moe:
        ologit[0] = jnp.dot(h, rw_ref[...], precision=HI, preferred_element_type=F32) + rb_ref[...]


def _mix(x_all, mod_l, o_dn, zgate, attn, zu, zv2, dn_g, sgu_g, sgu_w, sgu_b, w_out, gain2, router=None):
    moe = router is not None
    off = 1 if moe else 0
    nt = N_TILES - off
    rows = TT - off * TM
    row = lambda b, i: (b, i + off, 0)
    orow = lambda b, i: (b, i, 0)
    const2 = lambda b, i: (0, 0)
    gd = MLP_WIDTH // MLP_GROUPS
    sgu_b_x = jnp.repeat(sgu_b.T, gd, axis=1)
    in_specs = [pl.BlockSpec((1, TM, D_MODEL), row),
                pl.BlockSpec((1, 6, D_MODEL), lambda b, i: _mod_index(b, i + off)),
                pl.BlockSpec((1, TM, DN_WIDTH), row),
                pl.BlockSpec((1, TM, DN_WIDTH), row),
                pl.BlockSpec((1, ATTN_HEADS, TM, HEAD_DIM), lambda b, i: (b, 0, i, 0)),
                pl.BlockSpec((1, TM, MLP_WIDTH), row),
                pl.BlockSpec((1, TM, MLP_WIDTH), row),
                pl.BlockSpec((1, DN_WIDTH), const2),
                pl.BlockSpec((1, MLP_WIDTH), const2),
                pl.BlockSpec((MLP_GROUPS, MLP_CHUNK, MLP_CHUNK), lambda b, i: (0, 0, 0)),
                pl.BlockSpec((MLP_CHUNK, MLP_WIDTH), const2),
                pl.BlockSpec((D_MIX, D_MODEL), const2),
                pl.BlockSpec((1, D_MODEL), const2)]
    args = [x_all, mod_l, o_dn, zgate, attn, zu, zv2, jnp.tile(dn_g.reshape(1, HEAD_DIM), (1, DN_HEADS)),
            sgu_g.reshape(1, MLP_WIDTH), sgu_w.astype(BF16), sgu_b_x, w_out, gain2.reshape(1, D_MODEL)]
    out_specs = [pl.BlockSpec((1, TM, D_MODEL), orow), pl.BlockSpec((1, TM, D_MODEL), orow)]
    out_shape = [jax.ShapeDtypeStruct((BATCH, rows, D_MODEL), F32),
                 jax.ShapeDtypeStruct((BATCH, rows, D_MODEL), F32 if moe else BF16)]
    if moe:
        rw, rb = router
        in_specs += [pl.BlockSpec((D_MODEL, LANES), const2), pl.BlockSpec((1, LANES), const2)]
        args += [jnp.pad(rw, ((0, 0), (0, LANES - MOE_EXPERTS))),
                 jnp.pad(rb.reshape(1, MOE_EXPERTS), ((0, 0), (0, LANES - MOE_EXPERTS)))]
        out_specs.append(pl.BlockSpec((1, TM, LANES), orow))
        out_shape.append(jax.ShapeDtypeStruct((BATCH, rows, LANES), F32))
    return pl.pallas_call(
        functools.partial(_mix_kernel, moe=moe),
        grid=(BATCH, nt),
        in_specs=in_specs, out_specs=out_specs, out_shape=out_shape,
        compiler_params=_cparams("parallel", "arbitrary"),
        name="mix_moe" if moe else "mix",
    )(*args)


FFN_SPLIT = 2
FFN_BLK = D_FF // FFN_SPLIT


def _ffn_kernel(x_ref, h_ref, mod_ref, w1_ref, w3_ref, w2_ref, o_ref):
    h = h_ref[0]
    y = None
    for c in range(FFN_SPLIT):
        sl = slice(c * FFN_BLK, (c + 1) * FFN_BLK)
        a = jnp.dot(h, w1_ref[:, sl], preferred_element_type=F32)
        b = jnp.dot(h, w3_ref[:, sl], preferred_element_type=F32)
        part = jnp.dot((jax.nn.silu(a) * b).astype(BF16), w2_ref[sl, :], preferred_element_type=F32)
        y = part if y is None else y + part
    o_ref[0] = x_ref[0] + mod_ref[0][5:6] * y


def _ffn(x_all, h_all, mod_l, w1, w3, w2):
    row = lambda b, i: (b, i, 0)
    const2 = lambda b, i: (0, 0)
    return pl.pallas_call(
        _ffn_kernel,
        grid=(BATCH, N_TILES),
        in_specs=[pl.BlockSpec((1, TM, D_MODEL), row),
                  pl.BlockSpec((1, TM, D_MODEL), row),
                  pl.BlockSpec((1, 6, D_MODEL), _mod_index),
                  pl.BlockSpec((D_MODEL, D_FF), const2),
                  pl.BlockSpec((D_MODEL, D_FF), const2),
                  pl.BlockSpec((D_FF, D_MODEL), const2)],
        out_specs=pl.BlockSpec((1, TM, D_MODEL), row),
        out_shape=jax.ShapeDtypeStruct((BATCH, TT, D_MODEL), F32),
        compiler_params=_cparams("parallel", "arbitrary"),
        name="ffn",
    )(x_all, h_all, mod_l, w1, w3, w2)


def _route_kernel(logit_ref, dest_ref, gate_ref, count_ref, run_scr):
    phase, t = pl.program_id(0), pl.program_id(1)

    @pl.when((phase == 0) & (t == 0))
    def _():
        run_scr[...] = jnp.zeros_like(run_scr)

    lane = lax.broadcasted_iota(jnp.int32, (ROUTE_TILE, LANES), 1).astype(F32)
    logits = jnp.where(lane < MOE_EXPERTS, logit_ref[...], -jnp.inf)
    m1 = jnp.max(logits, axis=-1, keepdims=True)
    e1 = jnp.min(jnp.where(logits == m1, lane, float(LANES)), axis=-1, keepdims=True)
    rest = jnp.where(lane == e1, -jnp.inf, logits)
    m2 = jnp.max(rest, axis=-1, keepdims=True)
    e2 = jnp.min(jnp.where(rest == m2, lane, float(LANES)), axis=-1, keepdims=True)
    hot1 = (lane == e1).astype(F32)
    hot2 = (lane == e2).astype(F32)
    hot = hot1 + hot2
    tile_count = jnp.sum(hot, axis=0, keepdims=True)

    @pl.when(phase == 0)
    def _():
        run_scr[0:1] = run_scr[0:1] + tile_count

    @pl.when((phase == 1) & (t == 0))
    def _():
        counts = jnp.broadcast_to(run_scr[0:1], (8, LANES))
        count_ref[...] = counts.astype(jnp.int32)
        padded = jnp.ceil(counts * (1.0 / MOE_BLOCK)) * MOE_BLOCK
        ei = lax.broadcasted_iota(jnp.int32, (LANES, LANES), 0)
        ej = lax.broadcasted_iota(jnp.int32, (LANES, LANES), 1)
        before = (ei < ej).astype(F32)
        run_scr[1:2] = jnp.dot(padded, before, precision=HI, preferred_element_type=F32)[0:1]

    @pl.when(phase == 1)
    def _():
        ri = lax.broadcasted_iota(jnp.int32, (ROUTE_TILE, ROUTE_TILE), 0)
        rj = lax.broadcasted_iota(jnp.int32, (ROUTE_TILE, ROUTE_TILE), 1)
        earlier = (ri > rj).astype(BF16)
        within = jnp.dot(earlier, hot.astype(BF16), preferred_element_type=F32)
        pos = within + run_scr[1:2]
        d1 = jnp.sum(pos * hot1, axis=-1, keepdims=True)
        d2 = jnp.sum(pos * hot2, axis=-1, keepdims=True)
        dest_ref[...] = jnp.concatenate([d1, d2], axis=1).astype(jnp.int32)
        w2 = jnp.exp(m2 - m1)
        gate_ref[...] = jnp.concatenate([1.0 / (1.0 + w2), w2 / (1.0 + w2)], axis=1)
        run_scr[1:2] = run_scr[1:2] + tile_count


def _route(logits):
    nt = N_LAT // ROUTE_TILE
    return pl.pallas_call(
        _route_kernel,
        grid=(2, nt),
        in_specs=[pl.BlockSpec((ROUTE_TILE, LANES), lambda p, t: (t, 0))],
        out_specs=[pl.BlockSpec((ROUTE_TILE, MOE_TOP_K), lambda p, t: (t * p, 0)),
                   pl.BlockSpec((ROUTE_TILE, MOE_TOP_K), lambda p, t: (t * p, 0)),
                   pl.BlockSpec((8, LANES), lambda p, t: (0, 0))],
        out_shape=[jax.ShapeDtypeStruct((N_LAT, MOE_TOP_K), jnp.int32),
                   jax.ShapeDtypeStruct((N_LAT, MOE_TOP_K), F32),
                   jax.ShapeDtypeStruct((8, LANES), jnp.int32)],
        scratch_shapes=[pltpu.VMEM((8, LANES), F32)],
        compiler_params=_cparams("arbitrary", "arbitrary"),
        name="moe_route",
    )(logits)


def _dispatch_kernel(dest_ref, h_ref, xs_in_ref, xs_ref, sem):
    del xs_in_ref
    base = pl.program_id(0) * TM

    def copy(r, k):
        return pltpu.make_async_copy(h_ref.at[pl.ds(r, 1)],
                                     xs_ref.at[pl.ds(dest_ref[(base + r) * MOE_TOP_K + k], 1)], sem)

    def start(r, c):
        for k in range(MOE_TOP_K):
            copy(r, k).start()
        return c

    def wait(r, c):
        for k in range(MOE_TOP_K):
            copy(r, k).wait()
        return c

    lax.fori_loop(0, TM, start, 0)
    lax.fori_loop(0, TM, wait, 0)


def _dispatch(dest_flat, h_lat):
    grid_spec = pltpu.PrefetchScalarGridSpec(
        num_scalar_prefetch=1,
        grid=(N_LAT // TM,),
        in_specs=[pl.BlockSpec((TM, D_MODEL), lambda i, d: (i, 0)),
                  pl.BlockSpec(memory_space=pl.ANY)],
        out_specs=pl.BlockSpec(memory_space=pl.ANY),
        scratch_shapes=[pltpu.SemaphoreType.DMA(())])
    return pl.pallas_call(
        _dispatch_kernel,
        grid_spec=grid_spec,
        out_shape=jax.ShapeDtypeStruct((MOE_ROWS, D_MODEL), F32),
        input_output_aliases={2: 0},
        compiler_params=_cparams("arbitrary"),
        name="moe_dispatch",
    )(dest_flat, h_lat, jnp.zeros((MOE_ROWS, D_MODEL), F32))


def _expert_kernel(be_ref, xs_ref, w1_ref, w3_ref, w2_ref, *rest, first):
    del be_ref
    if first:
        (o_ref,) = rest
    else:
        acc_ref, o_ref = rest
    x = xs_ref[...].astype(BF16)
    a = jnp.dot(x, w1_ref[0], preferred_element_type=F32)
    b = jnp.dot(x, w3_ref[0], preferred_element_type=F32)
    y = jnp.dot((jax.nn.silu(a) * b).astype(BF16), w2_ref[0], preferred_element_type=F32)
    o_ref[...] = y if first else acc_ref[...] + y


def _experts(block_expert, xs, w1, w3, w2):
    yb = None
    for c in range(MOE_FF_SPLIT):
        first = c == 0
        in_specs = [pl.BlockSpec((MOE_BLOCK, D_MODEL), lambda j, be: (j, 0)),
                    pl.BlockSpec((1, D_MODEL, MOE_FF_BLK), lambda j, be, c=c: (be[j], 0, c)),
                    pl.BlockSpec((1, D_MODEL, MOE_FF_BLK), lambda j, be, c=c: (be[j], 0, c)),
                    pl.BlockSpec((1, MOE_FF_BLK, D_MODEL), lambda j, be, c=c: (be[j], c, 0))]
        args = [block_expert, xs, w1, w3, w2]
        if not first:
            in_specs.append(pl.BlockSpec((MOE_BLOCK, D_MODEL), lambda j, be: (j, 0)))
            args.append(yb)
        yb = pl.pallas_call(
            functools.partial(_expert_kernel, first=first),
            grid_spec=pltpu.PrefetchScalarGridSpec(
                num_scalar_prefetch=1, grid=(MOE_N_BLOCKS,), in_specs=in_specs,
                out_specs=pl.BlockSpec((MOE_BLOCK, D_MODEL), lambda j, be: (j, 0))),
            out_shape=jax.ShapeDtypeStruct((MOE_ROWS, D_MODEL), F32),
            input_output_aliases={} if first else {5: 0},
            compiler_params=_cparams("arbitrary"),
            name="moe_experts_%d" % c,
        )(*args)
    return yb


def _combine_kernel(dest_ref, x_ref, gate_ref, mod_ref, g_ref, yb_ref, o_ref, buf, sem):
    base = (pl.program_id(0) * (SEQ // TM) + pl.program_id(1)) * TM

    def copy(r, k):
        return pltpu.make_async_copy(yb_ref.at[pl.ds(dest_ref[(base + r) * MOE_TOP_K + k], 1)],
                                     buf.at[k, pl.ds(r, 1)], sem)

    def start(r, c):
        for k in range(MOE_TOP_K):
            copy(r, k).start()
        return c

    def wait(r, c):
        for k in range(MOE_TOP_K):
            copy(r, k).wait()
        return c

    lax.fori_loop(0, TM, start, 0)
    lax.fori_loop(0, TM, wait, 0)
    gates = gate_ref[0]
    y = buf[0] * gates[:, 0:1] + buf[1] * gates[:, 1:2]
    x = x_ref[0] + mod_ref[0][5:6] * y
    o_ref[0] = x * lax.rsqrt(jnp.mean(x * x, axis=-1, keepdims=True) + NORM_EPS) * g_ref[...]


def _combine(dest_flat, x_lat, gates, mod_l, final_g, yb):
    grid_spec = pltpu.PrefetchScalarGridSpec(
        num_scalar_prefetch=1,
        grid=(BATCH, SEQ // TM),
        in_specs=[pl.BlockSpec((1, TM, D_MODEL), lambda b, i, d: (b, i, 0)),
                  pl.BlockSpec((1, TM, MOE_TOP_K), lambda b, i, d: (b, i, 0)),
                  pl.BlockSpec((1, 6, D_MODEL), lambda b, i, d: (b, 0, 0)),
                  pl.BlockSpec((1, D_MODEL), lambda b, i, d: (0, 0)),
                  pl.BlockSpec(memory_space=pl.ANY)],
        out_specs=pl.BlockSpec((1, TM, D_MODEL), lambda b, i, d: (b, i, 0)),
        scratch_shapes=[pltpu.VMEM((MOE_TOP_K, TM, D_MODEL), F32), pltpu.SemaphoreType.DMA(())])
    return pl.pallas_call(
        _combine_kernel,
        grid_spec=grid_spec,
        out_shape=jax.ShapeDtypeStruct((BATCH, SEQ, D_MODEL), F32),
        compiler_params=_cparams("arbitrary", "arbitrary"),
        name="moe_combine",
    )(dest_flat, x_lat, gates.reshape(BATCH, SEQ, MOE_TOP_K), mod_l, final_g.reshape(1, D_MODEL), yb)


def _reorder_w_in(w):
    s = np.cumsum((3 * DN_WIDTH, DN_WIDTH, 2 * DN_HEADS, 2 * DN_HEADS, ATTN_WIDTH, ATTN_KV_WIDTH, ATTN_KV_WIDTH,
                   MLP_WIDTH, MLP_WIDTH)).tolist()
    ba = w[:, s[1]:s[3]]
    return jnp.concatenate([w[:, :s[1]], w[:, s[3]:], ba,
                            jnp.zeros((D_MODEL, LANES - 4 * DN_HEADS), w.dtype)], axis=1).astype(BF16)


def kernel(x, c, ctx, c_ctx, mod_w, mod_b, norm1_g, norm2_g, w_in, conv_w, dn_a_log, dn_dt_bias, dn_norm_g,
           q_norm_g, k_norm_g, sgu_norm_g, sgu_w, sgu_b, w_out, ffn_w1, ffn_w3, ffn_w2, router_w, router_b,
           moe_w1, moe_w3, moe_w2, final_norm_g):
    assert DEPTH == 2 and x.shape == (BATCH, SEQ, D_MODEL) and ctx.shape == (BATCH, CTX_LEN, D_MODEL)
    cond8 = jnp.concatenate([c, c_ctx[None], jnp.zeros((8 - BATCH - 1, D_MODEL), F32)], axis=0)
    mod = _modulation(cond8, mod_w, mod_b)
    rope_c, rope_s = _rope_tables()
    x_all = jnp.concatenate([ctx, x], axis=1)
    for layer in range(DEPTH):
        last = layer == DEPTH - 1
        zqkv, zgate, zq, zk, zv, zu, zv2, zba = _in_proj(x_all, mod[layer], norm1_g[layer],
                                                         _reorder_w_in(w_in[layer]))
        qkv = _dn_prep(zqkv, conv_w[layer])
        o_dn = _deltanet(qkv, zba, dn_a_log[layer], dn_dt_bias[layer])
        qh, kh, vh = _attn_prep(zq, zk, zv, q_norm_g[layer], k_norm_g[layer], rope_c, rope_s)
        attn = _attention(qh, kh, vh, with_ctx=not last)
        mix_args = (x_all, mod[layer], o_dn, zgate, attn, zu, zv2, dn_norm_g[layer], sgu_norm_g[layer],
                    sgu_w[layer], sgu_b[layer], w_out[layer].astype(BF16), norm2_g[layer])
        if not last:
            i = layer // 2
            x_mid, h_mid = _mix(*mix_args)
            x_all = _ffn(x_mid, h_mid, mod[layer], ffn_w1[i].astype(BF16), ffn_w3[i].astype(BF16),
                         ffn_w2[i].astype(BF16))
        else:
            i = layer // 2
            x_lat, h_lat, logits = _mix(*mix_args, router=(router_w[i], router_b[i]))
            dest, gates, counts = _route(logits.reshape(N_LAT, LANES))
            padded = (counts[0, :MOE_EXPERTS] + MOE_BLOCK - 1) // MOE_BLOCK * MOE_BLOCK
            pad_ends = jnp.cumsum(padded)
            block_expert = jnp.minimum(
                jnp.sum(pad_ends[None, :] <= (jnp.arange(MOE_N_BLOCKS) * MOE_BLOCK)[:, None], axis=1),
                MOE_EXPERTS - 1).astype(jnp.int32)
            dest_flat = dest.reshape(N_ASSIGN)
            xs = _dispatch(dest_flat, h_lat.reshape(N_LAT, D_MODEL))
            yb = _experts(block_expert, xs, moe_w1[i].astype(BF16), moe_w3[i].astype(BF16),
                          moe_w2[i].astype(BF16))
            return _combine(dest_flat, x_lat, gates, mod[layer], final_norm_g, yb)
```

```python
import functools

import jax
import jax.numpy as jnp
import numpy as np
from jax import lax
from jax.experimental import pallas as pl
from jax.experimental.pallas import tpu as pltpu

D_MODEL = 1024
BATCH = 4
SEQ = 4096
DEPTH = 2
GRID_W = 64
CTX_LEN = 256
HEAD_DIM = 64
DN_HEADS = 6
ATTN_HEADS = 6
ATTN_KV_HEADS = 2
ATTN_GROUP = ATTN_HEADS // ATTN_KV_HEADS
MLP_GROUPS = 4
DN_WIDTH = DN_HEADS * HEAD_DIM
ATTN_WIDTH = ATTN_HEADS * HEAD_DIM
ATTN_KV_WIDTH = ATTN_KV_HEADS * HEAD_DIM
MLP_WIDTH = MLP_GROUPS * HEAD_DIM
D_MIX = DN_WIDTH + ATTN_WIDTH + MLP_WIDTH
CONV_K = 3
DN_CHUNK = 64
ATTN_SCALE = HEAD_DIM ** -0.5
MLP_CHUNK = 128
ROPE_THETA = 10000.0
ROPE_AXIS_DIM = HEAD_DIM // 2
ROPE_PAIRS = ROPE_AXIS_DIM // 2
D_FF = 2816
MOE_EXPERTS = 8
MOE_TOP_K = 2
MOE_D_FF = 3584
MOE_BLOCK = 256
NORM_EPS = 1e-6

LANES = 128
TT = CTX_LEN + SEQ
TM = 256
N_TILES = TT // TM
CTX_ROW = BATCH
N_DN_STEPS = TT // DN_CHUNK
N_CTX_CHUNKS = CTX_LEN // DN_CHUNK
IN_PAD = 3 * DN_WIDTH + DN_WIDTH + ATTN_WIDTH + 2 * ATTN_KV_WIDTH + 2 * MLP_WIDTH + LANES
N_LAT = BATCH * SEQ
N_ASSIGN = N_LAT * MOE_TOP_K
MOE_N_BLOCKS = -(-(N_ASSIGN + MOE_EXPERTS * (MOE_BLOCK - 1)) // MOE_BLOCK)
MOE_ROWS = MOE_N_BLOCKS * MOE_BLOCK
MOE_FF_SPLIT = 2
MOE_FF_BLK = MOE_D_FF // MOE_FF_SPLIT
ROUTE_TILE = 512
VMEM_LIMIT = 56 * 2 ** 20

F32 = jnp.float32
BF16 = jnp.bfloat16
HI = lax.Precision.HIGHEST


def _cparams(*sem):
    return pltpu.CompilerParams(dimension_semantics=sem, vmem_limit_bytes=VMEM_LIMIT)


def _bdot(a, b):
    return jnp.dot(a.astype(BF16), b.astype(BF16), preferred_element_type=F32)


def _bdot_nt(a, b):
    return lax.dot_general(a.astype(BF16), b.astype(BF16), (((1,), (1,)), ((), ())), preferred_element_type=F32)


def _bdot_tn(a, b):
    return lax.dot_general(a.astype(BF16), b.astype(BF16), (((0,), (0,)), ((), ())), preferred_element_type=F32)


def _seg_ones():
    r = lax.shift_right_logical(lax.broadcasted_iota(jnp.int32, (LANES, LANES), 0), 6)
    c = lax.shift_right_logical(lax.broadcasted_iota(jnp.int32, (LANES, LANES), 1), 6)
    return (r == c).astype(F32)


def _seg_sum(y, ones):
    parts = [jnp.dot(y[:, i:i + LANES], ones, precision=HI, preferred_element_type=F32)
             for i in range(0, y.shape[-1], LANES)]
    return parts[0] if len(parts) == 1 else jnp.concatenate(parts, axis=-1)


def _softplus(x):
    return jnp.maximum(x, 0.0) + jnp.log1p(jnp.exp(-jnp.abs(x)))


def _ada_norm(x, gain, shift, scale):
    y = x * lax.rsqrt(jnp.mean(x * x, axis=-1, keepdims=True) + NORM_EPS) * gain
    return y * (1.0 + scale) + shift


def _mod_index(b, i):
    return (jnp.where(i == 0, CTX_ROW, b), 0, 0)


def _mod_kernel(c_ref, w_ref, b_ref, o_ref):
    cond = jax.nn.silu(c_ref[...])
    o_ref[0] = jnp.dot(cond, w_ref[0], precision=HI, preferred_element_type=F32) + b_ref[0]


def _modulation(cond8, mod_w, mod_b):
    nblk = 4
    bn = 6 * D_MODEL // nblk
    out = pl.pallas_call(
        _mod_kernel,
        grid=(DEPTH, nblk),
        in_specs=[pl.BlockSpec((8, D_MODEL), lambda l, j: (0, 0)),
                  pl.BlockSpec((1, D_MODEL, bn), lambda l, j: (l, 0, j)),
                  pl.BlockSpec((1, 1, bn), lambda l, j: (l, 0, j))],
        out_specs=pl.BlockSpec((1, 8, bn), lambda l, j: (l, 0, j)),
        out_shape=jax.ShapeDtypeStruct((DEPTH, 8, 6 * D_MODEL), F32),
        compiler_params=_cparams("arbitrary", "arbitrary"),
        name="modulation",
    )(cond8, mod_w, mod_b.reshape(DEPTH, 1, 6 * D_MODEL))
    return out.reshape(DEPTH, 8, 6, D_MODEL)


_IN_SPLITS = (3 * DN_WIDTH, DN_WIDTH, ATTN_WIDTH, ATTN_KV_WIDTH, ATTN_KV_WIDTH, MLP_WIDTH, MLP_WIDTH, LANES)


def _in_proj_kernel(x_ref, mod_ref, g_ref, w_ref, oqkv, ogate, oq, ok, ov, ou, ov2, oba):
    m = mod_ref[0]
    h = _ada_norm(x_ref[0], g_ref[...], m[0:1], m[1:2]).astype(BF16)
    z = jnp.dot(h, w_ref[...], preferred_element_type=F32)
    off = 0
    for ref, width in zip((oqkv, ogate, oq, ok, ov, ou, ov2, oba), _IN_SPLITS):
        ref[0] = z[:, off:off + width].astype(ref.dtype)
        off += width


def _in_proj(x_all, mod_l, gain, w_in_r):
    dts = (BF16,) * 7 + (F32,)
    row = lambda b, i: (b, i, 0)
    return pl.pallas_call(
        _in_proj_kernel,
        grid=(BATCH, N_TILES),
        in_specs=[pl.BlockSpec((1, TM, D_MODEL), row),
                  pl.BlockSpec((1, 6, D_MODEL), _mod_index),
                  pl.BlockSpec((1, D_MODEL), lambda b, i: (0, 0)),
                  pl.BlockSpec((D_MODEL, IN_PAD), lambda b, i: (0, 0))],
        out_specs=[pl.BlockSpec((1, TM, w), row) for w in _IN_SPLITS],
        out_shape=[jax.ShapeDtypeStruct((BATCH, TT, w), dt) for w, dt in zip(_IN_SPLITS, dts)],
        compiler_params=_cparams("parallel", "arbitrary"),
        name="in_proj",
    )(x_all, mod_l, gain.reshape(1, D_MODEL), w_in_r)


def _dn_prep_kernel(z_ref, w_ref, o_ref):
    j = pl.program_id(1)
    z = z_ref[0].astype(F32)
    w = w_ref[...]
    row = lax.broadcasted_iota(jnp.int32, (TT, 1), 0)
    first = (row == 0) | (row == CTX_LEN)
    last = (row == CTX_LEN - 1) | (row == TT - 1)
    zp = jnp.where(first, 0.0, pltpu.roll(z, 1, 0))
    zn = jnp.where(last, 0.0, pltpu.roll(z, TT - 1, 0))
    y = jax.nn.silu(w[0:1] * zp + w[1:2] * z + w[2:3] * zn)
    n_qk = 2 * DN_WIDTH // LANES
    n_q = DN_WIDTH // LANES

    @pl.when(j < n_qk)
    def _():
        inv = lax.rsqrt(_seg_sum(y * y, _seg_ones()) + NORM_EPS)
        o_ref[0] = (y * inv * jnp.where(j < n_q, HEAD_DIM ** -0.5, 1.0)).astype(o_ref.dtype)

    @pl.when(j >= n_qk)
    def _():
        o_ref[0] = y.astype(o_ref.dtype)


def _dn_prep(zqkv, conv_w):
    nb = 3 * DN_WIDTH // LANES
    return pl.pallas_call(
        _dn_prep_kernel,
        grid=(BATCH, nb),
        in_specs=[pl.BlockSpec((1, TT, LANES), lambda b, j: (b, 0, j)),
                  pl.BlockSpec((CONV_K, LANES), lambda b, j: (0, j))],
        out_specs=pl.BlockSpec((1, TT, LANES), lambda b, j: (b, 0, j)),
        out_shape=jax.ShapeDtypeStruct((BATCH, TT, 3 * DN_WIDTH), BF16),
        compiler_params=_cparams("parallel", "arbitrary"),
        name="dn_prep",
    )(zqkv, conv_w)


DN_PAIRS = DN_HEADS // 2
DN_SUB = 2
LOG2_CHUNK = DN_CHUNK.bit_length() - 1


def _dn_kernel(qkv_ref, zba_ref, a_ref, dt_ref, o_ref, s_scr):
    C, P2 = DN_CHUNK, 2 * DN_CHUNK
    o_ref[...] = jnp.zeros_like(o_ref)
    s_scr[...] = jnp.zeros_like(s_scr)
    neg_decay_rate = -jnp.exp(a_ref[...])
    dt_bias = dt_ref[...]
    lo = lax.broadcasted_iota(jnp.int32, (1, P2), 1) < C

    def stack(a):
        return jnp.concatenate([jnp.where(lo, a, 0.0), jnp.where(lo, 0.0, a)], axis=0)

    def step(i, carry):
        ri = lax.broadcasted_iota(jnp.int32, (P2, P2), 0)
        ci = lax.broadcasted_iota(jnp.int32, (P2, P2), 1)
        ti = lax.broadcasted_iota(jnp.int32, (C, C), 0)
        tj = lax.broadcasted_iota(jnp.int32, (C, C), 1)
        same = lambda sh: lax.shift_right_logical(ri, sh) == lax.shift_right_logical(ci, sh)
        same_head = same(LOG2_CHUNK)
        eye = (ri == ci).astype(F32)

        ch = []
        for sub in range(DN_SUB):
            s = i * DN_SUB + sub
            chunk_of = (s, jnp.where(s < N_CTX_CHUNKS, N_CTX_CHUNKS - 1 - s, N_DN_STEPS + N_CTX_CHUNKS - 1 - s))
            for d in range(2):
                r0 = pl.multiple_of(chunk_of[d] * C, C)
                zba = zba_ref[0, pl.ds(r0, C), :]
                beta_all = jax.nn.sigmoid(zba)
                g_all = neg_decay_rate * _softplus(zba + dt_bias)
                tri = (ti >= tj) if d == 0 else (ti <= tj)
                gc_all = jnp.dot(tri.astype(F32), g_all, precision=HI, preferred_element_type=F32)
                gc_t = jnp.concatenate([gc_all, gc_all], axis=0).T
                last = C - 1 if d == 0 else 0
                for p in range(DN_PAIRS):
                    lb = (d * DN_HEADS + 2 * p, d * DN_HEADS + 2 * p + 1)
                    la = (2 * DN_HEADS + lb[0], 2 * DN_HEADS + lb[1])
                    col = lambda t, l: t[:, l:l + 1]
                    q2 = qkv_ref[0, pl.ds(r0, C), p * LANES:(p + 1) * LANES].astype(F32)
                    k2 = qkv_ref[0, pl.ds(r0, C), DN_WIDTH + p * LANES:DN_WIDTH + (p + 1) * LANES].astype(F32)
                    v2 = qkv_ref[0, pl.ds(r0, C), 2 * DN_WIDTH + p * LANES:2 * DN_WIDTH + (p + 1) * LANES].astype(F32)
                    beta2 = jnp.where(lo, col(beta_all, lb[0]), col(beta_all, lb[1]))
                    gc2 = jnp.where(lo, col(gc_all, la[0]), col(gc_all, la[1]))
                    gcol = jnp.concatenate([col(gc_all, la[0]), col(gc_all, la[1])], axis=0)
                    grow = jnp.where(lo, gc_t[la[0]:la[0] + 1, :], gc_t[la[1]:la[1] + 1, :])
                    g_last = jnp.where(lo, gc_all[last:last + 1, la[0]:la[0] + 1],
                                       gc_all[last:last + 1, la[1]:la[1] + 1])
                    e_diff = jnp.exp(gcol - grow)
                    order = (ri >= ci) if d == 0 else (ri <= ci)
                    kb2 = k2 * beta2
                    e_gc = jnp.exp(gc2)
                    ch.append(dict(
                        r0=r0, sub=sub, d=d, p=p, g_last=g_last,
                        dec_incl=jnp.where(same_head & order, e_diff, 0.0),
                        dec_strict=jnp.where(same_head & order & (ri != ci), e_diff, 0.0),
                        kq=jnp.concatenate([stack(kb2), stack(q2)], axis=0).astype(BF16),
                        k=stack(k2).astype(BF16),
                        rhs=jnp.concatenate([stack(v2 * beta2), stack(kb2 * e_gc)], axis=1).astype(BF16),
                        kd=stack(k2 * jnp.exp(g_last - gc2)).astype(BF16),
                        qe=stack(q2 * e_gc)))
        for c in ch:
            kk_qk = lax.dot_general(c["kq"], c["k"], (((1,), (1,)), ((), ())), preferred_element_type=F32)
            c["a"] = kk_qk[:P2] * c["dec_strict"]
            c["attn"] = (kk_qk[P2:] * c["dec_incl"]).astype(BF16)
            c["t"] = eye - jnp.where(same(1), c["a"], 0.0)
        for lvl in range(1, LOG2_CHUNK):
            joins = same(lvl + 1) & ~same(lvl)
            for c in ch:
                c["m"] = _bdot(jnp.where(joins, c["a"], 0.0), c["t"])
            for c in ch:
                c["t"] = c["t"] - _bdot(c["t"], c["m"])
        for c in ch:
            c["uw"] = _bdot(c["t"], c["rhs"]).astype(BF16)
        for c in ch:
            oa = jnp.dot(c["attn"], c["uw"], preferred_element_type=F32)
            c["o0"] = oa[:, :LANES]
            c["qp"] = c["qe"] - oa[:, LANES:]
            c["np"] = lax.dot_general(c["kd"], c["uw"], (((0,), (0,)), ((), ())), preferred_element_type=F32)

        states = [s_scr[j] for j in range(2 * DN_PAIRS)]
        for sub in range(DN_SUB):
            cur = [c for c in ch if c["sub"] == sub]
            for c in cur:
                j = c["d"] * DN_PAIRS + c["p"]
                c["r"] = _bdot(jnp.concatenate([c["qp"], c["np"][:, LANES:]], axis=0), states[j])
            for d in range(2):
                outs = []
                for c in cur:
                    if c["d"] != d:
                        continue
                    j = d * DN_PAIRS + c["p"]
                    o_st = c["o0"] + c["r"][:P2]
                    outs.append(o_st[:C] + o_st[C:])
                    states[j] = states[j] * jnp.exp(c["g_last"]) + c["np"][:, :LANES] - c["r"][P2:]
                    r0 = c["r0"]
                o_ref[0, pl.ds(r0, C), :] += jnp.concatenate(outs, axis=1)
        for j in range(2 * DN_PAIRS):
            s_scr[j] = states[j]
        return carry

    lax.fori_loop(0, N_DN_STEPS // DN_SUB, step, 0)


def _deltanet(qkv, zba, a_log, dt_bias):
    pad = lambda t: jnp.zeros((1, LANES), F32).at[0, 2 * DN_HEADS:4 * DN_HEADS].set(t.reshape(-1))
    return pl.pallas_call(
        _dn_kernel,
        grid=(BATCH,),
        in_specs=[pl.BlockSpec((1, TT, 3 * DN_WIDTH), lambda b: (b, 0, 0)),
                  pl.BlockSpec((1, TT, LANES), lambda b: (b, 0, 0)),
                  pl.BlockSpec((1, LANES), lambda b: (0, 0)),
                  pl.BlockSpec((1, LANES), lambda b: (0, 0))],
        out_specs=pl.BlockSpec((1, TT, DN_WIDTH), lambda b: (b, 0, 0)),
        out_shape=jax.ShapeDtypeStruct((BATCH, TT, DN_WIDTH), F32),
        scratch_shapes=[pltpu.VMEM((2 * DN_PAIRS, LANES, LANES), F32)],
        compiler_params=_cparams("parallel"),
        name="deltanet",
    )(qkv, zba, pad(a_log), pad(dt_bias))


def _rope_tables():
    rows = SEQ // GRID_W
    row = jnp.repeat(jnp.arange(rows, dtype=F32), GRID_W)
    col = jnp.tile(jnp.arange(GRID_W, dtype=F32), rows)
    inv = ROPE_THETA ** (-2.0 * jnp.arange(ROPE_PAIRS, dtype=F32) / ROPE_AXIS_DIM)
    ang = jnp.stack([row[:, None] * inv, col[:, None] * inv], axis=1)
    cos, sin = jnp.cos(ang), jnp.sin(ang)
    c = jnp.concatenate([cos[:, 0], cos[:, 0], cos[:, 1], cos[:, 1]], axis=-1)
    s = jnp.concatenate([-sin[:, 0], sin[:, 0], -sin[:, 1], sin[:, 1]], axis=-1)
    c = jnp.concatenate([jnp.ones((CTX_LEN, HEAD_DIM), F32), c], axis=0)
    s = jnp.concatenate([jnp.zeros((CTX_LEN, HEAD_DIM), F32), s], axis=0)
    return jnp.tile(c, (1, 2)), jnp.tile(s, (1, 2))


def _attn_prep_kernel(zq_ref, zk_ref, zv_ref, gq_ref, gk_ref, c_ref, s_ref, oq, ok, ov):
    ones = _seg_ones()
    cs, sn = c_ref[...], s_ref[...]
    lane = lax.broadcasted_iota(jnp.int32, (1, LANES), 1)
    first_half = (lane & (2 * ROPE_PAIRS - 1)) < ROPE_PAIRS

    def norm_rope(x, gain):
        y = x * lax.rsqrt(_seg_sum(x * x, ones) * (1.0 / HEAD_DIM) + NORM_EPS) * gain
        partner = jnp.where(first_half, pltpu.roll(y, LANES - ROPE_PAIRS, 1), pltpu.roll(y, ROPE_PAIRS, 1))
        return y * cs + partner * sn

    zq = zq_ref[0].astype(F32)
    for pair in range(ATTN_HEADS // 2):
        q2 = norm_rope(zq[:, pair * LANES:(pair + 1) * LANES], gq_ref[...]) * ATTN_SCALE
        oq[0, 2 * pair] = q2[:, :HEAD_DIM].astype(oq.dtype)
        oq[0, 2 * pair + 1] = q2[:, HEAD_DIM:].astype(oq.dtype)
    k2 = norm_rope(zk_ref[0].astype(F32), gk_ref[...])
    v2 = zv_ref[0]
    for h in range(ATTN_KV_HEADS):
        ok[0, h] = k2[:, h * HEAD_DIM:(h + 1) * HEAD_DIM].astype(ok.dtype)
        ov[0, h] = v2[:, h * HEAD_DIM:(h + 1) * HEAD_DIM]


def _attn_prep(zq, zk, zv, q_gain, k_gain, rope_c, rope_s):
    row = lambda b, i: (b, i, 0)
    hrow = lambda b, i: (b, 0, i, 0)
    tile2 = lambda g: jnp.tile(g.reshape(1, HEAD_DIM), (1, 2))
    return pl.pallas_call(
        _attn_prep_kernel,
        grid=(BATCH, N_TILES),
        in_specs=[pl.BlockSpec((1, TM, ATTN_WIDTH), row),
                  pl.BlockSpec((1, TM, ATTN_KV_WIDTH), row),
                  pl.BlockSpec((1, TM, ATTN_KV_WIDTH), row),
                  pl.BlockSpec((1, LANES), lambda b, i: (0, 0)),
                  pl.BlockSpec((1, LANES), lambda b, i: (0, 0)),
                  pl.BlockSpec((TM, LANES), lambda b, i: (i, 0)),
                  pl.BlockSpec((TM, LANES), lambda b, i: (i, 0))],
        out_specs=[pl.BlockSpec((1, ATTN_HEADS, TM, HEAD_DIM), hrow),
                   pl.BlockSpec((1, ATTN_KV_HEADS, TM, HEAD_DIM), hrow),
                   pl.BlockSpec((1, ATTN_KV_HEADS, TM, HEAD_DIM), hrow)],
        out_shape=[jax.ShapeDtypeStruct((BATCH, ATTN_HEADS, TT, HEAD_DIM), BF16),
                   jax.ShapeDtypeStruct((BATCH, ATTN_KV_HEADS, TT, HEAD_DIM), BF16),
                   jax.ShapeDtypeStruct((BATCH, ATTN_KV_HEADS, TT, HEAD_DIM), BF16)],
        compiler_params=_cparams("parallel", "arbitrary"),
        name="attn_prep",
    )(zq, zk, zv, tile2(q_gain), tile2(k_gain), rope_c, rope_s)


ATTN_TQ = 128
N_CTX_QTILES = CTX_LEN // ATTN_TQ


def _attn_kernel(q_ref, k_ref, v_ref, o_ref, *, tile_offset):
    q = q_ref[0].reshape(ATTN_GROUP * ATTN_TQ, HEAD_DIM)

    def attend(k, v):
        s = lax.dot_general(q, k, (((1,), (1,)), ((), ())), preferred_element_type=F32)
        p = jnp.exp(s - jnp.max(s, axis=-1, keepdims=True))
        o = jnp.dot(p.astype(BF16), v, preferred_element_type=F32) / jnp.sum(p, axis=-1, keepdims=True)
        o_ref[0] = o.reshape(ATTN_GROUP, ATTN_TQ, HEAD_DIM).astype(o_ref.dtype)

    if tile_offset == 0:
        i = pl.program_id(2)

        @pl.when(i < N_CTX_QTILES)
        def _():
            attend(k_ref[0, 0, :CTX_LEN, :], v_ref[0, 0, :CTX_LEN, :])

        @pl.when(i >= N_CTX_QTILES)
        def _():
            attend(k_ref[0, 0], v_ref[0, 0])
    else:
        attend(k_ref[0, 0], v_ref[0, 0])


def _attention(qh, kh, vh, with_ctx):
    off = 0 if with_ctx else N_CTX_QTILES
    nq = TT // ATTN_TQ - off
    return pl.pallas_call(
        functools.partial(_attn_kernel, tile_offset=off),
        grid=(BATCH, ATTN_KV_HEADS, nq),
        in_specs=[pl.BlockSpec((1, ATTN_GROUP, ATTN_TQ, HEAD_DIM), lambda b, g, i: (b, g, i + off, 0)),
                  pl.BlockSpec((1, 1, TT, HEAD_DIM), lambda b, g, i: (b, g, 0, 0)),
                  pl.BlockSpec((1, 1, TT, HEAD_DIM), lambda b, g, i: (b, g, 0, 0))],
        out_specs=pl.BlockSpec((1, ATTN_GROUP, ATTN_TQ, HEAD_DIM), lambda b, g, i: (b, g, i, 0)),
        out_shape=jax.ShapeDtypeStruct((BATCH, ATTN_HEADS, nq * ATTN_TQ, HEAD_DIM), BF16),
        compiler_params=_cparams("parallel", "parallel", "arbitrary"),
        name="attention",
    )(qh, kh, vh)


def _mix_kernel(x_ref, mod_ref, o_dn_ref, zgate_ref, attn_ref, zu_ref, zv_ref, dn_g_ref, sgu_g_ref, sgu_w_ref,
                sgu_b_ref, w_out_ref, g2_ref, *rest, moe):
    if moe:
        rw_ref, rb_ref, ox, oh, ologit = rest
    else:
        ox, oh = rest
    ones = _seg_ones()
    m = mod_ref[0]
    o = o_dn_ref[0]
    dn = o * lax.rsqrt(_seg_sum(o * o, ones) * (1.0 / HEAD_DIM) + NORM_EPS) * dn_g_ref[...]
    dn = dn * jax.nn.silu(zgate_ref[0].astype(F32))
    at = jnp.concatenate([attn_ref[0, h] for h in range(ATTN_HEADS)], axis=1)
    u = jax.nn.gelu(zu_ref[0].astype(F32))
    v = jax.nn.gelu(zv_ref[0].astype(F32))
    v = (v * lax.rsqrt(_seg_sum(v * v, ones) * (1.0 / HEAD_DIM) + NORM_EPS) * sgu_g_ref[...]).astype(BF16)
    gd = MLP_WIDTH // MLP_GROUPS
    chunks = []
    for ci in range(TM // MLP_CHUNK):
        vc = v[ci * MLP_CHUNK:(ci + 1) * MLP_CHUNK]
        mixed = jnp.concatenate(
            [jnp.dot(sgu_w_ref[g], vc[:, g * gd:(g + 1) * gd], preferred_element_type=F32)
             for g in range(MLP_GROUPS)], axis=1)
        chunks.append(mixed + sgu_b_ref[...])
    sg = u * jnp.concatenate(chunks, axis=0)
    mixed_all = jnp.concatenate([dn.astype(BF16), at, sg.astype(BF16)], axis=1)
    y = jnp.dot(mixed_all, w_out_ref[...], preferred_element_type=F32)
    x = x_ref[0] + m[2:3] * y
    ox[0] = x
    h = _ada_norm(x, g2_ref[...], m[3:4], m[4:5])
    oh[0] = h.astype(oh.dtype)
    if moe:
        ologit[0] = jnp.dot(h, rw_ref[...], precision=HI, preferred_element_type=F32) + rb_ref[...]


def _mix(x_all, mod_l, o_dn, zgate, attn, zu, zv2, dn_g, sgu_g, sgu_w, sgu_b, w_out, gain2, router=None):
    moe = router is not None
    off = 1 if moe else 0
    nt = N_TILES - off
    rows = TT - off * TM
    row = lambda b, i: (b, i + off, 0)
    orow = lambda b, i: (b, i, 0)
    const2 = lambda b, i: (0, 0)
    gd = MLP_WIDTH // MLP_GROUPS
    sgu_b_x = jnp.repeat(sgu_b.T, gd, axis=1)
    in_specs = [pl.BlockSpec((1, TM, D_MODEL), row),
                pl.BlockSpec((1, 6, D_MODEL), lambda b, i: _mod_index(b, i + off)),
                pl.BlockSpec((1, TM, DN_WIDTH), row),
                pl.BlockSpec((1, TM, DN_WIDTH), row),
                pl.BlockSpec((1, ATTN_HEADS, TM, HEAD_DIM), lambda b, i: (b, 0, i, 0)),
                pl.BlockSpec((1, TM, MLP_WIDTH), row),
                pl.BlockSpec((1, TM, MLP_WIDTH), row),
                pl.BlockSpec((1, DN_WIDTH), const2),
                pl.BlockSpec((1, MLP_WIDTH), const2),
                pl.BlockSpec((MLP_GROUPS, MLP_CHUNK, MLP_CHUNK), lambda b, i: (0, 0, 0)),
                pl.BlockSpec((MLP_CHUNK, MLP_WIDTH), const2),
                pl.BlockSpec((D_MIX, D_MODEL), const2),
                pl.BlockSpec((1, D_MODEL), const2)]
    args = [x_all, mod_l, o_dn, zgate, attn, zu, zv2, jnp.tile(dn_g.reshape(1, HEAD_DIM), (1, DN_HEADS)),
            sgu_g.reshape(1, MLP_WIDTH), sgu_w.astype(BF16), sgu_b_x, w_out, gain2.reshape(1, D_MODEL)]
    out_specs = [pl.BlockSpec((1, TM, D_MODEL), orow), pl.BlockSpec((1, TM, D_MODEL), orow)]
    out_shape = [jax.ShapeDtypeStruct((BATCH, rows, D_MODEL), F32),
                 jax.ShapeDtypeStruct((BATCH, rows, D_MODEL), F32 if moe else BF16)]
    if moe:
        rw, rb = router
        in_specs += [pl.BlockSpec((D_MODEL, LANES), const2), pl.BlockSpec((1, LANES), const2)]
        args += [jnp.pad(rw, ((0, 0), (0, LANES - MOE_EXPERTS))),
                 jnp.pad(rb.reshape(1, MOE_EXPERTS), ((0, 0), (0, LANES - MOE_EXPERTS)))]
        out_specs.append(pl.BlockSpec((1, TM, LANES), orow))
        out_shape.append(jax.ShapeDtypeStruct((BATCH, rows, LANES), F32))
    return pl.pallas_call(
        functools.partial(_mix_kernel, moe=moe),
        grid=(BATCH, nt),
        in_specs=in_specs, out_specs=out_specs, out_shape=out_shape,
        compiler_params=_cparams("parallel", "arbitrary"),
        name="mix_moe" if moe else "mix",
    )(*args)


FFN_SPLIT = 2
FFN_BLK = D_FF // FFN_SPLIT


def _ffn_kernel(x_ref, h_ref, mod_ref, w1_ref, w3_ref, w2_ref, o_ref):
    h = h_ref[0]
    y = None
    for c in range(FFN_SPLIT):
        sl = slice(c * FFN_BLK, (c + 1) * FFN_BLK)
        a = jnp.dot(h, w1_ref[:, sl], preferred_element_type=F32)
        b = jnp.dot(h, w3_ref[:, sl], preferred_element_type=F32)
        part = jnp.dot((jax.nn.silu(a) * b).astype(BF16), w2_ref[sl, :], preferred_element_type=F32)
        y = part if y is None else y + part
    o_ref[0] = x_ref[0] + mod_ref[0][5:6] * y


def _ffn(x_all, h_all, mod_l, w1, w3, w2):
    row = lambda b, i: (b, i, 0)
    const2 = lambda b, i: (0, 0)
    return pl.pallas_call(
        _ffn_kernel,
        grid=(BATCH, N_TILES),
        in_specs=[pl.BlockSpec((1, TM, D_MODEL), row),
                  pl.BlockSpec((1, TM, D_MODEL), row),
                  pl.BlockSpec((1, 6, D_MODEL), _mod_index),
                  pl.BlockSpec((D_MODEL, D_FF), const2),
                  pl.BlockSpec((D_MODEL, D_FF), const2),
                  pl.BlockSpec((D_FF, D_MODEL), const2)],
        out_specs=pl.BlockSpec((1, TM, D_MODEL), row),
        out_shape=jax.ShapeDtypeStruct((BATCH, TT, D_MODEL), F32),
        compiler_params=_cparams("parallel", "arbitrary"),
        name="ffn",
    )(x_all, h_all, mod_l, w1, w3, w2)


def _route_kernel(logit_ref, dest_ref, gate_ref, count_ref, run_scr):
    phase, t = pl.program_id(0), pl.program_id(1)

    @pl.when((phase == 0) & (t == 0))
    def _():
        run_scr[...] = jnp.zeros_like(run_scr)

    lane = lax.broadcasted_iota(jnp.int32, (ROUTE_TILE, LANES), 1).astype(F32)
    logits = jnp.where(lane < MOE_EXPERTS, logit_ref[...], -jnp.inf)
    m1 = jnp.max(logits, axis=-1, keepdims=True)
    e1 = jnp.min(jnp.where(logits == m1, lane, float(LANES)), axis=-1, keepdims=True)
    rest = jnp.where(lane == e1, -jnp.inf, logits)
    m2 = jnp.max(rest, axis=-1, keepdims=True)
    e2 = jnp.min(jnp.where(rest == m2, lane, float(LANES)), axis=-1, keepdims=True)
    hot1 = (lane == e1).astype(F32)
    hot2 = (lane == e2).astype(F32)
    hot = hot1 + hot2
    tile_count = jnp.sum(hot, axis=0, keepdims=True)

    @pl.when(phase == 0)
    def _():
        run_scr[0:1] = run_scr[0:1] + tile_count

    @pl.when((phase == 1) & (t == 0))
    def _():
        counts = jnp.broadcast_to(run_scr[0:1], (8, LANES))
        count_ref[...] = counts.astype(jnp.int32)
        padded = jnp.ceil(counts * (1.0 / MOE_BLOCK)) * MOE_BLOCK
        ei = lax.broadcasted_iota(jnp.int32, (LANES, LANES), 0)
        ej = lax.broadcasted_iota(jnp.int32, (LANES, LANES), 1)
        before = (ei < ej).astype(F32)
        run_scr[1:2] = jnp.dot(padded, before, precision=HI, preferred_element_type=F32)[0:1]

    @pl.when(phase == 1)
    def _():
        ri = lax.broadcasted_iota(jnp.int32, (ROUTE_TILE, ROUTE_TILE), 0)
        rj = lax.broadcasted_iota(jnp.int32, (ROUTE_TILE, ROUTE_TILE), 1)
        earlier = (ri > rj).astype(BF16)
        within = jnp.dot(earlier, hot.astype(BF16), preferred_element_type=F32)
        pos = within + run_scr[1:2]
        d1 = jnp.sum(pos * hot1, axis=-1, keepdims=True)
        d2 = jnp.sum(pos * hot2, axis=-1, keepdims=True)
        dest_ref[...] = jnp.concatenate([d1, d2], axis=1).astype(jnp.int32)
        w2 = jnp.exp(m2 - m1)
        gate_ref[...] = jnp.concatenate([1.0 / (1.0 + w2), w2 / (1.0 + w2)], axis=1)
        run_scr[1:2] = run_scr[1:2] + tile_count


def _route(logits):
    nt = N_LAT // ROUTE_TILE
    return pl.pallas_call(
        _route_kernel,
        grid=(2, nt),
        in_specs=[pl.BlockSpec((ROUTE_TILE, LANES), lambda p, t: (t, 0))],
        out_specs=[pl.BlockSpec((ROUTE_TILE, MOE_TOP_K), lambda p, t: (t * p, 0)),
                   pl.BlockSpec((ROUTE_TILE, MOE_TOP_K), lambda p, t: (t * p, 0)),
                   pl.BlockSpec((8, LANES), lambda p, t: (0, 0))],
        out_shape=[jax.ShapeDtypeStruct((N_LAT, MOE_TOP_K), jnp.int32),
                   jax.ShapeDtypeStruct((N_LAT, MOE_TOP_K), F32),
                   jax.ShapeDtypeStruct((8, LANES), jnp.int32)],
        scratch_shapes=[pltpu.VMEM((8, LANES), F32)],
        compiler_params=_cparams("arbitrary", "arbitrary"),
        name="moe_route",
    )(logits)


def _dispatch_kernel(dest_ref, h_ref, xs_in_ref, xs_ref, sem):
    del xs_in_ref
    base = pl.program_id(0) * TM

    def copy(r, k):
        return pltpu.make_async_copy(h_ref.at[pl.ds(r, 1)],
                                     xs_ref.at[pl.ds(dest_ref[(base + r) * MOE_TOP_K + k], 1)], sem)

    def start(r, c):
        for k in range(MOE_TOP_K):
            copy(r, k).start()
        return c

    def wait(r, c):
        for k in range(MOE_TOP_K):
            copy(r, k).wait()
        return c

    lax.fori_loop(0, TM, start, 0)
    lax.fori_loop(0, TM, wait, 0)


def _dispatch(dest_flat, h_lat):
    grid_spec = pltpu.PrefetchScalarGridSpec(
        num_scalar_prefetch=1,
        grid=(N_LAT // TM,),
        in_specs=[pl.BlockSpec((TM, D_MODEL), lambda i, d: (i, 0)),
                  pl.BlockSpec(memory_space=pl.ANY)],
        out_specs=pl.BlockSpec(memory_space=pl.ANY),
        scratch_shapes=[pltpu.SemaphoreType.DMA(())])
    return pl.pallas_call(
        _dispatch_kernel,
        grid_spec=grid_spec,
        out_shape=jax.ShapeDtypeStruct((MOE_ROWS, D_MODEL), F32),
        input_output_aliases={2: 0},
        compiler_params=_cparams("arbitrary"),
        name="moe_dispatch",
    )(dest_flat, h_lat, jnp.zeros((MOE_ROWS, D_MODEL), F32))


def _expert_kernel(be_ref, xs_ref, w1_ref, w3_ref, w2_ref, *rest, first):
    del be_ref
    if first:
        (o_ref,) = rest
    else:
        acc_ref, o_ref = rest
    x = xs_ref[...].astype(BF16)
    a = jnp.dot(x, w1_ref[0], preferred_element_type=F32)
    b = jnp.dot(x, w3_ref[0], preferred_element_type=F32)
    y = jnp.dot((jax.nn.silu(a) * b).astype(BF16), w2_ref[0], preferred_element_type=F32)
    o_ref[...] = y if first else acc_ref[...] + y


def _experts(block_expert, xs, w1, w3, w2):
    yb = None
    for c in range(MOE_FF_SPLIT):
        first = c == 0
        in_specs = [pl.BlockSpec((MOE_BLOCK, D_MODEL), lambda j, be: (j, 0)),
                    pl.BlockSpec((1, D_MODEL, MOE_FF_BLK), lambda j, be, c=c: (be[j], 0, c)),
                    pl.BlockSpec((1, D_MODEL, MOE_FF_BLK), lambda j, be, c=c: (be[j], 0, c)),
                    pl.BlockSpec((1, MOE_FF_BLK, D_MODEL), lambda j, be, c=c: (be[j], c, 0))]
        args = [block_expert, xs, w1, w3, w2]
        if not first:
            in_specs.append(pl.BlockSpec((MOE_BLOCK, D_MODEL), lambda j, be: (j, 0)))
            args.append(yb)
        yb = pl.pallas_call(
            functools.partial(_expert_kernel, first=first),
            grid_spec=pltpu.PrefetchScalarGridSpec(
                num_scalar_prefetch=1, grid=(MOE_N_BLOCKS,), in_specs=in_specs,
                out_specs=pl.BlockSpec((MOE_BLOCK, D_MODEL), lambda j, be: (j, 0))),
            out_shape=jax.ShapeDtypeStruct((MOE_ROWS, D_MODEL), F32),
            input_output_aliases={} if first else {5: 0},
            compiler_params=_cparams("arbitrary"),
            name="moe_experts_%d" % c,
        )(*args)
    return yb


def _combine_kernel(dest_ref, x_ref, gate_ref, mod_ref, g_ref, yb_ref, o_ref, buf, sem):
    base = (pl.program_id(0) * (SEQ // TM) + pl.program_id(1)) * TM

    def copy(r, k):
        return pltpu.make_async_copy(yb_ref.at[pl.ds(dest_ref[(base + r) * MOE_TOP_K + k], 1)],
                                     buf.at[k, pl.ds(r, 1)], sem)

    def start(r, c):
        for k in range(MOE_TOP_K):
            copy(r, k).start()
        return c

    def wait(r, c):
        for k in range(MOE_TOP_K):
            copy(r, k).wait()
        return c

    lax.fori_loop(0, TM, start, 0)
    lax.fori_loop(0, TM, wait, 0)
    gates = gate_ref[0]
    y = buf[0] * gates[:, 0:1] + buf[1] * gates[:, 1:2]
    x = x_ref[0] + mod_ref[0][5:6] * y
    o_ref[0] = x * lax.rsqrt(jnp.mean(x * x, axis=-1, keepdims=True) + NORM_EPS) * g_ref[...]


def _combine(dest_flat, x_lat, gates, mod_l, final_g, yb):
    grid_spec = pltpu.PrefetchScalarGridSpec(
        num_scalar_prefetch=1,
        grid=(BATCH, SEQ // TM),
        in_specs=[pl.BlockSpec((1, TM, D_MODEL), lambda b, i, d: (b, i, 0)),
                  pl.BlockSpec((1, TM, MOE_TOP_K), lambda b, i, d: (b, i, 0)),
                  pl.BlockSpec((1, 6, D_MODEL), lambda b, i, d: (b, 0, 0)),
                  pl.BlockSpec((1, D_MODEL), lambda b, i, d: (0, 0)),
                  pl.BlockSpec(memory_space=pl.ANY)],
        out_specs=pl.BlockSpec((1, TM, D_MODEL), lambda b, i, d: (b, i, 0)),
        scratch_shapes=[pltpu.VMEM((MOE_TOP_K, TM, D_MODEL), F32), pltpu.SemaphoreType.DMA(())])
    return pl.pallas_call(
        _combine_kernel,
        grid_spec=grid_spec,
        out_shape=jax.ShapeDtypeStruct((BATCH, SEQ, D_MODEL), F32),
        compiler_params=_cparams("arbitrary", "arbitrary"),
        name="moe_combine",
    )(dest_flat, x_lat, gates.reshape(BATCH, SEQ, MOE_TOP_K), mod_l, final_g.reshape(1, D_MODEL), yb)


def _reorder_w_in(w):
    s = np.cumsum((3 * DN_WIDTH, DN_WIDTH, 2 * DN_HEADS, 2 * DN_HEADS, ATTN_WIDTH, ATTN_KV_WIDTH, ATTN_KV_WIDTH,
                   MLP_WIDTH, MLP_WIDTH)).tolist()
    ba = w[:, s[1]:s[3]]
    return jnp.concatenate([w[:, :s[1]], w[:, s[3]:], ba,
                            jnp.zeros((D_MODEL, LANES - 4 * DN_HEADS), w.dtype)], axis=1).astype(BF16)


def kernel(x, c, ctx, c_ctx, mod_w, mod_b, norm1_g, norm2_g, w_in, conv_w, dn_a_log, dn_dt_bias, dn_norm_g,
           q_norm_g, k_norm_g, sgu_norm_g, sgu_w, sgu_b, w_out, ffn_w1, ffn_w3, ffn_w2, router_w, router_b,
           moe_w1, moe_w3, moe_w2, final_norm_g):
    assert DEPTH == 2 and x.shape == (BATCH, SEQ, D_MODEL) and ctx.shape == (BATCH, CTX_LEN, D_MODEL)
    cond8 = jnp.concatenate([c, c_ctx[None], jnp.zeros((8 - BATCH - 1, D_MODEL), F32)], axis=0)
    mod = _modulation(cond8, mod_w, mod_b)
    rope_c, rope_s = _rope_tables()
    x_all = jnp.concatenate([ctx, x], axis=1)
    for layer in range(DEPTH):
        last = layer == DEPTH - 1
        zqkv, zgate, zq, zk, zv, zu, zv2, zba = _in_proj(x_all, mod[layer], norm1_g[layer],
                                                         _reorder_w_in(w_in[layer]))
        qkv = _dn_prep(zqkv, conv_w[layer])
        o_dn = _deltanet(qkv, zba, dn_a_log[layer], dn_dt_bias[layer])
        qh, kh, vh = _attn_prep(zq, zk, zv, q_norm_g[layer], k_norm_g[layer], rope_c, rope_s)
        attn = _attention(qh, kh, vh, with_ctx=not last)
        mix_args = (x_all, mod[layer], o_dn, zgate, attn, zu, zv2, dn_norm_g[layer], sgu_norm_g[layer],
                    sgu_w[layer], sgu_b[layer], w_out[layer].astype(BF16), norm2_g[layer])
        if not last:
            i = layer // 2
            x_mid, h_mid = _mix(*mix_args)
            x_all = _ffn(x_mid, h_mid, mod[layer], ffn_w1[i].astype(BF16), ffn_w3[i].astype(BF16),
                         ffn_w2[i].astype(BF16))
        else:
            i = layer // 2
            x_lat, h_lat, logits = _mix(*mix_args, router=(router_w[i], router_b[i]))
            dest, gates, counts = _route(logits.reshape(N_LAT, LANES))
            padded = (counts[0, :MOE_EXPERTS] + MOE_BLOCK - 1) // MOE_BLOCK * MOE_BLOCK
            pad_ends = jnp.cumsum(padded)
            block_expert = jnp.minimum(
                jnp.sum(pad_ends[None, :] <= (jnp.arange(MOE_N_BLOCKS) * MOE_BLOCK)[:, None], axis=1),
                MOE_EXPERTS - 1).astype(jnp.int32)
            dest_flat = dest.reshape(N_ASSIGN)
            xs = _dispatch(dest_flat, h_lat.reshape(N_LAT, D_MODEL))
            yb = _experts(block_expert, xs, moe_w1[i].astype(BF16), moe_w3[i].astype(BF16),
                          moe_w2[i].astype(BF16))
            return _combine(dest_flat, x_lat, gates, mod[layer], final_norm_g, yb)
```

```python
import functools

import jax
import jax.numpy as jnp
import numpy as np
from jax import lax
from jax.experimental import pallas as pl
from jax.experimental.pallas import tpu as pltpu

D_MODEL = 1024
BATCH = 4
SEQ = 4096
DEPTH = 2
GRID_W = 64
CTX_LEN = 256
HEAD_DIM = 64
DN_HEADS = 6
ATTN_HEADS = 6
ATTN_KV_HEADS = 2
ATTN_GROUP = ATTN_HEADS // ATTN_KV_HEADS
MLP_GROUPS = 4
DN_WIDTH = DN_HEADS * HEAD_DIM
ATTN_WIDTH = ATTN_HEADS * HEAD_DIM
ATTN_KV_WIDTH = ATTN_KV_HEADS * HEAD_DIM
MLP_WIDTH = MLP_GROUPS * HEAD_DIM
D_MIX = DN_WIDTH + ATTN_WIDTH + MLP_WIDTH
CONV_K = 3
DN_CHUNK = 64
ATTN_SCALE = HEAD_DIM ** -0.5
LOG2E = 1.4426950408889634
MLP_CHUNK = 128
ROPE_THETA = 10000.0
ROPE_AXIS_DIM = HEAD_DIM // 2
ROPE_PAIRS = ROPE_AXIS_DIM // 2
D_FF = 2816
MOE_EXPERTS = 8
MOE_TOP_K = 2
MOE_D_FF = 3584
MOE_BLOCK = 256
NORM_EPS = 1e-6

LANES = 128
TT = SEQ + CTX_LEN
TM = 256
N_TILES = TT // TM
CTX_TILE = N_TILES - 1
CTX_ROW = BATCH
N_DN_STEPS = TT // DN_CHUNK
N_CTX_CHUNKS = CTX_LEN // DN_CHUNK
N_LAT_CHUNKS = SEQ // DN_CHUNK
IN_PAD = 3 * DN_WIDTH + DN_WIDTH + ATTN_WIDTH + 2 * ATTN_KV_WIDTH + 2 * MLP_WIDTH + LANES
N_LAT = BATCH * SEQ
N_ASSIGN = N_LAT * MOE_TOP_K
MOE_N_BLOCKS = -(-(N_ASSIGN + MOE_EXPERTS * (MOE_BLOCK - 1)) // MOE_BLOCK)
MOE_ROWS = MOE_N_BLOCKS * MOE_BLOCK
MOE_FF_SPLIT = 2
MOE_FF_BLK = MOE_D_FF // MOE_FF_SPLIT
ROUTE_TILE = 512
VMEM_LIMIT = 56 * 2 ** 20

F32 = jnp.float32
BF16 = jnp.bfloat16
HI = lax.Precision.HIGHEST


def _cparams(*sem):
    return pltpu.CompilerParams(dimension_semantics=sem, vmem_limit_bytes=VMEM_LIMIT)


def _bdot(a, b):
    return jnp.dot(a.astype(BF16), b.astype(BF16), preferred_element_type=F32)


def _bdot_nt(a, b):
    return lax.dot_general(a.astype(BF16), b.astype(BF16), (((1,), (1,)), ((), ())), preferred_element_type=F32)


def _bdot_tn(a, b):
    return lax.dot_general(a.astype(BF16), b.astype(BF16), (((0,), (0,)), ((), ())), preferred_element_type=F32)


def _seg_ones():
    r = lax.shift_right_logical(lax.broadcasted_iota(jnp.int32, (LANES, LANES), 0), 6)
    c = lax.shift_right_logical(lax.broadcasted_iota(jnp.int32, (LANES, LANES), 1), 6)
    return (r == c).astype(F32)


def _seg_sum(y, ones):
    parts = [jnp.dot(y[:, i:i + LANES], ones, precision=HI, preferred_element_type=F32)
             for i in range(0, y.shape[-1], LANES)]
    return parts[0] if len(parts) == 1 else jnp.concatenate(parts, axis=-1)


def _softplus(x):
    return jnp.maximum(x, 0.0) + jnp.log1p(jnp.exp(-jnp.abs(x)))


def _ada_norm(x, gain, shift, scale):
    y = x * lax.rsqrt(jnp.mean(x * x, axis=-1, keepdims=True) + NORM_EPS) * gain
    return y * (1.0 + scale) + shift


def _mod_index(b, i):
    return (jnp.where(i == CTX_TILE, CTX_ROW, b), 0, 0)


def _mod_kernel(c_ref, w_ref, b_ref, o_ref):
    cond = jax.nn.silu(c_ref[...])
    o_ref[0] = jnp.dot(cond, w_ref[0], precision=HI, preferred_element_type=F32) + b_ref[0]


def _modulation(cond8, mod_w, mod_b):
    nblk = 4
    bn = 6 * D_MODEL // nblk
    out = pl.pallas_call(
        _mod_kernel,
        grid=(DEPTH, nblk),
        in_specs=[pl.BlockSpec((8, D_MODEL), lambda l, j: (0, 0)),
                  pl.BlockSpec((1, D_MODEL, bn), lambda l, j: (l, 0, j)),
                  pl.BlockSpec((1, 1, bn), lambda l, j: (l, 0, j))],
        out_specs=pl.BlockSpec((1, 8, bn), lambda l, j: (l, 0, j)),
        out_shape=jax.ShapeDtypeStruct((DEPTH, 8, 6 * D_MODEL), F32),
        compiler_params=_cparams("arbitrary", "arbitrary"),
        name="modulation",
    )(cond8, mod_w, mod_b.reshape(DEPTH, 1, 6 * D_MODEL))
    return out.reshape(DEPTH, 8, 6, D_MODEL)


_IN_SPLITS = (3 * DN_WIDTH, DN_WIDTH, ATTN_WIDTH, ATTN_KV_WIDTH, ATTN_KV_WIDTH, MLP_WIDTH, MLP_WIDTH, LANES)


def _in_proj_kernel(x_ref, mod_ref, g_ref, w_ref, oqkv, ogate, oq, ok, ov, ou, ov2, oba):
    m = mod_ref[0]
    h = _ada_norm(x_ref[0], g_ref[...], m[0:1], m[1:2]).astype(BF16)
    z = jnp.dot(h, w_ref[...], preferred_element_type=F32)
    off = 0
    for ref, width in zip((oqkv, ogate, oq, ok, ov, ou, ov2, oba), _IN_SPLITS):
        ref[0] = z[:, off:off + width].astype(ref.dtype)
        off += width


def _in_proj(x_all, mod_l, gain, w_in_r):
    dts = (BF16,) * 7 + (F32,)
    row = lambda b, i: (b, i, 0)
    return pl.pallas_call(
        _in_proj_kernel,
        grid=(BATCH, N_TILES),
        in_specs=[pl.BlockSpec((1, TM, D_MODEL), row),
                  pl.BlockSpec((1, 6, D_MODEL), _mod_index),
                  pl.BlockSpec((1, D_MODEL), lambda b, i: (0, 0)),
                  pl.BlockSpec((D_MODEL, IN_PAD), lambda b, i: (0, 0))],
        out_specs=[pl.BlockSpec((1, TM, w), row) for w in _IN_SPLITS],
        out_shape=[jax.ShapeDtypeStruct((BATCH, TT, w), dt) for w, dt in zip(_IN_SPLITS, dts)],
        compiler_params=_cparams("parallel", "arbitrary"),
        name="in_proj",
    )(x_all, mod_l, gain.reshape(1, D_MODEL), w_in_r)


def _dn_prep_kernel(z_ref, w_ref, o_ref):
    j = pl.program_id(1)
    z = z_ref[0].astype(F32)
    w = w_ref[...]
    row = lax.broadcasted_iota(jnp.int32, (TT, 1), 0)
    first = (row == 0) | (row == SEQ)
    last = (row == SEQ - 1) | (row == TT - 1)
    zp = jnp.where(first, 0.0, pltpu.roll(z, 1, 0))
    zn = jnp.where(last, 0.0, pltpu.roll(z, TT - 1, 0))
    y = jax.nn.silu(w[0:1] * zp + w[1:2] * z + w[2:3] * zn)
    n_qk = 2 * DN_WIDTH // LANES
    n_q = DN_WIDTH // LANES

    @pl.when(j < n_qk)
    def _():
        inv = lax.rsqrt(_seg_sum(y * y, _seg_ones()) + NORM_EPS)
        o_ref[0] = (y * inv * jnp.where(j < n_q, HEAD_DIM ** -0.5, 1.0)).astype(o_ref.dtype)

    @pl.when(j >= n_qk)
    def _():
        o_ref[0] = y.astype(o_ref.dtype)


def _dn_prep(zqkv, conv_w):
    nb = 3 * DN_WIDTH // LANES
    return pl.pallas_call(
        _dn_prep_kernel,
        grid=(BATCH, nb),
        in_specs=[pl.BlockSpec((1, TT, LANES), lambda b, j: (b, 0, j)),
                  pl.BlockSpec((CONV_K, LANES), lambda b, j: (0, j))],
        out_specs=pl.BlockSpec((1, TT, LANES), lambda b, j: (b, 0, j)),
        out_shape=jax.ShapeDtypeStruct((BATCH, TT, 3 * DN_WIDTH), BF16),
        compiler_params=_cparams("parallel", "arbitrary"),
        name="dn_prep",
    )(zqkv, conv_w)


DN_PAIRS = DN_HEADS // 2
DN_SUB = 2
LOG2_CHUNK = DN_CHUNK.bit_length() - 1


def _dn_kernel(qkv_ref, zba_ref, a_ref, dt_ref, o_ref, s_scr):
    C, P2 = DN_CHUNK, 2 * DN_CHUNK
    o_ref[...] = jnp.zeros_like(o_ref)
    s_scr[...] = jnp.zeros_like(s_scr)
    neg_decay_rate = -jnp.exp(a_ref[...])
    dt_bias = dt_ref[...]
    lo = lax.broadcasted_iota(jnp.int32, (1, P2), 1) < C

    def stack(a):
        return jnp.concatenate([jnp.where(lo, a, 0.0), jnp.where(lo, 0.0, a)], axis=0)

    def step(i, carry):
        ri = lax.broadcasted_iota(jnp.int32, (P2, P2), 0)
        ci = lax.broadcasted_iota(jnp.int32, (P2, P2), 1)
        ti = lax.broadcasted_iota(jnp.int32, (C, C), 0)
        tj = lax.broadcasted_iota(jnp.int32, (C, C), 1)
        same = lambda sh: lax.shift_right_logical(ri, sh) == lax.shift_right_logical(ci, sh)
        same_head = same(LOG2_CHUNK)
        eye = (ri == ci).astype(F32)

        ch = []
        for sub in range(DN_SUB):
            s = i * DN_SUB + sub
            chunk_of = (jnp.where(s < N_CTX_CHUNKS, N_LAT_CHUNKS + s, s - N_CTX_CHUNKS), N_DN_STEPS - 1 - s)
            for d in range(2):
                r0 = pl.multiple_of(chunk_of[d] * C, C)
                zba = zba_ref[0, pl.ds(r0, C), :]
                beta_all = jax.nn.sigmoid(zba)
                g_all = neg_decay_rate * _softplus(zba + dt_bias)
                tri = (ti >= tj) if d == 0 else (ti <= tj)
                gc_all = jnp.dot(tri.astype(F32), g_all, precision=HI, preferred_element_type=F32)
                gc_t = jnp.concatenate([gc_all, gc_all], axis=0).T
                last = C - 1 if d == 0 else 0
                for p in range(DN_PAIRS):
                    lb = (d * DN_HEADS + 2 * p, d * DN_HEADS + 2 * p + 1)
                    la = (2 * DN_HEADS + lb[0], 2 * DN_HEADS + lb[1])
                    col = lambda t, l: t[:, l:l + 1]
                    q2 = qkv_ref[0, pl.ds(r0, C), p * LANES:(p + 1) * LANES].astype(F32)
                    k2 = qkv_ref[0, pl.ds(r0, C), DN_WIDTH + p * LANES:DN_WIDTH + (p + 1) * LANES].astype(F32)
                    v2 = qkv_ref[0, pl.ds(r0, C), 2 * DN_WIDTH + p * LANES:2 * DN_WIDTH + (p + 1) * LANES].astype(F32)
                    beta2 = jnp.where(lo, col(beta_all, lb[0]), col(beta_all, lb[1]))
                    gc2 = jnp.where(lo, col(gc_all, la[0]), col(gc_all, la[1]))
                    gcol = jnp.concatenate([col(gc_all, la[0]), col(gc_all, la[1])], axis=0)
                    grow = jnp.where(lo, gc_t[la[0]:la[0] + 1, :], gc_t[la[1]:la[1] + 1, :])
                    g_last = jnp.where(lo, gc_all[last:last + 1, la[0]:la[0] + 1],
                                       gc_all[last:last + 1, la[1]:la[1] + 1])
                    e_diff = jnp.exp(gcol - grow)
                    order = (ri >= ci) if d == 0 else (ri <= ci)
                    kb2 = k2 * beta2
                    e_gc = jnp.exp(gc2)
                    ch.append(dict(
                        r0=r0, sub=sub, d=d, p=p, g_last=g_last,
                        dec_incl=jnp.where(same_head & order, e_diff, 0.0),
                        dec_strict=jnp.where(same_head & order & (ri != ci), e_diff, 0.0),
                        kq=jnp.concatenate([stack(kb2), stack(q2)], axis=0).astype(BF16),
                        k=stack(k2).astype(BF16),
                        rhs=jnp.concatenate([stack(v2 * beta2), stack(kb2 * e_gc)], axis=1).astype(BF16),
                        kd=stack(k2 * jnp.exp(g_last - gc2)).astype(BF16),
                        qe=stack(q2 * e_gc)))
        for c in ch:
            kk_qk = lax.dot_general(c["kq"], c["k"], (((1,), (1,)), ((), ())), preferred_element_type=F32)
            c["a"] = kk_qk[:P2] * c["dec_strict"]
            c["attn"] = (kk_qk[P2:] * c["dec_incl"]).astype(BF16)
            c["t"] = eye - jnp.where(same(1), c["a"], 0.0)
        for lvl in range(1, LOG2_CHUNK):
            joins = same(lvl + 1) & ~same(lvl)
            for c in ch:
                c["m"] = _bdot(jnp.where(joins, c["a"], 0.0), c["t"])
            for c in ch:
                c["t"] = c["t"] - _bdot(c["t"], c["m"])
        for c in ch:
            c["uw"] = _bdot(c["t"], c["rhs"]).astype(BF16)
        for c in ch:
            oa = jnp.dot(c["attn"], c["uw"], preferred_element_type=F32)
            c["o0"] = oa[:, :LANES]
            c["qp"] = c["qe"] - oa[:, LANES:]
            c["np"] = lax.dot_general(c["kd"], c["uw"], (((0,), (0,)), ((), ())), preferred_element_type=F32)

        states = [s_scr[j] for j in range(2 * DN_PAIRS)]
        for sub in range(DN_SUB):
            cur = [c for c in ch if c["sub"] == sub]
            for c in cur:
                j = c["d"] * DN_PAIRS + c["p"]
                c["r"] = _bdot(jnp.concatenate([c["qp"], c["np"][:, LANES:]], axis=0), states[j])
            for d in range(2):
                outs = []
                for c in cur:
                    if c["d"] != d:
                        continue
                    j = d * DN_PAIRS + c["p"]
                    o_st = c["o0"] + c["r"][:P2]
                    outs.append(o_st[:C] + o_st[C:])
                    states[j] = states[j] * jnp.exp(c["g_last"]) + c["np"][:, :LANES] - c["r"][P2:]
                    r0 = c["r0"]
                o_ref[0, pl.ds(r0, C), :] += jnp.concatenate(outs, axis=1)
        for j in range(2 * DN_PAIRS):
            s_scr[j] = states[j]
        return carry

    lax.fori_loop(0, N_DN_STEPS // DN_SUB, step, 0)


def _deltanet(qkv, zba, a_log, dt_bias):
    pad = lambda t: jnp.zeros((1, LANES), F32).at[0, 2 * DN_HEADS:4 * DN_HEADS].set(t.reshape(-1))
    return pl.pallas_call(
        _dn_kernel,
        grid=(BATCH,),
        in_specs=[pl.BlockSpec((1, TT, 3 * DN_WIDTH), lambda b: (b, 0, 0)),
                  pl.BlockSpec((1, TT, LANES), lambda b: (b, 0, 0)),
                  pl.BlockSpec((1, LANES), lambda b: (0, 0)),
                  pl.BlockSpec((1, LANES), lambda b: (0, 0))],
        out_specs=pl.BlockSpec((1, TT, DN_WIDTH), lambda b: (b, 0, 0)),
        out_shape=jax.ShapeDtypeStruct((BATCH, TT, DN_WIDTH), F32),
        scratch_shapes=[pltpu.VMEM((2 * DN_PAIRS, LANES, LANES), F32)],
        compiler_params=_cparams("parallel"),
        name="deltanet",
    )(qkv, zba, pad(a_log), pad(dt_bias))


def _rope_tables():
    rows = SEQ // GRID_W
    row = jnp.repeat(jnp.arange(rows, dtype=F32), GRID_W)
    col = jnp.tile(jnp.arange(GRID_W, dtype=F32), rows)
    inv = ROPE_THETA ** (-2.0 * jnp.arange(ROPE_PAIRS, dtype=F32) / ROPE_AXIS_DIM)
    ang = jnp.stack([row[:, None] * inv, col[:, None] * inv], axis=1)
    cos, sin = jnp.cos(ang), jnp.sin(ang)
    c = jnp.concatenate([cos[:, 0], cos[:, 0], cos[:, 1], cos[:, 1]], axis=-1)
    s = jnp.concatenate([-sin[:, 0], sin[:, 0], -sin[:, 1], sin[:, 1]], axis=-1)
    c = jnp.concatenate([c, jnp.ones((CTX_LEN, HEAD_DIM), F32)], axis=0)
    s = jnp.concatenate([s, jnp.zeros((CTX_LEN, HEAD_DIM), F32)], axis=0)
    return jnp.tile(c, (1, 2)), jnp.tile(s, (1, 2))


def _attn_prep_kernel(zq_ref, zk_ref, zv_ref, gq_ref, gk_ref, c_ref, s_ref, oq, ok, ov):
    ones = _seg_ones()
    cs, sn = c_ref[...], s_ref[...]
    lane = lax.broadcasted_iota(jnp.int32, (1, LANES), 1)
    first_half = (lane & (2 * ROPE_PAIRS - 1)) < ROPE_PAIRS

    def norm_rope(x, gain):
        y = x * lax.rsqrt(_seg_sum(x * x, ones) * (1.0 / HEAD_DIM) + NORM_EPS) * gain
        partner = jnp.where(first_half, pltpu.roll(y, LANES - ROPE_PAIRS, 1), pltpu.roll(y, ROPE_PAIRS, 1))
        return y * cs + partner * sn

    zq = zq_ref[0].astype(F32)
    for pair in range(ATTN_HEADS // 2):
        q2 = norm_rope(zq[:, pair * LANES:(pair + 1) * LANES], gq_ref[...]) * (ATTN_SCALE * LOG2E)
        oq[0, 2 * pair] = q2[:, :HEAD_DIM].astype(oq.dtype)
        oq[0, 2 * pair + 1] = q2[:, HEAD_DIM:].astype(oq.dtype)
    k2 = norm_rope(zk_ref[0].astype(F32), gk_ref[...])
    v2 = zv_ref[0]
    one_cols = jnp.ones((TM, HEAD_DIM), ov.dtype)
    for h in range(ATTN_KV_HEADS):
        ok[0, h] = k2[:, h * HEAD_DIM:(h + 1) * HEAD_DIM].astype(ok.dtype)
        ov[0, h] = jnp.concatenate([v2[:, h * HEAD_DIM:(h + 1) * HEAD_DIM], one_cols], axis=1)


def _attn_prep(zq, zk, zv, q_gain, k_gain, rope_c, rope_s):
    row = lambda b, i: (b, i, 0)
    hrow = lambda b, i: (b, 0, i, 0)
    tile2 = lambda g: jnp.tile(g.reshape(1, HEAD_DIM), (1, 2))
    return pl.pallas_call(
        _attn_prep_kernel,
        grid=(BATCH, N_TILES),
        in_specs=[pl.BlockSpec((1, TM, ATTN_WIDTH), row),
                  pl.BlockSpec((1, TM, ATTN_KV_WIDTH), row),
                  pl.BlockSpec((1, TM, ATTN_KV_WIDTH), row),
                  pl.BlockSpec((1, LANES), lambda b, i: (0, 0)),
                  pl.BlockSpec((1, LANES), lambda b, i: (0, 0)),
                  pl.BlockSpec((TM, LANES), lambda b, i: (i, 0)),
                  pl.BlockSpec((TM, LANES), lambda b, i: (i, 0))],
        out_specs=[pl.BlockSpec((1, ATTN_HEADS, TM, HEAD_DIM), hrow),
                   pl.BlockSpec((1, ATTN_KV_HEADS, TM, HEAD_DIM), hrow),
                   pl.BlockSpec((1, ATTN_KV_HEADS, TM, LANES), hrow)],
        out_shape=[jax.ShapeDtypeStruct((BATCH, ATTN_HEADS, TT, HEAD_DIM), BF16),
                   jax.ShapeDtypeStruct((BATCH, ATTN_KV_HEADS, TT, HEAD_DIM), BF16),
                   jax.ShapeDtypeStruct((BATCH, ATTN_KV_HEADS, TT, LANES), BF16)],
        compiler_params=_cparams("parallel", "arbitrary"),
        name="attn_prep",
    )(zq, zk, zv, tile2(q_gain), tile2(k_gain), rope_c, rope_s)


ATTN_TQ = 128
ATTN_NQ = 4
ATTN_KB = 512
ATTN_KEY_BLOCKS = tuple((j * ATTN_KB, ATTN_KB) for j in range(SEQ // ATTN_KB)) + ((SEQ, CTX_LEN),)


def _dot_nt(a, b):
    return lax.dot_general(a, b, (((1,), (1,)), ((), ())), preferred_element_type=F32)


def _attn_finish(acc):
    return acc[:, :HEAD_DIM] / acc[:, HEAD_DIM:HEAD_DIM + 1]


def _attn_lat_kernel(q_ref, k_ref, v_ref, o_ref, s0_scr, s1_scr):
    rows = ATTN_GROUP * ATTN_TQ
    s_scr = (s0_scr, s1_scr)
    mpart = [None] * ATTN_NQ
    acc = [None] * ATTN_NQ
    for stage in range(ATTN_NQ + 1):
        a, b = stage, stage - 1
        if a < ATTN_NQ:
            qa = q_ref[0, :, a * ATTN_TQ:(a + 1) * ATTN_TQ, :].reshape(rows, HEAD_DIM)
        if b >= 0:
            m_b = jnp.max(mpart[b], axis=-1, keepdims=True)
        for k0, kn in ATTN_KEY_BLOCKS:
            if a < ATTN_NQ:
                s = _dot_nt(qa, k_ref[0, 0, k0:k0 + kn, :])
                s_scr[a % 2][:, k0:k0 + kn] = s
                blk = functools.reduce(jnp.maximum, [s[:, i:i + LANES] for i in range(0, kn, LANES)])
                mpart[a] = blk if mpart[a] is None else jnp.maximum(mpart[a], blk)
            if b >= 0:
                p = jnp.exp2(s_scr[b % 2][:, k0:k0 + kn] - m_b).astype(BF16)
                pv = jnp.dot(p, v_ref[0, 0, k0:k0 + kn, :], preferred_element_type=F32)
                acc[b] = pv if acc[b] is None else acc[b] + pv
        if b >= 0:
            o = _attn_finish(acc[b]).reshape(ATTN_GROUP, ATTN_TQ, HEAD_DIM)
            o_ref[0, :, b * ATTN_TQ:(b + 1) * ATTN_TQ, :] = o.astype(o_ref.dtype)


def _attn_ctx_kernel(q_ref, k_ref, v_ref, o_ref):
    q = q_ref[0].reshape(ATTN_GROUP * CTX_LEN, HEAD_DIM)
    s = _dot_nt(q, k_ref[0, 0])
    p = jnp.exp2(s - jnp.max(s, axis=-1, keepdims=True)).astype(BF16)
    o = _attn_finish(jnp.dot(p, v_ref[0, 0], preferred_element_type=F32))
    o_ref[0] = o.reshape(ATTN_GROUP, CTX_LEN, HEAD_DIM).astype(o_ref.dtype)


def _attention_lat(qh, kh, vh):
    tq = ATTN_NQ * ATTN_TQ
    rows = ATTN_GROUP * ATTN_TQ
    return pl.pallas_call(
        _attn_lat_kernel,
        grid=(BATCH, ATTN_KV_HEADS, SEQ // tq),
        in_specs=[pl.BlockSpec((1, ATTN_GROUP, tq, HEAD_DIM), lambda b, g, i: (b, g, i, 0)),
                  pl.BlockSpec((1, 1, TT, HEAD_DIM), lambda b, g, i: (b, g, 0, 0)),
                  pl.BlockSpec((1, 1, TT, LANES), lambda b, g, i: (b, g, 0, 0))],
        out_specs=pl.BlockSpec((1, ATTN_GROUP, tq, HEAD_DIM), lambda b, g, i: (b, g, i, 0)),
        out_shape=jax.ShapeDtypeStruct((BATCH, ATTN_HEADS, SEQ, HEAD_DIM), BF16),
        scratch_shapes=[pltpu.VMEM((rows, TT), F32), pltpu.VMEM((rows, TT), F32)],
        compiler_params=_cparams("parallel", "parallel", "arbitrary"),
        name="attention_lat",
    )(qh, kh, vh)


def _attention_ctx(qh, kh, vh):
    ctx_blk = SEQ // CTX_LEN
    return pl.pallas_call(
        _attn_ctx_kernel,
        grid=(BATCH, ATTN_KV_HEADS),
        in_specs=[pl.BlockSpec((1, ATTN_GROUP, CTX_LEN, HEAD_DIM), lambda b, g: (b, g, ctx_blk, 0)),
                  pl.BlockSpec((1, 1, CTX_LEN, HEAD_DIM), lambda b, g: (b, g, ctx_blk, 0)),
                  pl.BlockSpec((1, 1, CTX_LEN, LANES), lambda b, g: (b, g, ctx_blk, 0))],
        out_specs=pl.BlockSpec((1, ATTN_GROUP, CTX_LEN, HEAD_DIM), lambda b, g: (b, g, 0, 0)),
        out_shape=jax.ShapeDtypeStruct((BATCH, ATTN_HEADS, CTX_LEN, HEAD_DIM), BF16),
        compiler_params=_cparams("parallel", "arbitrary"),
        name="attention_ctx",
    )(qh, kh, vh)


def _mix_kernel(x_ref, mod_ref, o_dn_ref, zgate_ref, attn_ref, zu_ref, zv_ref, dn_g_ref, sgu_g_ref, sgu_w_ref,
                sgu_b_ref, w_out_ref, g2_ref, *rest, moe):
    if moe:
        rw_ref, rb_ref, ox, oh, ologit = rest
    else:
        ox, oh = rest
    ones = _seg_ones()
    m = mod_ref[0]
    o = o_dn_ref[0]
    dn = o * lax.rsqrt(_seg_sum(o * o, ones) * (1.0 / HEAD_DIM) + NORM_EPS) * dn_g_ref[...]
    dn = dn * jax.nn.silu(zgate_ref[0].astype(F32))
    at = jnp.concatenate([attn_ref[0, h] for h in range(ATTN_HEADS)], axis=1)
    u = jax.nn.gelu(zu_ref[0].astype(F32))
    v = jax.nn.gelu(zv_ref[0].astype(F32))
    v = (v * lax.rsqrt(_seg_sum(v * v, ones) * (1.0 / HEAD_DIM) + NORM_EPS) * sgu_g_ref[...]).astype(BF16)
    gd = MLP_WIDTH // MLP_GROUPS
    chunks = []
    for ci in range(TM // MLP_CHUNK):
        vc = v[ci * MLP_CHUNK:(ci + 1) * MLP_CHUNK]
        mixed = jnp.concatenate(
            [jnp.dot(sgu_w_ref[g], vc[:, g * gd:(g + 1) * gd], preferred_element_type=F32)
             for g in range(MLP_GROUPS)], axis=1)
        chunks.append(mixed + sgu_b_ref[...])
    sg = u * jnp.concatenate(chunks, axis=0)
    mixed_all = jnp.concatenate([dn.astype(BF16), at, sg.astype(BF16)], axis=1)
    y = jnp.dot(mixed_all, w_out_ref[...], preferred_element_type=F32)
    x = x_ref[0] + m[2:3] * y
    ox[0] = x
    h = _ada_norm(x, g2_ref[...], m[3:4], m[4:5])
    oh[0] = h.astype(oh.dtype)
    if moe:
        ologit[0] = jnp.dot(h, rw_ref[...], precision=HI, preferred_element_type=F32) + rb_ref[...]


def _mix(x_all, mod_l, o_dn, zgate, attn, zu, zv2, dn_g, sgu_g, sgu_w, sgu_b, w_out, gain2, router=None):
    moe = router is not None
    nt = N_TILES - 1 if moe else N_TILES
    rows = nt * TM
    row = lambda b, i: (b, i, 0)
    orow = row
    const2 = lambda b, i: (0, 0)
    gd = MLP_WIDTH // MLP_GROUPS
    sgu_b_x = jnp.repeat(sgu_b.T, gd, axis=1)
    in_specs = [pl.BlockSpec((1, TM, D_MODEL), row),
                pl.BlockSpec((1, 6, D_MODEL), _mod_index),
                pl.BlockSpec((1, TM, DN_WIDTH), row),
                pl.BlockSpec((1, TM, DN_WIDTH), row),
                pl.BlockSpec((1, ATTN_HEADS, TM, HEAD_DIM), lambda b, i: (b, 0, i, 0)),
                pl.BlockSpec((1, TM, MLP_WIDTH), row),
                pl.BlockSpec((1, TM, MLP_WIDTH), row),
                pl.BlockSpec((1, DN_WIDTH), const2),
                pl.BlockSpec((1, MLP_WIDTH), const2),
                pl.BlockSpec((MLP_GROUPS, MLP_CHUNK, MLP_CHUNK), lambda b, i: (0, 0, 0)),
                pl.BlockSpec((MLP_CHUNK, MLP_WIDTH), const2),
                pl.BlockSpec((D_MIX, D_MODEL), const2),
                pl.BlockSpec((1, D_MODEL), const2)]
    args = [x_all, mod_l, o_dn, zgate, attn, zu, zv2, jnp.tile(dn_g.reshape(1, HEAD_DIM), (1, DN_HEADS)),
            sgu_g.reshape(1, MLP_WIDTH), sgu_w.astype(BF16), sgu_b_x, w_out, gain2.reshape(1, D_MODEL)]
    out_specs = [pl.BlockSpec((1, TM, D_MODEL), orow), pl.BlockSpec((1, TM, D_MODEL), orow)]
    out_shape = [jax.ShapeDtypeStruct((BATCH, rows, D_MODEL), F32),
                 jax.ShapeDtypeStruct((BATCH, rows, D_MODEL), F32 if moe else BF16)]
    if moe:
        rw, rb = router
        in_specs += [pl.BlockSpec((D_MODEL, LANES), const2), pl.BlockSpec((1, LANES), const2)]
        args += [jnp.pad(rw, ((0, 0), (0, LANES - MOE_EXPERTS))),
                 jnp.pad(rb.reshape(1, MOE_EXPERTS), ((0, 0), (0, LANES - MOE_EXPERTS)))]
        out_specs.append(pl.BlockSpec((1, TM, LANES), orow))
        out_shape.append(jax.ShapeDtypeStruct((BATCH, rows, LANES), F32))
    return pl.pallas_call(
        functools.partial(_mix_kernel, moe=moe),
        grid=(BATCH, nt),
        in_specs=in_specs, out_specs=out_specs, out_shape=out_shape,
        compiler_params=_cparams("parallel", "arbitrary"),
        name="mix_moe" if moe else "mix",
    )(*args)


FFN_SPLIT = 2
FFN_BLK = D_FF // FFN_SPLIT


def _ffn_kernel(x_ref, h_ref, mod_ref, w1_ref, w3_ref, w2_ref, o_ref):
    h = h_ref[0]
    y = None
    for c in range(FFN_SPLIT):
        sl = slice(c * FFN_BLK, (c + 1) * FFN_BLK)
        a = jnp.dot(h, w1_ref[:, sl], preferred_element_type=F32)
        b = jnp.dot(h, w3_ref[:, sl], preferred_element_type=F32)
        part = jnp.dot((jax.nn.silu(a) * b).astype(BF16), w2_ref[sl, :], preferred_element_type=F32)
        y = part if y is None else y + part
    o_ref[0] = x_ref[0] + mod_ref[0][5:6] * y


def _ffn(x_all, h_all, mod_l, w1, w3, w2):
    row = lambda b, i: (b, i, 0)
    const2 = lambda b, i: (0, 0)
    return pl.pallas_call(
        _ffn_kernel,
        grid=(BATCH, N_TILES),
        in_specs=[pl.BlockSpec((1, TM, D_MODEL), row),
                  pl.BlockSpec((1, TM, D_MODEL), row),
                  pl.BlockSpec((1, 6, D_MODEL), _mod_index),
                  pl.BlockSpec((D_MODEL, D_FF), const2),
                  pl.BlockSpec((D_MODEL, D_FF), const2),
                  pl.BlockSpec((D_FF, D_MODEL), const2)],
        out_specs=pl.BlockSpec((1, TM, D_MODEL), row),
        out_shape=jax.ShapeDtypeStruct((BATCH, TT, D_MODEL), F32),
        compiler_params=_cparams("parallel", "arbitrary"),
        name="ffn",
    )(x_all, h_all, mod_l, w1, w3, w2)


def _route_kernel(logit_ref, dest_ref, gate_ref, count_ref, run_scr):
    phase, t = pl.program_id(0), pl.program_id(1)

    @pl.when((phase == 0) & (t == 0))
    def _():
        run_scr[...] = jnp.zeros_like(run_scr)

    lane = lax.broadcasted_iota(jnp.int32, (ROUTE_TILE, LANES), 1).astype(F32)
    logits = jnp.where(lane < MOE_EXPERTS, logit_ref[...], -jnp.inf)
    m1 = jnp.max(logits, axis=-1, keepdims=True)
    e1 = jnp.min(jnp.where(logits == m1, lane, float(LANES)), axis=-1, keepdims=True)
    rest = jnp.where(lane == e1, -jnp.inf, logits)
    m2 = jnp.max(rest, axis=-1, keepdims=True)
    e2 = jnp.min(jnp.where(rest == m2, lane, float(LANES)), axis=-1, keepdims=True)
    hot1 = (lane == e1).astype(F32)
    hot2 = (lane == e2).astype(F32)
    hot = hot1 + hot2
    tile_count = jnp.sum(hot, axis=0, keepdims=True)

    @pl.when(phase == 0)
    def _():
        run_scr[0:1] = run_scr[0:1] + tile_count

    @pl.when((phase == 1) & (t == 0))
    def _():
        counts = jnp.broadcast_to(run_scr[0:1], (8, LANES))
        count_ref[...] = counts.astype(jnp.int32)
        padded = jnp.ceil(counts * (1.0 / MOE_BLOCK)) * MOE_BLOCK
        ei = lax.broadcasted_iota(jnp.int32, (LANES, LANES), 0)
        ej = lax.broadcasted_iota(jnp.int32, (LANES, LANES), 1)
        before = (ei < ej).astype(F32)
        run_scr[1:2] = jnp.dot(padded, before, precision=HI, preferred_element_type=F32)[0:1]

    @pl.when(phase == 1)
    def _():
        ri = lax.broadcasted_iota(jnp.int32, (ROUTE_TILE, ROUTE_TILE), 0)
        rj = lax.broadcasted_iota(jnp.int32, (ROUTE_TILE, ROUTE_TILE), 1)
        earlier = (ri > rj).astype(BF16)
        within = jnp.dot(earlier, hot.astype(BF16), preferred_element_type=F32)
        pos = within + run_scr[1:2]
        d1 = jnp.sum(pos * hot1, axis=-1, keepdims=True)
        d2 = jnp.sum(pos * hot2, axis=-1, keepdims=True)
        dest_ref[...] = jnp.concatenate([d1, d2], axis=1).astype(jnp.int32)
        w2 = jnp.exp(m2 - m1)
        gate_ref[...] = jnp.concatenate([1.0 / (1.0 + w2), w2 / (1.0 + w2)], axis=1)
        run_scr[1:2] = run_scr[1:2] + tile_count


def _route(logits):
    nt = N_LAT // ROUTE_TILE
    return pl.pallas_call(
        _route_kernel,
        grid=(2, nt),
        in_specs=[pl.BlockSpec((ROUTE_TILE, LANES), lambda p, t: (t, 0))],
        out_specs=[pl.BlockSpec((ROUTE_TILE, MOE_TOP_K), lambda p, t: (t * p, 0)),
                   pl.BlockSpec((ROUTE_TILE, MOE_TOP_K), lambda p, t: (t * p, 0)),
                   pl.BlockSpec((8, LANES), lambda p, t: (0, 0))],
        out_shape=[jax.ShapeDtypeStruct((N_LAT, MOE_TOP_K), jnp.int32),
                   jax.ShapeDtypeStruct((N_LAT, MOE_TOP_K), F32),
                   jax.ShapeDtypeStruct((8, LANES), jnp.int32)],
        scratch_shapes=[pltpu.VMEM((8, LANES), F32)],
        compiler_params=_cparams("arbitrary", "arbitrary"),
        name="moe_route",
    )(logits)


def _dispatch_kernel(dest_ref, h_ref, xs_in_ref, xs_ref, sem):
    del xs_in_ref
    base = pl.program_id(0) * TM

    def copy(r, k):
        return pltpu.make_async_copy(h_ref.at[pl.ds(r, 1)],
                                     xs_ref.at[pl.ds(dest_ref[(base + r) * MOE_TOP_K + k], 1)], sem)

    def start(r, c):
        for k in range(MOE_TOP_K):
            copy(r, k).start()
        return c

    def wait(r, c):
        for k in range(MOE_TOP_K):
            copy(r, k).wait()
        return c

    lax.fori_loop(0, TM, start, 0)
    lax.fori_loop(0, TM, wait, 0)


def _dispatch(dest_flat, h_lat):
    grid_spec = pltpu.PrefetchScalarGridSpec(
        num_scalar_prefetch=1,
        grid=(N_LAT // TM,),
        in_specs=[pl.BlockSpec((TM, D_MODEL), lambda i, d: (i, 0)),
                  pl.BlockSpec(memory_space=pl.ANY)],
        out_specs=pl.BlockSpec(memory_space=pl.ANY),
        scratch_shapes=[pltpu.SemaphoreType.DMA(())])
    return pl.pallas_call(
        _dispatch_kernel,
        grid_spec=grid_spec,
        out_shape=jax.ShapeDtypeStruct((MOE_ROWS, D_MODEL), F32),
        input_output_aliases={2: 0},
        compiler_params=_cparams("arbitrary"),
        name="moe_dispatch",
    )(dest_flat, h_lat, jnp.zeros((MOE_ROWS, D_MODEL), F32))


def _expert_kernel(be_ref, xs_ref, w1_ref, w3_ref, w2_ref, *rest, first):
    del be_ref
    if first:
        (o_ref,) = rest
    else:
        acc_ref, o_ref = rest
    x = xs_ref[...].astype(BF16)
    a = jnp.dot(x, w1_ref[0], preferred_element_type=F32)
    b = jnp.dot(x, w3_ref[0], preferred_element_type=F32)
    y = jnp.dot((jax.nn.silu(a) * b).astype(BF16), w2_ref[0], preferred_element_type=F32)
    o_ref[...] = y if first else acc_ref[...] + y


def _experts(block_expert, xs, w1, w3, w2):
    yb = None
    for c in range(MOE_FF_SPLIT):
        first = c == 0
        in_specs = [pl.BlockSpec((MOE_BLOCK, D_MODEL), lambda j, be: (j, 0)),
                    pl.BlockSpec((1, D_MODEL, MOE_FF_BLK), lambda j, be, c=c: (be[j], 0, c)),
                    pl.BlockSpec((1, D_MODEL, MOE_FF_BLK), lambda j, be, c=c: (be[j], 0, c)),
                    pl.BlockSpec((1, MOE_FF_BLK, D_MODEL), lambda j, be, c=c: (be[j], c, 0))]
        args = [block_expert, xs, w1, w3, w2]
        if not first:
            in_specs.append(pl.BlockSpec((MOE_BLOCK, D_MODEL), lambda j, be: (j, 0)))
            args.append(yb)
        yb = pl.pallas_call(
            functools.partial(_expert_kernel, first=first),
            grid_spec=pltpu.PrefetchScalarGridSpec(
                num_scalar_prefetch=1, grid=(MOE_N_BLOCKS,), in_specs=in_specs,
                out_specs=pl.BlockSpec((MOE_BLOCK, D_MODEL), lambda j, be: (j, 0))),
            out_shape=jax.ShapeDtypeStruct((MOE_ROWS, D_MODEL), F32),
            input_output_aliases={} if first else {5: 0},
            compiler_params=_cparams("arbitrary"),
            name="moe_experts_%d" % c,
        )(*args)
    return yb


def _combine_kernel(dest_ref, x_ref, gate_ref, mod_ref, g_ref, yb_ref, o_ref, buf, sem):
    base = (pl.program_id(0) * (SEQ // TM) + pl.program_id(1)) * TM

    def copy(r, k):
        return pltpu.make_async_copy(yb_ref.at[pl.ds(dest_ref[(base + r) * MOE_TOP_K + k], 1)],
                                     buf.at[k, pl.ds(r, 1)], sem)

    def start(r, c):
        for k in range(MOE_TOP_K):
            copy(r, k).start()
        return c

    def wait(r, c):
        for k in range(MOE_TOP_K):
            copy(r, k).wait()
        return c

    lax.fori_loop(0, TM, start, 0)
    lax.fori_loop(0, TM, wait, 0)
    gates = gate_ref[0]
    y = buf[0] * gates[:, 0:1] + buf[1] * gates[:, 1:2]
    x = x_ref[0] + mod_ref[0][5:6] * y
    o_ref[0] = x * lax.rsqrt(jnp.mean(x * x, axis=-1, keepdims=True) + NORM_EPS) * g_ref[...]


def _combine(dest_flat, x_lat, gates, mod_l, final_g, yb):
    grid_spec = pltpu.PrefetchScalarGridSpec(
        num_scalar_prefetch=1,
        grid=(BATCH, SEQ // TM),
        in_specs=[pl.BlockSpec((1, TM, D_MODEL), lambda b, i, d: (b, i, 0)),
                  pl.BlockSpec((1, TM, MOE_TOP_K), lambda b, i, d: (b, i, 0)),
                  pl.BlockSpec((1, 6, D_MODEL), lambda b, i, d: (b, 0, 0)),
                  pl.BlockSpec((1, D_MODEL), lambda b, i, d: (0, 0)),
                  pl.BlockSpec(memory_space=pl.ANY)],
        out_specs=pl.BlockSpec((1, TM, D_MODEL), lambda b, i, d: (b, i, 0)),
        scratch_shapes=[pltpu.VMEM((MOE_TOP_K, TM, D_MODEL), F32), pltpu.SemaphoreType.DMA(())])
    return pl.pallas_call(
        _combine_kernel,
        grid_spec=grid_spec,
        out_shape=jax.ShapeDtypeStruct((BATCH, SEQ, D_MODEL), F32),
        compiler_params=_cparams("arbitrary", "arbitrary"),
        name="moe_combine",
    )(dest_flat, x_lat, gates.reshape(BATCH, SEQ, MOE_TOP_K), mod_l, final_g.reshape(1, D_MODEL), yb)


def _reorder_w_in(w):
    s = np.cumsum((3 * DN_WIDTH, DN_WIDTH, 2 * DN_HEADS, 2 * DN_HEADS, ATTN_WIDTH, ATTN_KV_WIDTH, ATTN_KV_WIDTH,
                   MLP_WIDTH, MLP_WIDTH)).tolist()
    ba = w[:, s[1]:s[3]]
    return jnp.concatenate([w[:, :s[1]], w[:, s[3]:], ba,
                            jnp.zeros((D_MODEL, LANES - 4 * DN_HEADS), w.dtype)], axis=1).astype(BF16)


def kernel(x, c, ctx, c_ctx, mod_w, mod_b, norm1_g, norm2_g, w_in, conv_w, dn_a_log, dn_dt_bias, dn_norm_g,
           q_norm_g, k_norm_g, sgu_norm_g, sgu_w, sgu_b, w_out, ffn_w1, ffn_w3, ffn_w2, router_w, router_b,
           moe_w1, moe_w3, moe_w2, final_norm_g):
    assert DEPTH == 2 and x.shape == (BATCH, SEQ, D_MODEL) and ctx.shape == (BATCH, CTX_LEN, D_MODEL)
    cond8 = jnp.concatenate([c, c_ctx[None], jnp.zeros((8 - BATCH - 1, D_MODEL), F32)], axis=0)
    mod = _modulation(cond8, mod_w, mod_b)
    rope_c, rope_s = _rope_tables()
    x_all = jnp.concatenate([x, ctx], axis=1)
    for layer in range(DEPTH):
        last = layer == DEPTH - 1
        zqkv, zgate, zq, zk, zv, zu, zv2, zba = _in_proj(x_all, mod[layer], norm1_g[layer],
                                                         _reorder_w_in(w_in[layer]))
        qkv = _dn_prep(zqkv, conv_w[layer])
        o_dn = _deltanet(qkv, zba, dn_a_log[layer], dn_dt_bias[layer])
        qh, kh, vh = _attn_prep(zq, zk, zv, q_norm_g[layer], k_norm_g[layer], rope_c, rope_s)
        attn = _attention_lat(qh, kh, vh)
        if not last:
            attn = jnp.concatenate([attn, _attention_ctx(qh, kh, vh)], axis=2)
        mix_args = (x_all, mod[layer], o_dn, zgate, attn, zu, zv2, dn_norm_g[layer], sgu_norm_g[layer],
                    sgu_w[layer], sgu_b[layer], w_out[layer].astype(BF16), norm2_g[layer])
        if not last:
            i = layer // 2
            x_mid, h_mid = _mix(*mix_args)
            x_all = _ffn(x_mid, h_mid, mod[layer], ffn_w1[i].astype(BF16), ffn_w3[i].astype(BF16),
                         ffn_w2[i].astype(BF16))
        else:
            i = layer // 2
            x_lat, h_lat, logits = _mix(*mix_args, router=(router_w[i], router_b[i]))
            dest, gates, counts = _route(logits.reshape(N_LAT, LANES))
            padded = (counts[0, :MOE_EXPERTS] + MOE_BLOCK - 1) // MOE_BLOCK * MOE_BLOCK
            pad_ends = jnp.cumsum(padded)
            block_expert = jnp.minimum(
                jnp.sum(pad_ends[None, :] <= (jnp.arange(MOE_N_BLOCKS) * MOE_BLOCK)[:, None], axis=1),
                MOE_EXPERTS - 1).astype(jnp.int32)
            dest_flat = dest.reshape(N_ASSIGN)
            xs = _dispatch(dest_flat, h_lat.reshape(N_LAT, D_MODEL))
            yb = _experts(block_expert, xs, moe_w1[i].astype(BF16), moe_w3[i].astype(BF16),
                          moe_w2[i].astype(BF16))
            return _combine(dest_flat, x_lat, gates, mod[layer], final_norm_g, yb)
```

```python
import functools

import jax
import jax.numpy as jnp
import numpy as np
from jax import lax
from jax.experimental import pallas as pl
from jax.experimental.pallas import tpu as pltpu

D_MODEL = 1024
BATCH = 4
SEQ = 4096
DEPTH = 2
GRID_W = 64
CTX_LEN = 256
HEAD_DIM = 64
DN_HEADS = 6
ATTN_HEADS = 6
ATTN_KV_HEADS = 2
ATTN_GROUP = ATTN_HEADS // ATTN_KV_HEADS
MLP_GROUPS = 4
DN_WIDTH = DN_HEADS * HEAD_DIM
ATTN_WIDTH = ATTN_HEADS * HEAD_DIM
ATTN_KV_WIDTH = ATTN_KV_HEADS * HEAD_DIM
MLP_WIDTH = MLP_GROUPS * HEAD_DIM
D_MIX = DN_WIDTH + ATTN_WIDTH + MLP_WIDTH
CONV_K = 3
DN_CHUNK = 64
ATTN_SCALE = HEAD_DIM ** -0.5
LOG2E = 1.4426950408889634
MLP_CHUNK = 128
ROPE_THETA = 10000.0
ROPE_AXIS_DIM = HEAD_DIM // 2
ROPE_PAIRS = ROPE_AXIS_DIM // 2
D_FF = 2816
MOE_EXPERTS = 8
MOE_TOP_K = 2
MOE_D_FF = 3584
MOE_BLOCK = 256
NORM_EPS = 1e-6

LANES = 128
TT = SEQ + CTX_LEN
TM = 256
N_TILES = TT // TM
CTX_TILE = N_TILES - 1
CTX_ROW = BATCH
N_DN_STEPS = TT // DN_CHUNK
N_CTX_CHUNKS = CTX_LEN // DN_CHUNK
N_LAT_CHUNKS = SEQ // DN_CHUNK
IN_PAD = 3 * DN_WIDTH + DN_WIDTH + ATTN_WIDTH + 2 * ATTN_KV_WIDTH + 2 * MLP_WIDTH + LANES
N_LAT = BATCH * SEQ
N_ASSIGN = N_LAT * MOE_TOP_K
MOE_N_BLOCKS = -(-(N_ASSIGN + MOE_EXPERTS * (MOE_BLOCK - 1)) // MOE_BLOCK)
MOE_ROWS = MOE_N_BLOCKS * MOE_BLOCK
MOE_FF_SPLIT = 2
MOE_FF_BLK = MOE_D_FF // MOE_FF_SPLIT
ROUTE_TILE = 512
VMEM_LIMIT = 56 * 2 ** 20

F32 = jnp.float32
BF16 = jnp.bfloat16
HI = lax.Precision.HIGHEST


def _cparams(*sem):
    return pltpu.CompilerParams(dimension_semantics=sem, vmem_limit_bytes=VMEM_LIMIT)


def _bdot(a, b):
    return jnp.dot(a.astype(BF16), b.astype(BF16), preferred_element_type=F32)


def _bdot_nt(a, b):
    return lax.dot_general(a.astype(BF16), b.astype(BF16), (((1,), (1,)), ((), ())), preferred_element_type=F32)


def _bdot_tn(a, b):
    return lax.dot_general(a.astype(BF16), b.astype(BF16), (((0,), (0,)), ((), ())), preferred_element_type=F32)


def _seg_ones():
    r = lax.shift_right_logical(lax.broadcasted_iota(jnp.int32, (LANES, LANES), 0), 6)
    c = lax.shift_right_logical(lax.broadcasted_iota(jnp.int32, (LANES, LANES), 1), 6)
    return (r == c).astype(F32)


def _seg_sum(y, ones):
    parts = [jnp.dot(y[:, i:i + LANES], ones, precision=HI, preferred_element_type=F32)
             for i in range(0, y.shape[-1], LANES)]
    return parts[0] if len(parts) == 1 else jnp.concatenate(parts, axis=-1)


def _softplus(x):
    return jnp.maximum(x, 0.0) + jnp.log1p(jnp.exp(-jnp.abs(x)))


def _ada_norm(x, gain, shift, scale):
    y = x * lax.rsqrt(jnp.mean(x * x, axis=-1, keepdims=True) + NORM_EPS) * gain
    return y * (1.0 + scale) + shift


def _mod_index(b, i):
    return (jnp.where(i == CTX_TILE, CTX_ROW, b), 0, 0)


def _mod_kernel(c_ref, w_ref, b_ref, o_ref):
    cond = jax.nn.silu(c_ref[...])
    o_ref[0] = jnp.dot(cond, w_ref[0], precision=HI, preferred_element_type=F32) + b_ref[0]


def _modulation(cond8, mod_w, mod_b):
    nblk = 4
    bn = 6 * D_MODEL // nblk
    out = pl.pallas_call(
        _mod_kernel,
        grid=(DEPTH, nblk),
        in_specs=[pl.BlockSpec((8, D_MODEL), lambda l, j: (0, 0)),
                  pl.BlockSpec((1, D_MODEL, bn), lambda l, j: (l, 0, j)),
                  pl.BlockSpec((1, 1, bn), lambda l, j: (l, 0, j))],
        out_specs=pl.BlockSpec((1, 8, bn), lambda l, j: (l, 0, j)),
        out_shape=jax.ShapeDtypeStruct((DEPTH, 8, 6 * D_MODEL), F32),
        compiler_params=_cparams("arbitrary", "arbitrary"),
        name="modulation",
    )(cond8, mod_w, mod_b.reshape(DEPTH, 1, 6 * D_MODEL))
    return out.reshape(DEPTH, 8, 6, D_MODEL)


_IN_SPLITS = (3 * DN_WIDTH, DN_WIDTH, ATTN_WIDTH, ATTN_KV_WIDTH, ATTN_KV_WIDTH, MLP_WIDTH, MLP_WIDTH, LANES)


def _in_proj_kernel(x_ref, mod_ref, g_ref, w_ref, oqkv, ogate, oq, ok, ov, ou, ov2, oba):
    m = mod_ref[0]
    h = _ada_norm(x_ref[0], g_ref[...], m[0:1], m[1:2]).astype(BF16)
    z = jnp.dot(h, w_ref[...], preferred_element_type=F32)
    off = 0
    for ref, width in zip((oqkv, ogate, oq, ok, ov, ou, ov2, oba), _IN_SPLITS):
        ref[0] = z[:, off:off + width].astype(ref.dtype)
        off += width


def _in_proj(x_all, mod_l, gain, w_in_r):
    dts = (BF16,) * 7 + (F32,)
    row = lambda b, i: (b, i, 0)
    return pl.pallas_call(
        _in_proj_kernel,
        grid=(BATCH, N_TILES),
        in_specs=[pl.BlockSpec((1, TM, D_MODEL), row),
                  pl.BlockSpec((1, 6, D_MODEL), _mod_index),
                  pl.BlockSpec((1, D_MODEL), lambda b, i: (0, 0)),
                  pl.BlockSpec((D_MODEL, IN_PAD), lambda b, i: (0, 0))],
        out_specs=[pl.BlockSpec((1, TM, w), row) for w in _IN_SPLITS],
        out_shape=[jax.ShapeDtypeStruct((BATCH, TT, w), dt) for w, dt in zip(_IN_SPLITS, dts)],
        compiler_params=_cparams("parallel", "arbitrary"),
        name="in_proj",
    )(x_all, mod_l, gain.reshape(1, D_MODEL), w_in_r)


def _dn_prep_kernel(z_ref, w_ref, o_ref):
    j = pl.program_id(1)
    z = z_ref[0].astype(F32)
    w = w_ref[...]
    row = lax.broadcasted_iota(jnp.int32, (TT, 1), 0)
    first = (row == 0) | (row == SEQ)
    last = (row == SEQ - 1) | (row == TT - 1)
    zp = jnp.where(first, 0.0, pltpu.roll(z, 1, 0))
    zn = jnp.where(last, 0.0, pltpu.roll(z, TT - 1, 0))
    y = jax.nn.silu(w[0:1] * zp + w[1:2] * z + w[2:3] * zn)
    n_qk = 2 * DN_WIDTH // LANES
    n_q = DN_WIDTH // LANES

    @pl.when(j < n_qk)
    def _():
        inv = lax.rsqrt(_seg_sum(y * y, _seg_ones()) + NORM_EPS)
        o_ref[0] = (y * inv * jnp.where(j < n_q, HEAD_DIM ** -0.5, 1.0)).astype(o_ref.dtype)

    @pl.when(j >= n_qk)
    def _():
        o_ref[0] = y.astype(o_ref.dtype)


def _dn_prep(zqkv, conv_w):
    nb = 3 * DN_WIDTH // LANES
    return pl.pallas_call(
        _dn_prep_kernel,
        grid=(BATCH, nb),
        in_specs=[pl.BlockSpec((1, TT, LANES), lambda b, j: (b, 0, j)),
                  pl.BlockSpec((CONV_K, LANES), lambda b, j: (0, j))],
        out_specs=pl.BlockSpec((1, TT, LANES), lambda b, j: (b, 0, j)),
        out_shape=jax.ShapeDtypeStruct((BATCH, TT, 3 * DN_WIDTH), BF16),
        compiler_params=_cparams("parallel", "arbitrary"),
        name="dn_prep",
    )(zqkv, conv_w)


DN_PAIRS = DN_HEADS // 2
DN_SUB = 2
LOG2_CHUNK = DN_CHUNK.bit_length() - 1


def _dn_kernel(qkv_ref, zba_ref, a_ref, dt_ref, o_ref, s_scr):
    C, P2 = DN_CHUNK, 2 * DN_CHUNK
    o_ref[...] = jnp.zeros_like(o_ref)
    s_scr[...] = jnp.zeros_like(s_scr)
    neg_decay_rate = -jnp.exp(a_ref[...])
    dt_bias = dt_ref[...]
    lo = lax.broadcasted_iota(jnp.int32, (1, P2), 1) < C

    def stack(a):
        return jnp.concatenate([jnp.where(lo, a, 0.0), jnp.where(lo, 0.0, a)], axis=0)

    def step(i, carry):
        ri = lax.broadcasted_iota(jnp.int32, (P2, P2), 0)
        ci = lax.broadcasted_iota(jnp.int32, (P2, P2), 1)
        ti = lax.broadcasted_iota(jnp.int32, (C, C), 0)
        tj = lax.broadcasted_iota(jnp.int32, (C, C), 1)
        same = lambda sh: lax.shift_right_logical(ri, sh) == lax.shift_right_logical(ci, sh)
        same_head = same(LOG2_CHUNK)
        eye = (ri == ci).astype(F32)

        ch = []
        for sub in range(DN_SUB):
            s = i * DN_SUB + sub
            chunk_of = (jnp.where(s < N_CTX_CHUNKS, N_LAT_CHUNKS + s, s - N_CTX_CHUNKS), N_DN_STEPS - 1 - s)
            for d in range(2):
                r0 = pl.multiple_of(chunk_of[d] * C, C)
                zba = zba_ref[0, pl.ds(r0, C), :]
                beta_all = jax.nn.sigmoid(zba)
                g_all = neg_decay_rate * _softplus(zba + dt_bias)
                tri = (ti >= tj) if d == 0 else (ti <= tj)
                gc_all = jnp.dot(tri.astype(F32), g_all, precision=HI, preferred_element_type=F32)
                gc_t = jnp.concatenate([gc_all, gc_all], axis=0).T
                last = C - 1 if d == 0 else 0
                for p in range(DN_PAIRS):
                    lb = (d * DN_HEADS + 2 * p, d * DN_HEADS + 2 * p + 1)
                    la = (2 * DN_HEADS + lb[0], 2 * DN_HEADS + lb[1])
                    col = lambda t, l: t[:, l:l + 1]
                    q2 = qkv_ref[0, pl.ds(r0, C), p * LANES:(p + 1) * LANES].astype(F32)
                    k2 = qkv_ref[0, pl.ds(r0, C), DN_WIDTH + p * LANES:DN_WIDTH + (p + 1) * LANES].astype(F32)
                    v2 = qkv_ref[0, pl.ds(r0, C), 2 * DN_WIDTH + p * LANES:2 * DN_WIDTH + (p + 1) * LANES].astype(F32)
                    beta2 = jnp.where(lo, col(beta_all, lb[0]), col(beta_all, lb[1]))
                    gc2 = jnp.where(lo, col(gc_all, la[0]), col(gc_all, la[1]))
                    gcol = jnp.concatenate([col(gc_all, la[0]), col(gc_all, la[1])], axis=0)
                    grow = jnp.where(lo, gc_t[la[0]:la[0] + 1, :], gc_t[la[1]:la[1] + 1, :])
                    g_last = jnp.where(lo, gc_all[last:last + 1, la[0]:la[0] + 1],
                                       gc_all[last:last + 1, la[1]:la[1] + 1])
                    e_diff = jnp.exp(gcol - grow)
                    order = (ri >= ci) if d == 0 else (ri <= ci)
                    kb2 = k2 * beta2
                    e_gc = jnp.exp(gc2)
                    ch.append(dict(
                        r0=r0, sub=sub, d=d, p=p, g_last=g_last,
                        dec_incl=jnp.where(same_head & order, e_diff, 0.0),
                        dec_strict=jnp.where(same_head & order & (ri != ci), e_diff, 0.0),
                        kq=jnp.concatenate([stack(kb2), stack(q2)], axis=0).astype(BF16),
                        k=stack(k2).astype(BF16),
                        rhs=jnp.concatenate([stack(v2 * beta2), stack(kb2 * e_gc)], axis=1).astype(BF16),
                        kd=stack(k2 * jnp.exp(g_last - gc2)).astype(BF16),
                        qe=stack(q2 * e_gc)))
        for c in ch:
            kk_qk = lax.dot_general(c["kq"], c["k"], (((1,), (1,)), ((), ())), preferred_element_type=F32)
            c["a"] = kk_qk[:P2] * c["dec_strict"]
            c["attn"] = (kk_qk[P2:] * c["dec_incl"]).astype(BF16)
            c["t"] = eye - jnp.where(same(1), c["a"], 0.0)
        for lvl in range(1, LOG2_CHUNK):
            joins = same(lvl + 1) & ~same(lvl)
            for c in ch:
                c["m"] = _bdot(jnp.where(joins, c["a"], 0.0), c["t"])
            for c in ch:
                c["t"] = c["t"] - _bdot(c["t"], c["m"])
        for c in ch:
            c["uw"] = _bdot(c["t"], c["rhs"]).astype(BF16)
        for c in ch:
            oa = jnp.dot(c["attn"], c["uw"], preferred_element_type=F32)
            c["o0"] = oa[:, :LANES]
            c["qp"] = c["qe"] - oa[:, LANES:]
            c["np"] = lax.dot_general(c["kd"], c["uw"], (((0,), (0,)), ((), ())), preferred_element_type=F32)

        states = [s_scr[j] for j in range(2 * DN_PAIRS)]
        for sub in range(DN_SUB):
            cur = [c for c in ch if c["sub"] == sub]
            for c in cur:
                j = c["d"] * DN_PAIRS + c["p"]
                c["r"] = _bdot(jnp.concatenate([c["qp"], c["np"][:, LANES:]], axis=0), states[j])
            for d in range(2):
                outs = []
                for c in cur:
                    if c["d"] != d:
                        continue
                    j = d * DN_PAIRS + c["p"]
                    o_st = c["o0"] + c["r"][:P2]
                    outs.append(o_st[:C] + o_st[C:])
                    states[j] = states[j] * jnp.exp(c["g_last"]) + c["np"][:, :LANES] - c["r"][P2:]
                    r0 = c["r0"]
                o_ref[0, pl.ds(r0, C), :] += jnp.concatenate(outs, axis=1)
        for j in range(2 * DN_PAIRS):
            s_scr[j] = states[j]
        return carry

    lax.fori_loop(0, N_DN_STEPS // DN_SUB, step, 0)


def _deltanet(qkv, zba, a_log, dt_bias):
    pad = lambda t: jnp.zeros((1, LANES), F32).at[0, 2 * DN_HEADS:4 * DN_HEADS].set(t.reshape(-1))
    return pl.pallas_call(
        _dn_kernel,
        grid=(BATCH,),
        in_specs=[pl.BlockSpec((1, TT, 3 * DN_WIDTH), lambda b: (b, 0, 0)),
                  pl.BlockSpec((1, TT, LANES), lambda b: (b, 0, 0)),
                  pl.BlockSpec((1, LANES), lambda b: (0, 0)),
                  pl.BlockSpec((1, LANES), lambda b: (0, 0))],
        out_specs=pl.BlockSpec((1, TT, DN_WIDTH), lambda b: (b, 0, 0)),
        out_shape=jax.ShapeDtypeStruct((BATCH, TT, DN_WIDTH), F32),
        scratch_shapes=[pltpu.VMEM((2 * DN_PAIRS, LANES, LANES), F32)],
        compiler_params=_cparams("parallel"),
        name="deltanet",
    )(qkv, zba, pad(a_log), pad(dt_bias))


def _rope_tables():
    rows = SEQ // GRID_W
    row = jnp.repeat(jnp.arange(rows, dtype=F32), GRID_W)
    col = jnp.tile(jnp.arange(GRID_W, dtype=F32), rows)
    inv = ROPE_THETA ** (-2.0 * jnp.arange(ROPE_PAIRS, dtype=F32) / ROPE_AXIS_DIM)
    ang = jnp.stack([row[:, None] * inv, col[:, None] * inv], axis=1)
    cos, sin = jnp.cos(ang), jnp.sin(ang)
    c = jnp.concatenate([cos[:, 0], cos[:, 0], cos[:, 1], cos[:, 1]], axis=-1)
    s = jnp.concatenate([-sin[:, 0], sin[:, 0], -sin[:, 1], sin[:, 1]], axis=-1)
    c = jnp.concatenate([c, jnp.ones((CTX_LEN, HEAD_DIM), F32)], axis=0)
    s = jnp.concatenate([s, jnp.zeros((CTX_LEN, HEAD_DIM), F32)], axis=0)
    return jnp.tile(c, (1, 2)), jnp.tile(s, (1, 2))


def _attn_prep_kernel(zq_ref, zk_ref, zv_ref, gq_ref, gk_ref, c_ref, s_ref, oq, ok, ov):
    ones = _seg_ones()
    cs, sn = c_ref[...], s_ref[...]
    lane = lax.broadcasted_iota(jnp.int32, (1, LANES), 1)
    first_half = (lane & (2 * ROPE_PAIRS - 1)) < ROPE_PAIRS

    def norm_rope(x, gain):
        y = x * lax.rsqrt(_seg_sum(x * x, ones) * (1.0 / HEAD_DIM) + NORM_EPS) * gain
        partner = jnp.where(first_half, pltpu.roll(y, LANES - ROPE_PAIRS, 1), pltpu.roll(y, ROPE_PAIRS, 1))
        return y * cs + partner * sn

    zq = zq_ref[0].astype(F32)
    for pair in range(ATTN_HEADS // 2):
        q2 = norm_rope(zq[:, pair * LANES:(pair + 1) * LANES], gq_ref[...]) * (ATTN_SCALE * LOG2E)
        oq[0, 2 * pair] = q2[:, :HEAD_DIM].astype(oq.dtype)
        oq[0, 2 * pair + 1] = q2[:, HEAD_DIM:].astype(oq.dtype)
    k2 = norm_rope(zk_ref[0].astype(F32), gk_ref[...])
    v2 = zv_ref[0]
    one_cols = jnp.ones((TM, HEAD_DIM), ov.dtype)
    for h in range(ATTN_KV_HEADS):
        ok[0, h] = k2[:, h * HEAD_DIM:(h + 1) * HEAD_DIM].astype(ok.dtype)
        ov[0, h] = jnp.concatenate([v2[:, h * HEAD_DIM:(h + 1) * HEAD_DIM], one_cols], axis=1)


def _attn_prep(zq, zk, zv, q_gain, k_gain, rope_c, rope_s):
    row = lambda b, i: (b, i, 0)
    hrow = lambda b, i: (b, 0, i, 0)
    tile2 = lambda g: jnp.tile(g.reshape(1, HEAD_DIM), (1, 2))
    return pl.pallas_call(
        _attn_prep_kernel,
        grid=(BATCH, N_TILES),
        in_specs=[pl.BlockSpec((1, TM, ATTN_WIDTH), row),
                  pl.BlockSpec((1, TM, ATTN_KV_WIDTH), row),
                  pl.BlockSpec((1, TM, ATTN_KV_WIDTH), row),
                  pl.BlockSpec((1, LANES), lambda b, i: (0, 0)),
                  pl.BlockSpec((1, LANES), lambda b, i: (0, 0)),
                  pl.BlockSpec((TM, LANES), lambda b, i: (i, 0)),
                  pl.BlockSpec((TM, LANES), lambda b, i: (i, 0))],
        out_specs=[pl.BlockSpec((1, ATTN_HEADS, TM, HEAD_DIM), hrow),
                   pl.BlockSpec((1, ATTN_KV_HEADS, TM, HEAD_DIM), hrow),
                   pl.BlockSpec((1, ATTN_KV_HEADS, TM, LANES), hrow)],
        out_shape=[jax.ShapeDtypeStruct((BATCH, ATTN_HEADS, TT, HEAD_DIM), BF16),
                   jax.ShapeDtypeStruct((BATCH, ATTN_KV_HEADS, TT, HEAD_DIM), BF16),
                   jax.ShapeDtypeStruct((BATCH, ATTN_KV_HEADS, TT, LANES), BF16)],
        compiler_params=_cparams("parallel", "arbitrary"),
        name="attn_prep",
    )(zq, zk, zv, tile2(q_gain), tile2(k_gain), rope_c, rope_s)


ATTN_TQ = 128
ATTN_NQ = 8
ATTN_KB = 512
ATTN_KEY_BLOCKS = tuple((j * ATTN_KB, ATTN_KB) for j in range(SEQ // ATTN_KB)) + ((SEQ, CTX_LEN),)


def _dot_nt(a, b):
    return lax.dot_general(a, b, (((1,), (1,)), ((), ())), preferred_element_type=F32)


def _attn_finish(acc):
    return acc[:, :HEAD_DIM] / acc[:, HEAD_DIM:HEAD_DIM + 1]


def _attn_lat_kernel(q_ref, k_ref, v_ref, o_ref, s0_scr, s1_scr):
    rows = ATTN_GROUP * ATTN_TQ
    s_scr = (s0_scr, s1_scr)
    mpart = [None] * ATTN_NQ
    acc = [None] * ATTN_NQ
    for stage in range(ATTN_NQ + 1):
        a, b = stage, stage - 1
        if a < ATTN_NQ:
            qa = q_ref[0, :, a * ATTN_TQ:(a + 1) * ATTN_TQ, :].reshape(rows, HEAD_DIM)
        if b >= 0:
            m_b = jnp.max(mpart[b], axis=-1, keepdims=True)
        for k0, kn in ATTN_KEY_BLOCKS:
            if a < ATTN_NQ:
                s = _dot_nt(qa, k_ref[0, 0, k0:k0 + kn, :])
                s_scr[a % 2][:, k0:k0 + kn] = s
                blk = functools.reduce(jnp.maximum, [s[:, i:i + LANES] for i in range(0, kn, LANES)])
                mpart[a] = blk if mpart[a] is None else jnp.maximum(mpart[a], blk)
            if b >= 0:
                p = jnp.exp2(s_scr[b % 2][:, k0:k0 + kn] - m_b).astype(BF16)
                pv = jnp.dot(p, v_ref[0, 0, k0:k0 + kn, :], preferred_element_type=F32)
                acc[b] = pv if acc[b] is None else acc[b] + pv
        if b >= 0:
            o = _attn_finish(acc[b]).reshape(ATTN_GROUP, ATTN_TQ, HEAD_DIM)
            o_ref[0, :, b * ATTN_TQ:(b + 1) * ATTN_TQ, :] = o.astype(o_ref.dtype)


def _attn_ctx_kernel(q_ref, k_ref, v_ref, o_ref):
    q = q_ref[0].reshape(ATTN_GROUP * CTX_LEN, HEAD_DIM)
    s = _dot_nt(q, k_ref[0, 0])
    p = jnp.exp2(s - jnp.max(s, axis=-1, keepdims=True)).astype(BF16)
    o = _attn_finish(jnp.dot(p, v_ref[0, 0], preferred_element_type=F32))
    o_ref[0] = o.reshape(ATTN_GROUP, CTX_LEN, HEAD_DIM).astype(o_ref.dtype)


def _attention_lat(qh, kh, vh):
    tq = ATTN_NQ * ATTN_TQ
    rows = ATTN_GROUP * ATTN_TQ
    return pl.pallas_call(
        _attn_lat_kernel,
        grid=(BATCH, ATTN_KV_HEADS, SEQ // tq),
        in_specs=[pl.BlockSpec((1, ATTN_GROUP, tq, HEAD_DIM), lambda b, g, i: (b, g, i, 0)),
                  pl.BlockSpec((1, 1, TT, HEAD_DIM), lambda b, g, i: (b, g, 0, 0)),
                  pl.BlockSpec((1, 1, TT, LANES), lambda b, g, i: (b, g, 0, 0))],
        out_specs=pl.BlockSpec((1, ATTN_GROUP, tq, HEAD_DIM), lambda b, g, i: (b, g, i, 0)),
        out_shape=jax.ShapeDtypeStruct((BATCH, ATTN_HEADS, SEQ, HEAD_DIM), BF16),
        scratch_shapes=[pltpu.VMEM((rows, TT), F32), pltpu.VMEM((rows, TT), F32)],
        compiler_params=_cparams("parallel", "parallel", "arbitrary"),
        name="attention_lat",
    )(qh, kh, vh)


def _attention_ctx(qh, kh, vh):
    ctx_blk = SEQ // CTX_LEN
    return pl.pallas_call(
        _attn_ctx_kernel,
        grid=(BATCH, ATTN_KV_HEADS),
        in_specs=[pl.BlockSpec((1, ATTN_GROUP, CTX_LEN, HEAD_DIM), lambda b, g: (b, g, ctx_blk, 0)),
                  pl.BlockSpec((1, 1, CTX_LEN, HEAD_DIM), lambda b, g: (b, g, ctx_blk, 0)),
                  pl.BlockSpec((1, 1, CTX_LEN, LANES), lambda b, g: (b, g, ctx_blk, 0))],
        out_specs=pl.BlockSpec((1, ATTN_GROUP, CTX_LEN, HEAD_DIM), lambda b, g: (b, g, 0, 0)),
        out_shape=jax.ShapeDtypeStruct((BATCH, ATTN_HEADS, CTX_LEN, HEAD_DIM), BF16),
        compiler_params=_cparams("parallel", "arbitrary"),
        name="attention_ctx",
    )(qh, kh, vh)


def _mix_kernel(x_ref, mod_ref, o_dn_ref, zgate_ref, attn_ref, zu_ref, zv_ref, dn_g_ref, sgu_g_ref, sgu_w_ref,
                sgu_b_ref, w_out_ref, g2_ref, *rest, moe):
    if moe:
        rw_ref, rb_ref, ox, oh, ologit = rest
    else:
        ox, oh = rest
    ones = _seg_ones()
    m = mod_ref[0]
    o = o_dn_ref[0]
    dn = o * lax.rsqrt(_seg_sum(o * o, ones) * (1.0 / HEAD_DIM) + NORM_EPS) * dn_g_ref[...]
    dn = dn * jax.nn.silu(zgate_ref[0].astype(F32))
    at = jnp.concatenate([attn_ref[0, h] for h in range(ATTN_HEADS)], axis=1)
    u = jax.nn.gelu(zu_ref[0].astype(F32))
    v = jax.nn.gelu(zv_ref[0].astype(F32))
    v = (v * lax.rsqrt(_seg_sum(v * v, ones) * (1.0 / HEAD_DIM) + NORM_EPS) * sgu_g_ref[...]).astype(BF16)
    gd = MLP_WIDTH // MLP_GROUPS
    chunks = []
    for ci in range(TM // MLP_CHUNK):
        vc = v[ci * MLP_CHUNK:(ci + 1) * MLP_CHUNK]
        mixed = jnp.concatenate(
            [jnp.dot(sgu_w_ref[g], vc[:, g * gd:(g + 1) * gd], preferred_element_type=F32)
             for g in range(MLP_GROUPS)], axis=1)
        chunks.append(mixed + sgu_b_ref[...])
    sg = u * jnp.concatenate(chunks, axis=0)
    mixed_all = jnp.concatenate([dn.astype(BF16), at, sg.astype(BF16)], axis=1)
    y = jnp.dot(mixed_all, w_out_ref[...], preferred_element_type=F32)
    x = x_ref[0] + m[2:3] * y
    ox[0] = x
    h = _ada_norm(x, g2_ref[...], m[3:4], m[4:5])
    oh[0] = h.astype(oh.dtype)
    if moe:
        ologit[0] = jnp.dot(h, rw_ref[...], precision=HI, preferred_element_type=F32) + rb_ref[...]


def _mix(x_all, mod_l, o_dn, zgate, attn, zu, zv2, dn_g, sgu_g, sgu_w, sgu_b, w_out, gain2, router=None):
    moe = router is not None
    nt = N_TILES - 1 if moe else N_TILES
    rows = nt * TM
    row = lambda b, i: (b, i, 0)
    orow = row
    const2 = lambda b, i: (0, 0)
    gd = MLP_WIDTH // MLP_GROUPS
    sgu_b_x = jnp.repeat(sgu_b.T, gd, axis=1)
    in_specs = [pl.BlockSpec((1, TM, D_MODEL), row),
                pl.BlockSpec((1, 6, D_MODEL), _mod_index),
                pl.BlockSpec((1, TM, DN_WIDTH), row),
                pl.BlockSpec((1, TM, DN_WIDTH), row),
                pl.BlockSpec((1, ATTN_HEADS, TM, HEAD_DIM), lambda b, i: (b, 0, i, 0)),
                pl.BlockSpec((1, TM, MLP_WIDTH), row),
                pl.BlockSpec((1, TM, MLP_WIDTH), row),
                pl.BlockSpec((1, DN_WIDTH), const2),
                pl.BlockSpec((1, MLP_WIDTH), const2),
                pl.BlockSpec((MLP_GROUPS, MLP_CHUNK, MLP_CHUNK), lambda b, i: (0, 0, 0)),
                pl.BlockSpec((MLP_CHUNK, MLP_WIDTH), const2),
                pl.BlockSpec((D_MIX, D_MODEL), const2),
                pl.BlockSpec((1, D_MODEL), const2)]
    args = [x_all, mod_l, o_dn, zgate, attn, zu, zv2, jnp.tile(dn_g.reshape(1, HEAD_DIM), (1, DN_HEADS)),
            sgu_g.reshape(1, MLP_WIDTH), sgu_w.astype(BF16), sgu_b_x, w_out, gain2.reshape(1, D_MODEL)]
    out_specs = [pl.BlockSpec((1, TM, D_MODEL), orow), pl.BlockSpec((1, TM, D_MODEL), orow)]
    out_shape = [jax.ShapeDtypeStruct((BATCH, rows, D_MODEL), F32),
                 jax.ShapeDtypeStruct((BATCH, rows, D_MODEL), F32 if moe else BF16)]
    if moe:
        rw, rb = router
        in_specs += [pl.BlockSpec((D_MODEL, LANES), const2), pl.BlockSpec((1, LANES), const2)]
        args += [jnp.pad(rw, ((0, 0), (0, LANES - MOE_EXPERTS))),
                 jnp.pad(rb.reshape(1, MOE_EXPERTS), ((0, 0), (0, LANES - MOE_EXPERTS)))]
        out_specs.append(pl.BlockSpec((1, TM, LANES), orow))
        out_shape.append(jax.ShapeDtypeStruct((BATCH, rows, LANES), F32))
    return pl.pallas_call(
        functools.partial(_mix_kernel, moe=moe),
        grid=(BATCH, nt),
        in_specs=in_specs, out_specs=out_specs, out_shape=out_shape,
        compiler_params=_cparams("parallel", "arbitrary"),
        name="mix_moe" if moe else "mix",
    )(*args)


FFN_SPLIT = 2
FFN_BLK = D_FF // FFN_SPLIT


def _ffn_kernel(x_ref, h_ref, mod_ref, w1_ref, w3_ref, w2_ref, o_ref):
    h = h_ref[0]
    y = None
    for c in range(FFN_SPLIT):
        sl = slice(c * FFN_BLK, (c + 1) * FFN_BLK)
        a = jnp.dot(h, w1_ref[:, sl], preferred_element_type=F32)
        b = jnp.dot(h, w3_ref[:, sl], preferred_element_type=F32)
        part = jnp.dot((jax.nn.silu(a) * b).astype(BF16), w2_ref[sl, :], preferred_element_type=F32)
        y = part if y is None else y + part
    o_ref[0] = x_ref[0] + mod_ref[0][5:6] * y


def _ffn(x_all, h_all, mod_l, w1, w3, w2):
    row = lambda b, i: (b, i, 0)
    const2 = lambda b, i: (0, 0)
    return pl.pallas_call(
        _ffn_kernel,
        grid=(BATCH, N_TILES),
        in_specs=[pl.BlockSpec((1, TM, D_MODEL), row),
                  pl.BlockSpec((1, TM, D_MODEL), row),
                  pl.BlockSpec((1, 6, D_MODEL), _mod_index),
                  pl.BlockSpec((D_MODEL, D_FF), const2),
                  pl.BlockSpec((D_MODEL, D_FF), const2),
                  pl.BlockSpec((D_FF, D_MODEL), const2)],
        out_specs=pl.BlockSpec((1, TM, D_MODEL), row),
        out_shape=jax.ShapeDtypeStruct((BATCH, TT, D_MODEL), F32),
        compiler_params=_cparams("parallel", "arbitrary"),
        name="ffn",
    )(x_all, h_all, mod_l, w1, w3, w2)


def _route_kernel(logit_ref, dest_ref, gate_ref, count_ref, run_scr):
    phase, t = pl.program_id(0), pl.program_id(1)

    @pl.when((phase == 0) & (t == 0))
    def _():
        run_scr[...] = jnp.zeros_like(run_scr)

    lane = lax.broadcasted_iota(jnp.int32, (ROUTE_TILE, LANES), 1).astype(F32)
    logits = jnp.where(lane < MOE_EXPERTS, logit_ref[...], -jnp.inf)
    m1 = jnp.max(logits, axis=-1, keepdims=True)
    e1 = jnp.min(jnp.where(logits == m1, lane, float(LANES)), axis=-1, keepdims=True)
    rest = jnp.where(lane == e1, -jnp.inf, logits)
    m2 = jnp.max(rest, axis=-1, keepdims=True)
    e2 = jnp.min(jnp.where(rest == m2, lane, float(LANES)), axis=-1, keepdims=True)
    hot1 = (lane == e1).astype(F32)
    hot2 = (lane == e2).astype(F32)
    hot = hot1 + hot2
    tile_count = jnp.sum(hot, axis=0, keepdims=True)

    @pl.when(phase == 0)
    def _():
        run_scr[0:1] = run_scr[0:1] + tile_count

    @pl.when((phase == 1) & (t == 0))
    def _():
        counts = jnp.broadcast_to(run_scr[0:1], (8, LANES))
        count_ref[...] = counts.astype(jnp.int32)
        padded = jnp.ceil(counts * (1.0 / MOE_BLOCK)) * MOE_BLOCK
        ei = lax.broadcasted_iota(jnp.int32, (LANES, LANES), 0)
        ej = lax.broadcasted_iota(jnp.int32, (LANES, LANES), 1)
        before = (ei < ej).astype(F32)
        run_scr[1:2] = jnp.dot(padded, before, precision=HI, preferred_element_type=F32)[0:1]

    @pl.when(phase == 1)
    def _():
        ri = lax.broadcasted_iota(jnp.int32, (ROUTE_TILE, ROUTE_TILE), 0)
        rj = lax.broadcasted_iota(jnp.int32, (ROUTE_TILE, ROUTE_TILE), 1)
        earlier = (ri > rj).astype(BF16)
        within = jnp.dot(earlier, hot.astype(BF16), preferred_element_type=F32)
        pos = within + run_scr[1:2]
        d1 = jnp.sum(pos * hot1, axis=-1, keepdims=True)
        d2 = jnp.sum(pos * hot2, axis=-1, keepdims=True)
        dest_ref[...] = jnp.concatenate([d1, d2], axis=1).astype(jnp.int32)
        w2 = jnp.exp(m2 - m1)
        gate_ref[...] = jnp.concatenate([1.0 / (1.0 + w2), w2 / (1.0 + w2)], axis=1)
        run_scr[1:2] = run_scr[1:2] + tile_count


def _route(logits):
    nt = N_LAT // ROUTE_TILE
    return pl.pallas_call(
        _route_kernel,
        grid=(2, nt),
        in_specs=[pl.BlockSpec((ROUTE_TILE, LANES), lambda p, t: (t, 0))],
        out_specs=[pl.BlockSpec((ROUTE_TILE, MOE_TOP_K), lambda p, t: (t * p, 0)),
                   pl.BlockSpec((ROUTE_TILE, MOE_TOP_K), lambda p, t: (t * p, 0)),
                   pl.BlockSpec((8, LANES), lambda p, t: (0, 0))],
        out_shape=[jax.ShapeDtypeStruct((N_LAT, MOE_TOP_K), jnp.int32),
                   jax.ShapeDtypeStruct((N_LAT, MOE_TOP_K), F32),
                   jax.ShapeDtypeStruct((8, LANES), jnp.int32)],
        scratch_shapes=[pltpu.VMEM((8, LANES), F32)],
        compiler_params=_cparams("arbitrary", "arbitrary"),
        name="moe_route",
    )(logits)


N_PAD_SLOTS = MOE_ROWS - N_ASSIGN
Y2_ROWS = N_LAT + N_PAD_SLOTS // MOE_TOP_K


def _invert_kernel(dest_ref, bounds_ref, slot_ref):
    def real(a, c):
        slot_ref[dest_ref[a]] = a
        return c

    lax.fori_loop(0, N_ASSIGN, real, 0, unroll=8)

    def pad_range(e, count):
        def pad(s, cnt):
            slot_ref[s] = N_ASSIGN + cnt
            return cnt + 1

        return lax.fori_loop(bounds_ref[2 * e], bounds_ref[2 * e + 1], pad, count)

    lax.fori_loop(0, MOE_EXPERTS + 1, pad_range, 0)


def _invert(dest_flat, pad_bounds):
    return pl.pallas_call(
        _invert_kernel,
        grid_spec=pltpu.PrefetchScalarGridSpec(
            num_scalar_prefetch=2, grid=(1,), in_specs=[],
            out_specs=pl.BlockSpec(memory_space=pltpu.SMEM)),
        out_shape=jax.ShapeDtypeStruct((MOE_ROWS,), jnp.int32),
        compiler_params=_cparams("arbitrary"),
        name="moe_invert",
    )(dest_flat, pad_bounds)


LAST_BLOCK = MOE_N_BLOCKS - 1


def _swiglu_half(x, w1_ref, w3_ref, w2_ref):
    a = jnp.dot(x, w1_ref[0], preferred_element_type=F32)
    b = jnp.dot(x, w3_ref[0], preferred_element_type=F32)
    return jnp.dot((jax.nn.silu(a) * b).astype(BF16), w2_ref[0], preferred_element_type=F32)


def _expert_gather_kernel(be_ref, slot_ref, h_hbm, w1_ref, w3_ref, w2_ref, xs_ref, yb_ref, xbuf, sem):
    del be_ref
    j = pl.program_id(0)
    cur = j % 2

    def gather(blk, buf):
        base = blk * MOE_BLOCK
        for r in range(MOE_BLOCK):
            row = jnp.minimum(lax.shift_right_logical(slot_ref[base + r], 1), N_LAT - 1)
            pltpu.make_async_copy(h_hbm.at[pl.ds(row, 1)], xbuf.at[buf, pl.ds(r, 1)], sem.at[buf]).start()

    def wait(buf):
        pltpu.make_async_copy(h_hbm.at[pl.ds(0, MOE_BLOCK)], xbuf.at[buf], sem.at[buf]).wait()

    @pl.when(j == 0)
    def _():
        gather(0, 0)

    wait(cur)
    gather(jnp.minimum(j + 1, LAST_BLOCK), 1 - cur)
    x = xbuf[cur]
    xs_ref[...] = x
    yb_ref[...] = _swiglu_half(x.astype(BF16), w1_ref, w3_ref, w2_ref)

    @pl.when(j == LAST_BLOCK)
    def _():
        wait(1 - cur)


def _expert_scatter_kernel(be_ref, slot_ref, xs_ref, w1_ref, w3_ref, w2_ref, yb_ref, y2_hbm, obuf, sem):
    del be_ref
    j = pl.program_id(0)
    cur = j % 2

    def scatter(blk, buf):
        base = blk * MOE_BLOCK
        for r in range(MOE_BLOCK):
            a = slot_ref[base + r]
            pltpu.make_async_copy(obuf.at[buf, pl.ds(r, 1)],
                                  y2_hbm.at[a & 1, pl.ds(lax.shift_right_logical(a, 1), 1)], sem.at[buf]).start()

    def wait(buf):
        pltpu.make_async_copy(obuf.at[buf], y2_hbm.at[0, pl.ds(0, MOE_BLOCK)], sem.at[buf]).wait()

    def compute():
        obuf[cur] = yb_ref[...] + _swiglu_half(xs_ref[...].astype(BF16), w1_ref, w3_ref, w2_ref)

    @pl.when(j >= 2)
    def _():
        wait(cur)

    @pl.when(j == 0)
    def _():
        compute()

    @pl.when(j > 0)
    def _():
        scatter(j - 1, 1 - cur)
        compute()

    @pl.when(j == LAST_BLOCK)
    def _():
        scatter(j, cur)
        wait(1 - cur)
        wait(cur)


def _experts(block_expert, slot_src, h_lat, w1, w3, w2):
    def w_specs(c):
        return [pl.BlockSpec((1, D_MODEL, MOE_FF_BLK), lambda j, be, sl: (be[j], 0, c)),
                pl.BlockSpec((1, D_MODEL, MOE_FF_BLK), lambda j, be, sl: (be[j], 0, c)),
                pl.BlockSpec((1, MOE_FF_BLK, D_MODEL), lambda j, be, sl: (be[j], c, 0))]

    blk = pl.BlockSpec((MOE_BLOCK, D_MODEL), lambda j, be, sl: (j, 0))
    xs, yb = pl.pallas_call(
        _expert_gather_kernel,
        grid_spec=pltpu.PrefetchScalarGridSpec(
            num_scalar_prefetch=2, grid=(MOE_N_BLOCKS,),
            in_specs=[pl.BlockSpec(memory_space=pl.ANY)] + w_specs(0),
            out_specs=[blk, blk],
            scratch_shapes=[pltpu.VMEM((2, MOE_BLOCK, D_MODEL), F32), pltpu.SemaphoreType.DMA((2,))]),
        out_shape=[jax.ShapeDtypeStruct((MOE_ROWS, D_MODEL), F32)] * 2,
        compiler_params=_cparams("arbitrary"),
        name="moe_experts_gather",
    )(block_expert, slot_src, h_lat, w1, w3, w2)
    return pl.pallas_call(
        _expert_scatter_kernel,
        grid_spec=pltpu.PrefetchScalarGridSpec(
            num_scalar_prefetch=2, grid=(MOE_N_BLOCKS,),
            in_specs=[blk] + w_specs(1) + [blk],
            out_specs=pl.BlockSpec(memory_space=pl.ANY),
            scratch_shapes=[pltpu.VMEM((2, MOE_BLOCK, D_MODEL), F32), pltpu.SemaphoreType.DMA((2,))]),
        out_shape=jax.ShapeDtypeStruct((MOE_TOP_K, Y2_ROWS, D_MODEL), F32),
        compiler_params=_cparams("arbitrary"),
        name="moe_experts_scatter",
    )(block_expert, slot_src, xs, w1, w3, w2, yb)


def _combine_kernel(x_ref, gate_ref, mod_ref, g_ref, y2_ref, o_ref):
    gates = gate_ref[0]
    y = y2_ref[0] * gates[:, 0:1] + y2_ref[1] * gates[:, 1:2]
    x = x_ref[0] + mod_ref[0][5:6] * y
    o_ref[0] = x * lax.rsqrt(jnp.mean(x * x, axis=-1, keepdims=True) + NORM_EPS) * g_ref[...]


def _combine(x_lat, gates, mod_l, final_g, y2):
    nt = SEQ // TM
    return pl.pallas_call(
        _combine_kernel,
        grid=(BATCH, nt),
        in_specs=[pl.BlockSpec((1, TM, D_MODEL), lambda b, i: (b, i, 0)),
                  pl.BlockSpec((1, TM, MOE_TOP_K), lambda b, i: (b, i, 0)),
                  pl.BlockSpec((1, 6, D_MODEL), lambda b, i: (b, 0, 0)),
                  pl.BlockSpec((1, D_MODEL), lambda b, i: (0, 0)),
                  pl.BlockSpec((MOE_TOP_K, TM, D_MODEL), lambda b, i: (0, b * nt + i, 0))],
        out_specs=pl.BlockSpec((1, TM, D_MODEL), lambda b, i: (b, i, 0)),
        out_shape=jax.ShapeDtypeStruct((BATCH, SEQ, D_MODEL), F32),
        compiler_params=_cparams("parallel", "arbitrary"),
        name="moe_combine",
    )(x_lat, gates.reshape(BATCH, SEQ, MOE_TOP_K), mod_l, final_g.reshape(1, D_MODEL), y2)


def _reorder_w_in(w):
    s = np.cumsum((3 * DN_WIDTH, DN_WIDTH, 2 * DN_HEADS, 2 * DN_HEADS, ATTN_WIDTH, ATTN_KV_WIDTH, ATTN_KV_WIDTH,
                   MLP_WIDTH, MLP_WIDTH)).tolist()
    ba = w[:, s[1]:s[3]]
    return jnp.concatenate([w[:, :s[1]], w[:, s[3]:], ba,
                            jnp.zeros((D_MODEL, LANES - 4 * DN_HEADS), w.dtype)], axis=1).astype(BF16)


def kernel(x, c, ctx, c_ctx, mod_w, mod_b, norm1_g, norm2_g, w_in, conv_w, dn_a_log, dn_dt_bias, dn_norm_g,
           q_norm_g, k_norm_g, sgu_norm_g, sgu_w, sgu_b, w_out, ffn_w1, ffn_w3, ffn_w2, router_w, router_b,
           moe_w1, moe_w3, moe_w2, final_norm_g):
    assert DEPTH == 2 and x.shape == (BATCH, SEQ, D_MODEL) and ctx.shape == (BATCH, CTX_LEN, D_MODEL)
    cond8 = jnp.concatenate([c, c_ctx[None], jnp.zeros((8 - BATCH - 1, D_MODEL), F32)], axis=0)
    mod = _modulation(cond8, mod_w, mod_b)
    rope_c, rope_s = _rope_tables()
    x_all = jnp.concatenate([x, ctx], axis=1)
    for layer in range(DEPTH):
        last = layer == DEPTH - 1
        zqkv, zgate, zq, zk, zv, zu, zv2, zba = _in_proj(x_all, mod[layer], norm1_g[layer],
                                                         _reorder_w_in(w_in[layer]))
        qkv = _dn_prep(zqkv, conv_w[layer])
        o_dn = _deltanet(qkv, zba, dn_a_log[layer], dn_dt_bias[layer])
        qh, kh, vh = _attn_prep(zq, zk, zv, q_norm_g[layer], k_norm_g[layer], rope_c, rope_s)
        attn = _attention_lat(qh, kh, vh)
        if not last:
            attn = jnp.concatenate([attn, _attention_ctx(qh, kh, vh)], axis=2)
        mix_args = (x_all, mod[layer], o_dn, zgate, attn, zu, zv2, dn_norm_g[layer], sgu_norm_g[layer],
                    sgu_w[layer], sgu_b[layer], w_out[layer].astype(BF16), norm2_g[layer])
        if not last:
            i = layer // 2
            x_mid, h_mid = _mix(*mix_args)
            x_all = _ffn(x_mid, h_mid, mod[layer], ffn_w1[i].astype(BF16), ffn_w3[i].astype(BF16),
                         ffn_w2[i].astype(BF16))
        else:
            i = layer // 2
            x_lat, h_lat, logits = _mix(*mix_args, router=(router_w[i], router_b[i]))
            dest, gates, counts = _route(logits.reshape(N_LAT, LANES))
            cnt = counts[0, :MOE_EXPERTS]
            padded = (cnt + MOE_BLOCK - 1) // MOE_BLOCK * MOE_BLOCK
            pad_ends = jnp.cumsum(padded)
            block_expert = jnp.minimum(
                jnp.sum(pad_ends[None, :] <= (jnp.arange(MOE_N_BLOCKS) * MOE_BLOCK)[:, None], axis=1),
                MOE_EXPERTS - 1).astype(jnp.int32)
            lo = jnp.concatenate([pad_ends - padded + cnt, pad_ends[-1:]])
            hi = jnp.concatenate([pad_ends, jnp.full((1,), MOE_ROWS, pad_ends.dtype)])
            pad_bounds = jnp.stack([lo, hi], axis=1).reshape(-1).astype(jnp.int32)
            slot_src = _invert(dest.reshape(N_ASSIGN), pad_bounds)
            y2 = _experts(block_expert, slot_src, h_lat.reshape(N_LAT, D_MODEL), moe_w1[i].astype(BF16),
                          moe_w3[i].astype(BF16), moe_w2[i].astype(BF16))
            return _combine(x_lat, gates, mod[layer], final_norm_g, y2)
```

```python
import functools

import jax
import jax.numpy as jnp
import numpy as np
from jax import lax
from jax.experimental import pallas as pl
from jax.experimental.pallas import tpu as pltpu

D_MODEL = 1024
BATCH = 4
SEQ = 4096
DEPTH = 2
GRID_W = 64
CTX_LEN = 256
HEAD_DIM = 64
DN_HEADS = 6
ATTN_HEADS = 6
ATTN_KV_HEADS = 2
ATTN_GROUP = ATTN_HEADS // ATTN_KV_HEADS
MLP_GROUPS = 4
DN_WIDTH = DN_HEADS * HEAD_DIM
ATTN_WIDTH = ATTN_HEADS * HEAD_DIM
ATTN_KV_WIDTH = ATTN_KV_HEADS * HEAD_DIM
MLP_WIDTH = MLP_GROUPS * HEAD_DIM
D_MIX = DN_WIDTH + ATTN_WIDTH + MLP_WIDTH
CONV_K = 3
DN_CHUNK = 64
ATTN_SCALE = HEAD_DIM ** -0.5
LOG2E = 1.4426950408889634
MLP_CHUNK = 128
ROPE_THETA = 10000.0
ROPE_AXIS_DIM = HEAD_DIM // 2
ROPE_PAIRS = ROPE_AXIS_DIM // 2
D_FF = 2816
MOE_EXPERTS = 8
MOE_TOP_K = 2
MOE_D_FF = 3584
MOE_BLOCK = 256
NORM_EPS = 1e-6

LANES = 128
TT = SEQ + CTX_LEN
TM = 256
N_TILES = TT // TM
CTX_TILE = N_TILES - 1
CTX_ROW = BATCH
N_DN_STEPS = TT // DN_CHUNK
N_CTX_CHUNKS = CTX_LEN // DN_CHUNK
N_LAT_CHUNKS = SEQ // DN_CHUNK
IN_PAD = 3 * DN_WIDTH + DN_WIDTH + ATTN_WIDTH + 2 * ATTN_KV_WIDTH + 2 * MLP_WIDTH + LANES
N_LAT = BATCH * SEQ
N_ASSIGN = N_LAT * MOE_TOP_K
MOE_N_BLOCKS = -(-(N_ASSIGN + MOE_EXPERTS * (MOE_BLOCK - 1)) // MOE_BLOCK)
MOE_ROWS = MOE_N_BLOCKS * MOE_BLOCK
MOE_FF_SPLIT = 2
MOE_FF_BLK = MOE_D_FF // MOE_FF_SPLIT
ROUTE_TILE = 512
VMEM_LIMIT = 56 * 2 ** 20

F32 = jnp.float32
BF16 = jnp.bfloat16
HI = lax.Precision.HIGHEST


def _cparams(*sem):
    return pltpu.CompilerParams(dimension_semantics=sem, vmem_limit_bytes=VMEM_LIMIT)


def _bdot(a, b):
    return jnp.dot(a.astype(BF16), b.astype(BF16), preferred_element_type=F32)


def _bdot_nt(a, b):
    return lax.dot_general(a.astype(BF16), b.astype(BF16), (((1,), (1,)), ((), ())), preferred_element_type=F32)


def _bdot_tn(a, b):
    return lax.dot_general(a.astype(BF16), b.astype(BF16), (((0,), (0,)), ((), ())), preferred_element_type=F32)


def _seg_ones():
    r = lax.shift_right_logical(lax.broadcasted_iota(jnp.int32, (LANES, LANES), 0), 6)
    c = lax.shift_right_logical(lax.broadcasted_iota(jnp.int32, (LANES, LANES), 1), 6)
    return (r == c).astype(F32)


def _seg_sum(y, ones):
    parts = [jnp.dot(y[:, i:i + LANES], ones, precision=HI, preferred_element_type=F32)
             for i in range(0, y.shape[-1], LANES)]
    return parts[0] if len(parts) == 1 else jnp.concatenate(parts, axis=-1)


def _softplus(x):
    return jnp.maximum(x, 0.0) + jnp.log1p(jnp.exp(-jnp.abs(x)))


def _ada_norm(x, gain, shift, scale):
    y = x * lax.rsqrt(jnp.mean(x * x, axis=-1, keepdims=True) + NORM_EPS) * gain
    return y * (1.0 + scale) + shift


def _mod_index(b, i):
    return (jnp.where(i == CTX_TILE, CTX_ROW, b), 0, 0)


def _mod_kernel(c_ref, w_ref, b_ref, o_ref):
    cond = jax.nn.silu(c_ref[...])
    o_ref[0] = jnp.dot(cond, w_ref[0], precision=HI, preferred_element_type=F32) + b_ref[0]


def _modulation(cond8, mod_w, mod_b):
    nblk = 4
    bn = 6 * D_MODEL // nblk
    out = pl.pallas_call(
        _mod_kernel,
        grid=(DEPTH, nblk),
        in_specs=[pl.BlockSpec((8, D_MODEL), lambda l, j: (0, 0)),
                  pl.BlockSpec((1, D_MODEL, bn), lambda l, j: (l, 0, j)),
                  pl.BlockSpec((1, 1, bn), lambda l, j: (l, 0, j))],
        out_specs=pl.BlockSpec((1, 8, bn), lambda l, j: (l, 0, j)),
        out_shape=jax.ShapeDtypeStruct((DEPTH, 8, 6 * D_MODEL), F32),
        compiler_params=_cparams("arbitrary", "arbitrary"),
        name="modulation",
    )(cond8, mod_w, mod_b.reshape(DEPTH, 1, 6 * D_MODEL))
    return out.reshape(DEPTH, 8, 6, D_MODEL)


_IN_SPLITS = (3 * DN_WIDTH, DN_WIDTH, ATTN_WIDTH, ATTN_KV_WIDTH, ATTN_KV_WIDTH, MLP_WIDTH, MLP_WIDTH, LANES)


def _in_proj_kernel(x_ref, mod_ref, g_ref, w_ref, oqkv, ogate, oq, ok, ov, ou, ov2, oba):
    m = mod_ref[0]
    h = _ada_norm(x_ref[0], g_ref[...], m[0:1], m[1:2]).astype(BF16)
    z = jnp.dot(h, w_ref[...], preferred_element_type=F32)
    off = 0
    for ref, width in zip((oqkv, ogate, oq, ok, ov, ou, ov2, oba), _IN_SPLITS):
        ref[0] = z[:, off:off + width].astype(ref.dtype)
        off += width


def _in_proj(x_all, mod_l, gain, w_in_r):
    dts = (BF16,) * 7 + (F32,)
    row = lambda b, i: (b, i, 0)
    return pl.pallas_call(
        _in_proj_kernel,
        grid=(BATCH, N_TILES),
        in_specs=[pl.BlockSpec((1, TM, D_MODEL), row),
                  pl.BlockSpec((1, 6, D_MODEL), _mod_index),
                  pl.BlockSpec((1, D_MODEL), lambda b, i: (0, 0)),
                  pl.BlockSpec((D_MODEL, IN_PAD), lambda b, i: (0, 0))],
        out_specs=[pl.BlockSpec((1, TM, w), row) for w in _IN_SPLITS],
        out_shape=[jax.ShapeDtypeStruct((BATCH, TT, w), dt) for w, dt in zip(_IN_SPLITS, dts)],
        compiler_params=_cparams("parallel", "arbitrary"),
        name="in_proj",
    )(x_all, mod_l, gain.reshape(1, D_MODEL), w_in_r)


def _dn_prep_kernel(z_ref, w_ref, o_ref):
    j = pl.program_id(1)
    z = z_ref[0].astype(F32)
    w = w_ref[...]
    row = lax.broadcasted_iota(jnp.int32, (TT, 1), 0)
    first = (row == 0) | (row == SEQ)
    last = (row == SEQ - 1) | (row == TT - 1)
    zp = jnp.where(first, 0.0, pltpu.roll(z, 1, 0))
    zn = jnp.where(last, 0.0, pltpu.roll(z, TT - 1, 0))
    y = jax.nn.silu(w[0:1] * zp + w[1:2] * z + w[2:3] * zn)
    n_qk = 2 * DN_WIDTH // LANES
    n_q = DN_WIDTH // LANES

    @pl.when(j < n_qk)
    def _():
        inv = lax.rsqrt(_seg_sum(y * y, _seg_ones()) + NORM_EPS)
        o_ref[0] = (y * inv * jnp.where(j < n_q, HEAD_DIM ** -0.5, 1.0)).astype(o_ref.dtype)

    @pl.when(j >= n_qk)
    def _():
        o_ref[0] = y.astype(o_ref.dtype)


def _dn_prep(zqkv, conv_w):
    nb = 3 * DN_WIDTH // LANES
    return pl.pallas_call(
        _dn_prep_kernel,
        grid=(BATCH, nb),
        in_specs=[pl.BlockSpec((1, TT, LANES), lambda b, j: (b, 0, j)),
                  pl.BlockSpec((CONV_K, LANES), lambda b, j: (0, j))],
        out_specs=pl.BlockSpec((1, TT, LANES), lambda b, j: (b, 0, j)),
        out_shape=jax.ShapeDtypeStruct((BATCH, TT, 3 * DN_WIDTH), BF16),
        compiler_params=_cparams("parallel", "arbitrary"),
        name="dn_prep",
    )(zqkv, conv_w)


DN_PAIRS = DN_HEADS // 2
DN_SUB = 2
LOG2_CHUNK = DN_CHUNK.bit_length() - 1


def _dn_kernel(qkv_ref, zba_ref, a_ref, dt_ref, o_ref, s_scr):
    C, P2 = DN_CHUNK, 2 * DN_CHUNK
    o_ref[...] = jnp.zeros_like(o_ref)
    s_scr[...] = jnp.zeros_like(s_scr)
    neg_decay_rate = -jnp.exp(a_ref[...])
    dt_bias = dt_ref[...]
    lo = lax.broadcasted_iota(jnp.int32, (1, P2), 1) < C

    def stack(a):
        return jnp.concatenate([jnp.where(lo, a, 0.0), jnp.where(lo, 0.0, a)], axis=0)

    def step(i, carry):
        ri = lax.broadcasted_iota(jnp.int32, (P2, P2), 0)
        ci = lax.broadcasted_iota(jnp.int32, (P2, P2), 1)
        ti = lax.broadcasted_iota(jnp.int32, (C, C), 0)
        tj = lax.broadcasted_iota(jnp.int32, (C, C), 1)
        same = lambda sh: lax.shift_right_logical(ri, sh) == lax.shift_right_logical(ci, sh)
        same_head = same(LOG2_CHUNK)
        eye = (ri == ci).astype(F32)

        ch = []
        for sub in range(DN_SUB):
            s = i * DN_SUB + sub
            chunk_of = (jnp.where(s < N_CTX_CHUNKS, N_LAT_CHUNKS + s, s - N_CTX_CHUNKS), N_DN_STEPS - 1 - s)
            for d in range(2):
                r0 = pl.multiple_of(chunk_of[d] * C, C)
                zba = zba_ref[0, pl.ds(r0, C), :]
                beta_all = jax.nn.sigmoid(zba)
                g_all = neg_decay_rate * _softplus(zba + dt_bias)
                tri = (ti >= tj) if d == 0 else (ti <= tj)
                gc_all = jnp.dot(tri.astype(F32), g_all, precision=HI, preferred_element_type=F32)
                gc_t = jnp.concatenate([gc_all, gc_all], axis=0).T
                last = C - 1 if d == 0 else 0
                for p in range(DN_PAIRS):
                    lb = (d * DN_HEADS + 2 * p, d * DN_HEADS + 2 * p + 1)
                    la = (2 * DN_HEADS + lb[0], 2 * DN_HEADS + lb[1])
                    col = lambda t, l: t[:, l:l + 1]
                    q2 = qkv_ref[0, pl.ds(r0, C), p * LANES:(p + 1) * LANES].astype(F32)
                    k2 = qkv_ref[0, pl.ds(r0, C), DN_WIDTH + p * LANES:DN_WIDTH + (p + 1) * LANES].astype(F32)
                    v2 = qkv_ref[0, pl.ds(r0, C), 2 * DN_WIDTH + p * LANES:2 * DN_WIDTH + (p + 1) * LANES].astype(F32)
                    beta2 = jnp.where(lo, col(beta_all, lb[0]), col(beta_all, lb[1]))
                    gc2 = jnp.where(lo, col(gc_all, la[0]), col(gc_all, la[1]))
                    gcol = jnp.concatenate([col(gc_all, la[0]), col(gc_all, la[1])], axis=0)
                    grow = jnp.where(lo, gc_t[la[0]:la[0] + 1, :], gc_t[la[1]:la[1] + 1, :])
                    g_last = jnp.where(lo, gc_all[last:last + 1, la[0]:la[0] + 1],
                                       gc_all[last:last + 1, la[1]:la[1] + 1])
                    e_diff = jnp.exp(gcol - grow)
                    order = (ri >= ci) if d == 0 else (ri <= ci)
                    kb2 = k2 * beta2
                    e_gc = jnp.exp(gc2)
                    ch.append(dict(
                        r0=r0, sub=sub, d=d, p=p, g_last=g_last,
                        dec_incl=jnp.where(same_head & order, e_diff, 0.0),
                        dec_strict=jnp.where(same_head & order & (ri != ci), e_diff, 0.0),
                        kq=jnp.concatenate([stack(kb2), stack(q2)], axis=0).astype(BF16),
                        k=stack(k2).astype(BF16),
                        rhs=jnp.concatenate([stack(v2 * beta2), stack(kb2 * e_gc)], axis=1).astype(BF16),
                        kd=stack(k2 * jnp.exp(g_last - gc2)).astype(BF16),
                        qe=stack(q2 * e_gc)))
        for c in ch:
            kk_qk = lax.dot_general(c["kq"], c["k"], (((1,), (1,)), ((), ())), preferred_element_type=F32)
            c["a"] = kk_qk[:P2] * c["dec_strict"]
            c["attn"] = (kk_qk[P2:] * c["dec_incl"]).astype(BF16)
            c["t"] = eye - jnp.where(same(1), c["a"], 0.0)
        for lvl in range(1, LOG2_CHUNK):
            joins = same(lvl + 1) & ~same(lvl)
            for c in ch:
                c["m"] = _bdot(jnp.where(joins, c["a"], 0.0), c["t"])
            for c in ch:
                c["t"] = c["t"] - _bdot(c["t"], c["m"])
        for c in ch:
            c["uw"] = _bdot(c["t"], c["rhs"]).astype(BF16)
        for c in ch:
            oa = jnp.dot(c["attn"], c["uw"], preferred_element_type=F32)
            c["o0"] = oa[:, :LANES]
            c["qp"] = c["qe"] - oa[:, LANES:]
            c["np"] = lax.dot_general(c["kd"], c["uw"], (((0,), (0,)), ((), ())), preferred_element_type=F32)

        states = [s_scr[j] for j in range(2 * DN_PAIRS)]
        for sub in range(DN_SUB):
            cur = [c for c in ch if c["sub"] == sub]
            for c in cur:
                j = c["d"] * DN_PAIRS + c["p"]
                c["r"] = _bdot(jnp.concatenate([c["qp"], c["np"][:, LANES:]], axis=0), states[j])
            for d in range(2):
                outs = []
                for c in cur:
                    if c["d"] != d:
                        continue
                    j = d * DN_PAIRS + c["p"]
                    o_st = c["o0"] + c["r"][:P2]
                    outs.append(o_st[:C] + o_st[C:])
                    states[j] = states[j] * jnp.exp(c["g_last"]) + c["np"][:, :LANES] - c["r"][P2:]
                    r0 = c["r0"]
                o_ref[0, pl.ds(r0, C), :] += jnp.concatenate(outs, axis=1)
        for j in range(2 * DN_PAIRS):
            s_scr[j] = states[j]
        return carry

    lax.fori_loop(0, N_DN_STEPS // DN_SUB, step, 0)


def _deltanet(qkv, zba, a_log, dt_bias):
    pad = lambda t: jnp.zeros((1, LANES), F32).at[0, 2 * DN_HEADS:4 * DN_HEADS].set(t.reshape(-1))
    return pl.pallas_call(
        _dn_kernel,
        grid=(BATCH,),
        in_specs=[pl.BlockSpec((1, TT, 3 * DN_WIDTH), lambda b: (b, 0, 0)),
                  pl.BlockSpec((1, TT, LANES), lambda b: (b, 0, 0)),
                  pl.BlockSpec((1, LANES), lambda b: (0, 0)),
                  pl.BlockSpec((1, LANES), lambda b: (0, 0))],
        out_specs=pl.BlockSpec((1, TT, DN_WIDTH), lambda b: (b, 0, 0)),
        out_shape=jax.ShapeDtypeStruct((BATCH, TT, DN_WIDTH), F32),
        scratch_shapes=[pltpu.VMEM((2 * DN_PAIRS, LANES, LANES), F32)],
        compiler_params=_cparams("parallel"),
        name="deltanet",
    )(qkv, zba, pad(a_log), pad(dt_bias))


def _rope_tables():
    rows = SEQ // GRID_W
    row = jnp.repeat(jnp.arange(rows, dtype=F32), GRID_W)
    col = jnp.tile(jnp.arange(GRID_W, dtype=F32), rows)
    inv = ROPE_THETA ** (-2.0 * jnp.arange(ROPE_PAIRS, dtype=F32) / ROPE_AXIS_DIM)
    ang = jnp.stack([row[:, None] * inv, col[:, None] * inv], axis=1)
    cos, sin = jnp.cos(ang), jnp.sin(ang)
    c = jnp.concatenate([cos[:, 0], cos[:, 0], cos[:, 1], cos[:, 1]], axis=-1)
    s = jnp.concatenate([-sin[:, 0], sin[:, 0], -sin[:, 1], sin[:, 1]], axis=-1)
    c = jnp.concatenate([c, jnp.ones((CTX_LEN, HEAD_DIM), F32)], axis=0)
    s = jnp.concatenate([s, jnp.zeros((CTX_LEN, HEAD_DIM), F32)], axis=0)
    return jnp.tile(c, (1, 2)), jnp.tile(s, (1, 2))


def _attn_prep_kernel(zq_ref, zk_ref, zv_ref, gq_ref, gk_ref, c_ref, s_ref, oq, ok, ov):
    ones = _seg_ones()
    cs, sn = c_ref[...], s_ref[...]
    lane = lax.broadcasted_iota(jnp.int32, (1, LANES), 1)
    first_half = (lane & (2 * ROPE_PAIRS - 1)) < ROPE_PAIRS

    def norm_rope(x, gain):
        y = x * lax.rsqrt(_seg_sum(x * x, ones) * (1.0 / HEAD_DIM) + NORM_EPS) * gain
        partner = jnp.where(first_half, pltpu.roll(y, LANES - ROPE_PAIRS, 1), pltpu.roll(y, ROPE_PAIRS, 1))
        return y * cs + partner * sn

    zq = zq_ref[0].astype(F32)
    for pair in range(ATTN_HEADS // 2):
        q2 = norm_rope(zq[:, pair * LANES:(pair + 1) * LANES], gq_ref[...]) * (ATTN_SCALE * LOG2E)
        oq[0, 2 * pair] = q2[:, :HEAD_DIM].astype(oq.dtype)
        oq[0, 2 * pair + 1] = q2[:, HEAD_DIM:].astype(oq.dtype)
    k2 = norm_rope(zk_ref[0].astype(F32), gk_ref[...])
    v2 = zv_ref[0]
    one_cols = jnp.ones((TM, HEAD_DIM), ov.dtype)
    for h in range(ATTN_KV_HEADS):
        ok[0, h] = k2[:, h * HEAD_DIM:(h + 1) * HEAD_DIM].astype(ok.dtype)
        ov[0, h] = jnp.concatenate([v2[:, h * HEAD_DIM:(h + 1) * HEAD_DIM], one_cols], axis=1)


def _attn_prep(zq, zk, zv, q_gain, k_gain, rope_c, rope_s):
    row = lambda b, i: (b, i, 0)
    hrow = lambda b, i: (b, 0, i, 0)
    tile2 = lambda g: jnp.tile(g.reshape(1, HEAD_DIM), (1, 2))
    return pl.pallas_call(
        _attn_prep_kernel,
        grid=(BATCH, N_TILES),
        in_specs=[pl.BlockSpec((1, TM, ATTN_WIDTH), row),
                  pl.BlockSpec((1, TM, ATTN_KV_WIDTH), row),
                  pl.BlockSpec((1, TM, ATTN_KV_WIDTH), row),
                  pl.BlockSpec((1, LANES), lambda b, i: (0, 0)),
                  pl.BlockSpec((1, LANES), lambda b, i: (0, 0)),
                  pl.BlockSpec((TM, LANES), lambda b, i: (i, 0)),
                  pl.BlockSpec((TM, LANES), lambda b, i: (i, 0))],
        out_specs=[pl.BlockSpec((1, ATTN_HEADS, TM, HEAD_DIM), hrow),
                   pl.BlockSpec((1, ATTN_KV_HEADS, TM, HEAD_DIM), hrow),
                   pl.BlockSpec((1, ATTN_KV_HEADS, TM, LANES), hrow)],
        out_shape=[jax.ShapeDtypeStruct((BATCH, ATTN_HEADS, TT, HEAD_DIM), BF16),
                   jax.ShapeDtypeStruct((BATCH, ATTN_KV_HEADS, TT, HEAD_DIM), BF16),
                   jax.ShapeDtypeStruct((BATCH, ATTN_KV_HEADS, TT, LANES), BF16)],
        compiler_params=_cparams("parallel", "arbitrary"),
        name="attn_prep",
    )(zq, zk, zv, tile2(q_gain), tile2(k_gain), rope_c, rope_s)


ATTN_TQ = 128
ATTN_NQ = 8
ATTN_KB = 512
ATTN_KEY_BLOCKS = tuple((j * ATTN_KB, ATTN_KB) for j in range(SEQ // ATTN_KB)) + ((SEQ, CTX_LEN),)


def _dot_nt(a, b):
    return lax.dot_general(a, b, (((1,), (1,)), ((), ())), preferred_element_type=F32)


def _attn_finish(acc):
    return acc[:, :HEAD_DIM] / acc[:, HEAD_DIM:HEAD_DIM + 1]


def _attn_lat_kernel(q_ref, k_ref, v_ref, o_ref, s0_scr, s1_scr):
    rows = ATTN_GROUP * ATTN_TQ
    s_scr = (s0_scr, s1_scr)
    mpart = [None] * ATTN_NQ
    acc = [None] * ATTN_NQ
    for stage in range(ATTN_NQ + 1):
        a, b = stage, stage - 1
        if a < ATTN_NQ:
            qa = q_ref[0, :, a * ATTN_TQ:(a + 1) * ATTN_TQ, :].reshape(rows, HEAD_DIM)
        if b >= 0:
            m_b = jnp.max(mpart[b], axis=-1, keepdims=True)
        for k0, kn in ATTN_KEY_BLOCKS:
            if a < ATTN_NQ:
                s = _dot_nt(qa, k_ref[0, 0, k0:k0 + kn, :])
                s_scr[a % 2][:, k0:k0 + kn] = s
                blk = functools.reduce(jnp.maximum, [s[:, i:i + LANES] for i in range(0, kn, LANES)])
                mpart[a] = blk if mpart[a] is None else jnp.maximum(mpart[a], blk)
            if b >= 0:
                p = jnp.exp2(s_scr[b % 2][:, k0:k0 + kn] - m_b).astype(BF16)
                pv = jnp.dot(p, v_ref[0, 0, k0:k0 + kn, :], preferred_element_type=F32)
                acc[b] = pv if acc[b] is None else acc[b] + pv
        if b >= 0:
            o = _attn_finish(acc[b]).reshape(ATTN_GROUP, ATTN_TQ, HEAD_DIM)
            o_ref[0, :, b * ATTN_TQ:(b + 1) * ATTN_TQ, :] = o.astype(o_ref.dtype)


def _attn_ctx_kernel(q_ref, k_ref, v_ref, o_ref):
    q = q_ref[0].reshape(ATTN_GROUP * CTX_LEN, HEAD_DIM)
    s = _dot_nt(q, k_ref[0, 0])
    p = jnp.exp2(s - jnp.max(s, axis=-1, keepdims=True)).astype(BF16)
    o = _attn_finish(jnp.dot(p, v_ref[0, 0], preferred_element_type=F32))
    o_ref[0] = o.reshape(ATTN_GROUP, CTX_LEN, HEAD_DIM).astype(o_ref.dtype)


def _attention_lat(qh, kh, vh):
    tq = ATTN_NQ * ATTN_TQ
    rows = ATTN_GROUP * ATTN_TQ
    return pl.pallas_call(
        _attn_lat_kernel,
        grid=(BATCH, ATTN_KV_HEADS, SEQ // tq),
        in_specs=[pl.BlockSpec((1, ATTN_GROUP, tq, HEAD_DIM), lambda b, g, i: (b, g, i, 0)),
                  pl.BlockSpec((1, 1, TT, HEAD_DIM), lambda b, g, i: (b, g, 0, 0)),
                  pl.BlockSpec((1, 1, TT, LANES), lambda b, g, i: (b, g, 0, 0))],
        out_specs=pl.BlockSpec((1, ATTN_GROUP, tq, HEAD_DIM), lambda b, g, i: (b, g, i, 0)),
        out_shape=jax.ShapeDtypeStruct((BATCH, ATTN_HEADS, SEQ, HEAD_DIM), BF16),
        scratch_shapes=[pltpu.VMEM((rows, TT), F32), pltpu.VMEM((rows, TT), F32)],
        compiler_params=_cparams("parallel", "parallel", "arbitrary"),
        name="attention_lat",
    )(qh, kh, vh)


def _attention_ctx(qh, kh, vh):
    ctx_blk = SEQ // CTX_LEN
    return pl.pallas_call(
        _attn_ctx_kernel,
        grid=(BATCH, ATTN_KV_HEADS),
        in_specs=[pl.BlockSpec((1, ATTN_GROUP, CTX_LEN, HEAD_DIM), lambda b, g: (b, g, ctx_blk, 0)),
                  pl.BlockSpec((1, 1, CTX_LEN, HEAD_DIM), lambda b, g: (b, g, ctx_blk, 0)),
                  pl.BlockSpec((1, 1, CTX_LEN, LANES), lambda b, g: (b, g, ctx_blk, 0))],
        out_specs=pl.BlockSpec((1, ATTN_GROUP, CTX_LEN, HEAD_DIM), lambda b, g: (b, g, 0, 0)),
        out_shape=jax.ShapeDtypeStruct((BATCH, ATTN_HEADS, CTX_LEN, HEAD_DIM), BF16),
        compiler_params=_cparams("parallel", "arbitrary"),
        name="attention_ctx",
    )(qh, kh, vh)


def _mix_kernel(x_ref, mod_ref, o_dn_ref, zgate_ref, attn_ref, zu_ref, zv_ref, dn_g_ref, sgu_g_ref, sgu_w_ref,
                sgu_b_ref, w_out_ref, g2_ref, *rest, moe):
    if moe:
        rw_ref, rb_ref, ox, oh, ologit = rest
    else:
        ox, oh = rest
    ones = _seg_ones()
    m = mod_ref[0]
    o = o_dn_ref[0]
    dn = o * lax.rsqrt(_seg_sum(o * o, ones) * (1.0 / HEAD_DIM) + NORM_EPS) * dn_g_ref[...]
    dn = dn * jax.nn.silu(zgate_ref[0].astype(F32))
    at = jnp.concatenate([attn_ref[0, h] for h in range(ATTN_HEADS)], axis=1)
    u = jax.nn.gelu(zu_ref[0].astype(F32))
    v = jax.nn.gelu(zv_ref[0].astype(F32))
    v = (v * lax.rsqrt(_seg_sum(v * v, ones) * (1.0 / HEAD_DIM) + NORM_EPS) * sgu_g_ref[...]).astype(BF16)
    gd = MLP_WIDTH // MLP_GROUPS
    chunks = []
    for ci in range(TM // MLP_CHUNK):
        vc = v[ci * MLP_CHUNK:(ci + 1) * MLP_CHUNK]
        mixed = jnp.concatenate(
            [jnp.dot(sgu_w_ref[g], vc[:, g * gd:(g + 1) * gd], preferred_element_type=F32)
             for g in range(MLP_GROUPS)], axis=1)
        chunks.append(mixed + sgu_b_ref[...])
    sg = u * jnp.concatenate(chunks, axis=0)
    mixed_all = jnp.concatenate([dn.astype(BF16), at, sg.astype(BF16)], axis=1)
    y = jnp.dot(mixed_all, w_out_ref[...], preferred_element_type=F32)
    x = x_ref[0] + m[2:3] * y
    ox[0] = x
    h = _ada_norm(x, g2_ref[...], m[3:4], m[4:5])
    oh[0] = h.astype(oh.dtype)
    if moe:
        ologit[0] = jnp.dot(h, rw_ref[...], precision=HI, preferred_element_type=F32) + rb_ref[...]


def _mix(x_all, mod_l, o_dn, zgate, attn, zu, zv2, dn_g, sgu_g, sgu_w, sgu_b, w_out, gain2, router=None):
    moe = router is not None
    nt = N_TILES - 1 if moe else N_TILES
    rows = nt * TM
    row = lambda b, i: (b, i, 0)
    orow = row
    const2 = lambda b, i: (0, 0)
    gd = MLP_WIDTH // MLP_GROUPS
    sgu_b_x = jnp.repeat(sgu_b.T, gd, axis=1)
    in_specs = [pl.BlockSpec((1, TM, D_MODEL), row),
                pl.BlockSpec((1, 6, D_MODEL), _mod_index),
                pl.BlockSpec((1, TM, DN_WIDTH), row),
                pl.BlockSpec((1, TM, DN_WIDTH), row),
                pl.BlockSpec((1, ATTN_HEADS, TM, HEAD_DIM), lambda b, i: (b, 0, i, 0)),
                pl.BlockSpec((1, TM, MLP_WIDTH), row),
                pl.BlockSpec((1, TM, MLP_WIDTH), row),
                pl.BlockSpec((1, DN_WIDTH), const2),
                pl.BlockSpec((1, MLP_WIDTH), const2),
                pl.BlockSpec((MLP_GROUPS, MLP_CHUNK, MLP_CHUNK), lambda b, i: (0, 0, 0)),
                pl.BlockSpec((MLP_CHUNK, MLP_WIDTH), const2),
                pl.BlockSpec((D_MIX, D_MODEL), const2),
                pl.BlockSpec((1, D_MODEL), const2)]
    args = [x_all, mod_l, o_dn, zgate, attn, zu, zv2, jnp.tile(dn_g.reshape(1, HEAD_DIM), (1, DN_HEADS)),
            sgu_g.reshape(1, MLP_WIDTH), sgu_w.astype(BF16), sgu_b_x, w_out, gain2.reshape(1, D_MODEL)]
    out_specs = [pl.BlockSpec((1, TM, D_MODEL), orow), pl.BlockSpec((1, TM, D_MODEL), orow)]
    out_shape = [jax.ShapeDtypeStruct((BATCH, rows, D_MODEL), F32),
                 jax.ShapeDtypeStruct((BATCH, rows, D_MODEL), F32 if moe else BF16)]
    if moe:
        rw, rb = router
        in_specs += [pl.BlockSpec((D_MODEL, LANES), const2), pl.BlockSpec((1, LANES), const2)]
        args += [jnp.pad(rw, ((0, 0), (0, LANES - MOE_EXPERTS))),
                 jnp.pad(rb.reshape(1, MOE_EXPERTS), ((0, 0), (0, LANES - MOE_EXPERTS)))]
        out_specs.append(pl.BlockSpec((1, TM, LANES), orow))
        out_shape.append(jax.ShapeDtypeStruct((BATCH, rows, LANES), F32))
    return pl.pallas_call(
        functools.partial(_mix_kernel, moe=moe),
        grid=(BATCH, nt),
        in_specs=in_specs, out_specs=out_specs, out_shape=out_shape,
        compiler_params=_cparams("parallel", "arbitrary"),
        name="mix_moe" if moe else "mix",
    )(*args)


FFN_SPLIT = 2
FFN_BLK = D_FF // FFN_SPLIT


def _ffn_kernel(x_ref, h_ref, mod_ref, w1_ref, w3_ref, w2_ref, o_ref):
    h = h_ref[0]
    y = None
    for c in range(FFN_SPLIT):
        sl = slice(c * FFN_BLK, (c + 1) * FFN_BLK)
        a = jnp.dot(h, w1_ref[:, sl], preferred_element_type=F32)
        b = jnp.dot(h, w3_ref[:, sl], preferred_element_type=F32)
        part = jnp.dot((jax.nn.silu(a) * b).astype(BF16), w2_ref[sl, :], preferred_element_type=F32)
        y = part if y is None else y + part
    o_ref[0] = x_ref[0] + mod_ref[0][5:6] * y


def _ffn(x_all, h_all, mod_l, w1, w3, w2):
    row = lambda b, i: (b, i, 0)
    const2 = lambda b, i: (0, 0)
    return pl.pallas_call(
        _ffn_kernel,
        grid=(BATCH, N_TILES),
        in_specs=[pl.BlockSpec((1, TM, D_MODEL), row),
                  pl.BlockSpec((1, TM, D_MODEL), row),
                  pl.BlockSpec((1, 6, D_MODEL), _mod_index),
                  pl.BlockSpec((D_MODEL, D_FF), const2),
                  pl.BlockSpec((D_MODEL, D_FF), const2),
                  pl.BlockSpec((D_FF, D_MODEL), const2)],
        out_specs=pl.BlockSpec((1, TM, D_MODEL), row),
        out_shape=jax.ShapeDtypeStruct((BATCH, TT, D_MODEL), F32),
        compiler_params=_cparams("parallel", "arbitrary"),
        name="ffn",
    )(x_all, h_all, mod_l, w1, w3, w2)


def _route_kernel(logit_ref, dest_ref, gate_ref, count_ref, run_scr):
    phase, t = pl.program_id(0), pl.program_id(1)

    @pl.when((phase == 0) & (t == 0))
    def _():
        run_scr[...] = jnp.zeros_like(run_scr)

    lane = lax.broadcasted_iota(jnp.int32, (ROUTE_TILE, LANES), 1).astype(F32)
    logits = jnp.where(lane < MOE_EXPERTS, logit_ref[...], -jnp.inf)
    m1 = jnp.max(logits, axis=-1, keepdims=True)
    e1 = jnp.min(jnp.where(logits == m1, lane, float(LANES)), axis=-1, keepdims=True)
    rest = jnp.where(lane == e1, -jnp.inf, logits)
    m2 = jnp.max(rest, axis=-1, keepdims=True)
    e2 = jnp.min(jnp.where(rest == m2, lane, float(LANES)), axis=-1, keepdims=True)
    hot1 = (lane == e1).astype(F32)
    hot2 = (lane == e2).astype(F32)
    hot = hot1 + hot2
    tile_count = jnp.sum(hot, axis=0, keepdims=True)

    @pl.when(phase == 0)
    def _():
        run_scr[0:1] = run_scr[0:1] + tile_count

    @pl.when((phase == 1) & (t == 0))
    def _():
        counts = jnp.broadcast_to(run_scr[0:1], (8, LANES))
        count_ref[...] = counts.astype(jnp.int32)
        padded = jnp.ceil(counts * (1.0 / MOE_BLOCK)) * MOE_BLOCK
        ei = lax.broadcasted_iota(jnp.int32, (LANES, LANES), 0)
        ej = lax.broadcasted_iota(jnp.int32, (LANES, LANES), 1)
        before = (ei < ej).astype(F32)
        run_scr[1:2] = jnp.dot(padded, before, precision=HI, preferred_element_type=F32)[0:1]

    @pl.when(phase == 1)
    def _():
        ri = lax.broadcasted_iota(jnp.int32, (ROUTE_TILE, ROUTE_TILE), 0)
        rj = lax.broadcasted_iota(jnp.int32, (ROUTE_TILE, ROUTE_TILE), 1)
        earlier = (ri > rj).astype(BF16)
        within = jnp.dot(earlier, hot.astype(BF16), preferred_element_type=F32)
        pos = within + run_scr[1:2]
        d1 = jnp.sum(pos * hot1, axis=-1, keepdims=True)
        d2 = jnp.sum(pos * hot2, axis=-1, keepdims=True)
        dest_ref[...] = jnp.concatenate([d1, d2], axis=1).astype(jnp.int32)
        w2 = jnp.exp(m2 - m1)
        gate_ref[...] = jnp.concatenate([1.0 / (1.0 + w2), w2 / (1.0 + w2)], axis=1)
        run_scr[1:2] = run_scr[1:2] + tile_count


def _route(logits):
    nt = N_LAT // ROUTE_TILE
    return pl.pallas_call(
        _route_kernel,
        grid=(2, nt),
        in_specs=[pl.BlockSpec((ROUTE_TILE, LANES), lambda p, t: (t, 0))],
        out_specs=[pl.BlockSpec((ROUTE_TILE, MOE_TOP_K), lambda p, t: (t * p, 0)),
                   pl.BlockSpec((ROUTE_TILE, MOE_TOP_K), lambda p, t: (t * p, 0)),
                   pl.BlockSpec((8, LANES), lambda p, t: (0, 0))],
        out_shape=[jax.ShapeDtypeStruct((N_LAT, MOE_TOP_K), jnp.int32),
                   jax.ShapeDtypeStruct((N_LAT, MOE_TOP_K), F32),
                   jax.ShapeDtypeStruct((8, LANES), jnp.int32)],
        scratch_shapes=[pltpu.VMEM((8, LANES), F32)],
        compiler_params=_cparams("arbitrary", "arbitrary"),
        name="moe_route",
    )(logits)


N_PAD_SLOTS = MOE_ROWS - N_ASSIGN
Y2_ROWS = N_LAT + N_PAD_SLOTS // MOE_TOP_K


def _invert_kernel(dest_ref, bounds_ref, slot_ref):
    def real(a, c):
        slot_ref[dest_ref[a]] = a
        return c

    lax.fori_loop(0, N_ASSIGN, real, 0, unroll=8)

    def pad_range(e, count):
        def pad(s, cnt):
            slot_ref[s] = N_ASSIGN + cnt
            return cnt + 1

        return lax.fori_loop(bounds_ref[2 * e], bounds_ref[2 * e + 1], pad, count)

    lax.fori_loop(0, MOE_EXPERTS + 1, pad_range, 0)


def _invert(dest_flat, pad_bounds):
    return pl.pallas_call(
        _invert_kernel,
        grid_spec=pltpu.PrefetchScalarGridSpec(
            num_scalar_prefetch=2, grid=(1,), in_specs=[],
            out_specs=pl.BlockSpec(memory_space=pltpu.SMEM)),
        out_shape=jax.ShapeDtypeStruct((MOE_ROWS,), jnp.int32),
        compiler_params=_cparams("arbitrary"),
        name="moe_invert",
    )(dest_flat, pad_bounds)


LAST_BLOCK = MOE_N_BLOCKS - 1
MOE_NBUF = 3


def _swiglu_half(x, w1_ref, w3_ref, w2_ref):
    a = jnp.dot(x, w1_ref[0], preferred_element_type=F32)
    b = jnp.dot(x, w3_ref[0], preferred_element_type=F32)
    return jnp.dot((jax.nn.silu(a) * b).astype(BF16), w2_ref[0], preferred_element_type=F32)


def _expert_gather_kernel(be_ref, slot_ref, h_hbm, w1_ref, w3_ref, w2_ref, xs_ref, yb_ref, xbuf, sem):
    del be_ref
    j = pl.program_id(0)
    cur = j % MOE_NBUF

    def gather(blk, buf):
        base = blk * MOE_BLOCK
        for r in range(MOE_BLOCK):
            row = jnp.minimum(lax.shift_right_logical(slot_ref[base + r], 1), N_LAT - 1)
            pltpu.make_async_copy(h_hbm.at[pl.ds(row, 1)], xbuf.at[buf, pl.ds(r, 1)], sem.at[buf]).start()

    def wait(buf):
        pltpu.make_async_copy(h_hbm.at[pl.ds(0, MOE_BLOCK)], xbuf.at[buf], sem.at[buf]).wait()

    @pl.when(j == 0)
    def _():
        gather(0, 0)
        gather(1, 1)

    wait(cur)
    xs_ref[...] = xbuf[cur]
    gather(jnp.minimum(j + 2, LAST_BLOCK), (j + 2) % MOE_NBUF)
    yb_ref[...] = _swiglu_half(xs_ref[...].astype(BF16), w1_ref, w3_ref, w2_ref)

    @pl.when(j == LAST_BLOCK)
    def _():
        wait((j + 1) % MOE_NBUF)
        wait((j + 2) % MOE_NBUF)


def _expert_scatter_kernel(be_ref, slot_ref, xs_ref, w1_ref, w3_ref, w2_ref, yb_ref, y2_hbm, obuf, sem):
    del be_ref
    j = pl.program_id(0)
    cur = j % MOE_NBUF
    prev = (j + MOE_NBUF - 1) % MOE_NBUF

    def scatter(blk, buf):
        base = blk * MOE_BLOCK
        for r in range(MOE_BLOCK):
            a = slot_ref[base + r]
            pltpu.make_async_copy(obuf.at[buf, pl.ds(r, 1)],
                                  y2_hbm.at[a & 1, pl.ds(lax.shift_right_logical(a, 1), 1)], sem.at[buf]).start()

    def wait(buf):
        pltpu.make_async_copy(obuf.at[buf], y2_hbm.at[0, pl.ds(0, MOE_BLOCK)], sem.at[buf]).wait()

    def compute():
        obuf[cur] = yb_ref[...] + _swiglu_half(xs_ref[...].astype(BF16), w1_ref, w3_ref, w2_ref)

    @pl.when(j >= MOE_NBUF)
    def _():
        wait(cur)

    @pl.when(j == 0)
    def _():
        compute()

    @pl.when(j > 0)
    def _():
        scatter(j - 1, prev)
        compute()

    @pl.when(j == LAST_BLOCK)
    def _():
        scatter(j, cur)
        for b in range(MOE_NBUF):
            wait(b)


def _experts(block_expert, slot_src, h_lat, w1, w3, w2):
    def w_specs(c):
        return [pl.BlockSpec((1, D_MODEL, MOE_FF_BLK), lambda j, be, sl: (be[j], 0, c)),
                pl.BlockSpec((1, D_MODEL, MOE_FF_BLK), lambda j, be, sl: (be[j], 0, c)),
                pl.BlockSpec((1, MOE_FF_BLK, D_MODEL), lambda j, be, sl: (be[j], c, 0))]

    blk = pl.BlockSpec((MOE_BLOCK, D_MODEL), lambda j, be, sl: (j, 0))
    xs, yb = pl.pallas_call(
        _expert_gather_kernel,
        grid_spec=pltpu.PrefetchScalarGridSpec(
            num_scalar_prefetch=2, grid=(MOE_N_BLOCKS,),
            in_specs=[pl.BlockSpec(memory_space=pl.ANY)] + w_specs(0),
            out_specs=[blk, blk],
            scratch_shapes=[pltpu.VMEM((MOE_NBUF, MOE_BLOCK, D_MODEL), F32), pltpu.SemaphoreType.DMA((MOE_NBUF,))]),
        out_shape=[jax.ShapeDtypeStruct((MOE_ROWS, D_MODEL), F32)] * 2,
        compiler_params=_cparams("arbitrary"),
        name="moe_experts_gather",
    )(block_expert, slot_src, h_lat, w1, w3, w2)
    return pl.pallas_call(
        _expert_scatter_kernel,
        grid_spec=pltpu.PrefetchScalarGridSpec(
            num_scalar_prefetch=2, grid=(MOE_N_BLOCKS,),
            in_specs=[blk] + w_specs(1) + [blk],
            out_specs=pl.BlockSpec(memory_space=pl.ANY),
            scratch_shapes=[pltpu.VMEM((MOE_NBUF, MOE_BLOCK, D_MODEL), F32), pltpu.SemaphoreType.DMA((MOE_NBUF,))]),
        out_shape=jax.ShapeDtypeStruct((MOE_TOP_K, Y2_ROWS, D_MODEL), F32),
        compiler_params=_cparams("arbitrary"),
        name="moe_experts_scatter",
    )(block_expert, slot_src, xs, w1, w3, w2, yb)


def _combine_kernel(x_ref, gate_ref, mod_ref, g_ref, y2_ref, o_ref):
    gates = gate_ref[0]
    y = y2_ref[0] * gates[:, 0:1] + y2_ref[1] * gates[:, 1:2]
    x = x_ref[0] + mod_ref[0][5:6] * y
    o_ref[0] = x * lax.rsqrt(jnp.mean(x * x, axis=-1, keepdims=True) + NORM_EPS) * g_ref[...]


def _combine(x_lat, gates, mod_l, final_g, y2):
    nt = SEQ // TM
    return pl.pallas_call(
        _combine_kernel,
        grid=(BATCH, nt),
        in_specs=[pl.BlockSpec((1, TM, D_MODEL), lambda b, i: (b, i, 0)),
                  pl.BlockSpec((1, TM, MOE_TOP_K), lambda b, i: (b, i, 0)),
                  pl.BlockSpec((1, 6, D_MODEL), lambda b, i: (b, 0, 0)),
                  pl.BlockSpec((1, D_MODEL), lambda b, i: (0, 0)),
                  pl.BlockSpec((MOE_TOP_K, TM, D_MODEL), lambda b, i: (0, b * nt + i, 0))],
        out_specs=pl.BlockSpec((1, TM, D_MODEL), lambda b, i: (b, i, 0)),
        out_shape=jax.ShapeDtypeStruct((BATCH, SEQ, D_MODEL), F32),
        compiler_params=_cparams("parallel", "arbitrary"),
        name="moe_combine",
    )(x_lat, gates.reshape(BATCH, SEQ, MOE_TOP_K), mod_l, final_g.reshape(1, D_MODEL), y2)


def _reorder_w_in(w):
    s = np.cumsum((3 * DN_WIDTH, DN_WIDTH, 2 * DN_HEADS, 2 * DN_HEADS, ATTN_WIDTH, ATTN_KV_WIDTH, ATTN_KV_WIDTH,
                   MLP_WIDTH, MLP_WIDTH)).tolist()
    ba = w[:, s[1]:s[3]]
    return jnp.concatenate([w[:, :s[1]], w[:, s[3]:], ba,
                            jnp.zeros((D_MODEL, LANES - 4 * DN_HEADS), w.dtype)], axis=1).astype(BF16)


def kernel(x, c, ctx, c_ctx, mod_w, mod_b, norm1_g, norm2_g, w_in, conv_w, dn_a_log, dn_dt_bias, dn_norm_g,
           q_norm_g, k_norm_g, sgu_norm_g, sgu_w, sgu_b, w_out, ffn_w1, ffn_w3, ffn_w2, router_w, router_b,
           moe_w1, moe_w3, moe_w2, final_norm_g):
    assert DEPTH == 2 and x.shape == (BATCH, SEQ, D_MODEL) and ctx.shape == (BATCH, CTX_LEN, D_MODEL)
    cond8 = jnp.concatenate([c, c_ctx[None], jnp.zeros((8 - BATCH - 1, D_MODEL), F32)], axis=0)
    mod = _modulation(cond8, mod_w, mod_b)
    rope_c, rope_s = _rope_tables()
    x_all = jnp.concatenate([x, ctx], axis=1)
    for layer in range(DEPTH):
        last = layer == DEPTH - 1
        zqkv, zgate, zq, zk, zv, zu, zv2, zba = _in_proj(x_all, mod[layer], norm1_g[layer],
                                                         _reorder_w_in(w_in[layer]))
        qkv = _dn_prep(zqkv, conv_w[layer])
        o_dn = _deltanet(qkv, zba, dn_a_log[layer], dn_dt_bias[layer])
        qh, kh, vh = _attn_prep(zq, zk, zv, q_norm_g[layer], k_norm_g[layer], rope_c, rope_s)
        attn = _attention_lat(qh, kh, vh)
        if not last:
            attn = jnp.concatenate([attn, _attention_ctx(qh, kh, vh)], axis=2)
        mix_args = (x_all, mod[layer], o_dn, zgate, attn, zu, zv2, dn_norm_g[layer], sgu_norm_g[layer],
                    sgu_w[layer], sgu_b[layer], w_out[layer].astype(BF16), norm2_g[layer])
        if not last:
            i = layer // 2
            x_mid, h_mid = _mix(*mix_args)
            x_all = _ffn(x_mid, h_mid, mod[layer], ffn_w1[i].astype(BF16), ffn_w3[i].astype(BF16),
                         ffn_w2[i].astype(BF16))
        else:
            i = layer // 2
            x_lat, h_lat, logits = _mix(*mix_args, router=(router_w[i], router_b[i]))
            dest, gates, counts = _route(logits.reshape(N_LAT, LANES))
            cnt = counts[0, :MOE_EXPERTS]
            padded = (cnt + MOE_BLOCK - 1) // MOE_BLOCK * MOE_BLOCK
            pad_ends = jnp.cumsum(padded)
            block_expert = jnp.minimum(
                jnp.sum(pad_ends[None, :] <= (jnp.arange(MOE_N_BLOCKS) * MOE_BLOCK)[:, None], axis=1),
                MOE_EXPERTS - 1).astype(jnp.int32)
            lo = jnp.concatenate([pad_ends - padded + cnt, pad_ends[-1:]])
            hi = jnp.concatenate([pad_ends, jnp.full((1,), MOE_ROWS, pad_ends.dtype)])
            pad_bounds = jnp.stack([lo, hi], axis=1).reshape(-1).astype(jnp.int32)
            slot_src = _invert(dest.reshape(N_ASSIGN), pad_bounds)
            y2 = _experts(block_expert, slot_src, h_lat.reshape(N_LAT, D_MODEL), moe_w1[i].astype(BF16),
                          moe_w3[i].astype(BF16), moe_w2[i].astype(BF16))
            return _combine(x_lat, gates, mod[layer], final_norm_g, y2)
```

```python
import functools

import jax
import jax.numpy as jnp
import numpy as np
from jax import lax
from jax.experimental import pallas as pl
from jax.experimental.pallas import tpu as pltpu

D_MODEL = 1024
BATCH = 4
SEQ = 4096
DEPTH = 2
GRID_W = 64
CTX_LEN = 256
HEAD_DIM = 64
DN_HEADS = 6
ATTN_HEADS = 6
ATTN_KV_HEADS = 2
ATTN_GROUP = ATTN_HEADS // ATTN_KV_HEADS
MLP_GROUPS = 4
DN_WIDTH = DN_HEADS * HEAD_DIM
ATTN_WIDTH = ATTN_HEADS * HEAD_DIM
ATTN_KV_WIDTH = ATTN_KV_HEADS * HEAD_DIM
MLP_WIDTH = MLP_GROUPS * HEAD_DIM
D_MIX = DN_WIDTH + ATTN_WIDTH + MLP_WIDTH
CONV_K = 3
DN_CHUNK = 64
ATTN_SCALE = HEAD_DIM ** -0.5
LOG2E = 1.4426950408889634
MLP_CHUNK = 128
ROPE_THETA = 10000.0
ROPE_AXIS_DIM = HEAD_DIM // 2
ROPE_PAIRS = ROPE_AXIS_DIM // 2
D_FF = 2816
MOE_EXPERTS = 8
MOE_TOP_K = 2
MOE_D_FF = 3584
MOE_BLOCK = 256
NORM_EPS = 1e-6

LANES = 128
TT = SEQ + CTX_LEN
TM = 256
N_TILES = TT // TM
CTX_TILE = N_TILES - 1
CTX_ROW = BATCH
N_DN_STEPS = TT // DN_CHUNK
N_CTX_CHUNKS = CTX_LEN // DN_CHUNK
N_LAT_CHUNKS = SEQ // DN_CHUNK
IN_PAD = 3 * DN_WIDTH + DN_WIDTH + ATTN_WIDTH + 2 * ATTN_KV_WIDTH + 2 * MLP_WIDTH + LANES
N_LAT = BATCH * SEQ
N_ASSIGN = N_LAT * MOE_TOP_K
MOE_N_BLOCKS = -(-(N_ASSIGN + MOE_EXPERTS * (MOE_BLOCK - 1)) // MOE_BLOCK)
MOE_ROWS = MOE_N_BLOCKS * MOE_BLOCK
MOE_FF_SPLIT = 2
MOE_FF_BLK = MOE_D_FF // MOE_FF_SPLIT
ROUTE_TILE = 512
VMEM_LIMIT = 56 * 2 ** 20

F32 = jnp.float32
BF16 = jnp.bfloat16
HI = lax.Precision.HIGHEST


def _cparams(*sem):
    return pltpu.CompilerParams(dimension_semantics=sem, vmem_limit_bytes=VMEM_LIMIT)


def _bdot(a, b):
    return jnp.dot(a.astype(BF16), b.astype(BF16), preferred_element_type=F32)


def _bdot_nt(a, b):
    return lax.dot_general(a.astype(BF16), b.astype(BF16), (((1,), (1,)), ((), ())), preferred_element_type=F32)


def _bdot_tn(a, b):
    return lax.dot_general(a.astype(BF16), b.astype(BF16), (((0,), (0,)), ((), ())), preferred_element_type=F32)


def _seg_ones():
    r = lax.shift_right_logical(lax.broadcasted_iota(jnp.int32, (LANES, LANES), 0), 6)
    c = lax.shift_right_logical(lax.broadcasted_iota(jnp.int32, (LANES, LANES), 1), 6)
    return (r == c).astype(F32).astype(BF16)


def _split_bf16(x):
    hi = x.astype(BF16)
    return hi, (x - hi.astype(F32)).astype(BF16)


def _seg_sum(y, ones):
    parts = []
    for i in range(0, y.shape[-1], LANES):
        hi, lo = _split_bf16(y[:, i:i + LANES])
        parts.append(jnp.dot(hi, ones, preferred_element_type=F32) + jnp.dot(lo, ones, preferred_element_type=F32))
    return parts[0] if len(parts) == 1 else jnp.concatenate(parts, axis=-1)


def _softplus(x):
    return jnp.maximum(x, 0.0) + jnp.log1p(jnp.exp(-jnp.abs(x)))


def _ada_norm(x, gain, shift, scale):
    y = x * lax.rsqrt(jnp.mean(x * x, axis=-1, keepdims=True) + NORM_EPS) * gain
    return y * (1.0 + scale) + shift


def _mod_index(b, i):
    return (jnp.where(i == CTX_TILE, CTX_ROW, b), 0, 0)


def _mod_kernel(c_ref, w_ref, b_ref, o_ref):
    cond = jax.nn.silu(c_ref[...])
    o_ref[0] = jnp.dot(cond, w_ref[0], precision=HI, preferred_element_type=F32) + b_ref[0]


def _modulation(cond8, mod_w, mod_b):
    nblk = 4
    bn = 6 * D_MODEL // nblk
    out = pl.pallas_call(
        _mod_kernel,
        grid=(DEPTH, nblk),
        in_specs=[pl.BlockSpec((8, D_MODEL), lambda l, j: (0, 0)),
                  pl.BlockSpec((1, D_MODEL, bn), lambda l, j: (l, 0, j)),
                  pl.BlockSpec((1, 1, bn), lambda l, j: (l, 0, j))],
        out_specs=pl.BlockSpec((1, 8, bn), lambda l, j: (l, 0, j)),
        out_shape=jax.ShapeDtypeStruct((DEPTH, 8, 6 * D_MODEL), F32),
        compiler_params=_cparams("arbitrary", "arbitrary"),
        name="modulation",
    )(cond8, mod_w, mod_b.reshape(DEPTH, 1, 6 * D_MODEL))
    return out.reshape(DEPTH, 8, 6, D_MODEL)


_IN_SPLITS = (3 * DN_WIDTH, DN_WIDTH, ATTN_WIDTH, ATTN_KV_WIDTH, ATTN_KV_WIDTH, MLP_WIDTH, MLP_WIDTH, LANES)


def _in_proj_kernel(x_ref, mod_ref, g_ref, w_ref, oqkv, ogate, oq, ok, ov, ou, ov2, oba):
    m = mod_ref[0]
    h = _ada_norm(x_ref[0], g_ref[...], m[0:1], m[1:2]).astype(BF16)
    z = jnp.dot(h, w_ref[...], preferred_element_type=F32)
    off = 0
    for ref, width in zip((oqkv, ogate, oq, ok, ov, ou, ov2, oba), _IN_SPLITS):
        ref[0] = z[:, off:off + width].astype(ref.dtype)
        off += width


def _in_proj(x_all, mod_l, gain, w_in_r):
    dts = (BF16,) * 7 + (F32,)
    row = lambda b, i: (b, i, 0)
    return pl.pallas_call(
        _in_proj_kernel,
        grid=(BATCH, N_TILES),
        in_specs=[pl.BlockSpec((1, TM, D_MODEL), row),
                  pl.BlockSpec((1, 6, D_MODEL), _mod_index),
                  pl.BlockSpec((1, D_MODEL), lambda b, i: (0, 0)),
                  pl.BlockSpec((D_MODEL, IN_PAD), lambda b, i: (0, 0))],
        out_specs=[pl.BlockSpec((1, TM, w), row) for w in _IN_SPLITS],
        out_shape=[jax.ShapeDtypeStruct((BATCH, TT, w), dt) for w, dt in zip(_IN_SPLITS, dts)],
        compiler_params=_cparams("parallel", "arbitrary"),
        name="in_proj",
    )(x_all, mod_l, gain.reshape(1, D_MODEL), w_in_r)


def _dn_prep_kernel(z_ref, w_ref, o_ref):
    j = pl.program_id(1)
    z = z_ref[0].astype(F32)
    w = w_ref[...]
    row = lax.broadcasted_iota(jnp.int32, (TT, 1), 0)
    first = (row == 0) | (row == SEQ)
    last = (row == SEQ - 1) | (row == TT - 1)
    zp = jnp.where(first, 0.0, pltpu.roll(z, 1, 0))
    zn = jnp.where(last, 0.0, pltpu.roll(z, TT - 1, 0))
    y = jax.nn.silu(w[0:1] * zp + w[1:2] * z + w[2:3] * zn)
    n_qk = 2 * DN_WIDTH // LANES
    n_q = DN_WIDTH // LANES

    @pl.when(j < n_qk)
    def _():
        inv = lax.rsqrt(_seg_sum(y * y, _seg_ones()) + NORM_EPS)
        o_ref[0] = (y * inv * jnp.where(j < n_q, HEAD_DIM ** -0.5, 1.0)).astype(o_ref.dtype)

    @pl.when(j >= n_qk)
    def _():
        o_ref[0] = y.astype(o_ref.dtype)


def _dn_prep(zqkv, conv_w):
    nb = 3 * DN_WIDTH // LANES
    return pl.pallas_call(
        _dn_prep_kernel,
        grid=(BATCH, nb),
        in_specs=[pl.BlockSpec((1, TT, LANES), lambda b, j: (b, 0, j)),
                  pl.BlockSpec((CONV_K, LANES), lambda b, j: (0, j))],
        out_specs=pl.BlockSpec((1, TT, LANES), lambda b, j: (b, 0, j)),
        out_shape=jax.ShapeDtypeStruct((BATCH, TT, 3 * DN_WIDTH), BF16),
        compiler_params=_cparams("parallel", "arbitrary"),
        name="dn_prep",
    )(zqkv, conv_w)


DN_PAIRS = DN_HEADS // 2
DN_SUB = 4
LOG2_CHUNK = DN_CHUNK.bit_length() - 1


def _dn_kernel(qkv_ref, zba_ref, a_ref, dt_ref, o_ref, s_scr):
    C, P2 = DN_CHUNK, 2 * DN_CHUNK
    o_ref[...] = jnp.zeros_like(o_ref)
    s_scr[...] = jnp.zeros_like(s_scr)
    neg_decay_rate = -jnp.exp(a_ref[...])
    dt_bias = dt_ref[...]
    lo = lax.broadcasted_iota(jnp.int32, (1, P2), 1) < C

    def stack(a):
        return jnp.concatenate([jnp.where(lo, a, 0.0), jnp.where(lo, 0.0, a)], axis=0)

    def step(i, carry):
        ri = lax.broadcasted_iota(jnp.int32, (P2, P2), 0)
        ci = lax.broadcasted_iota(jnp.int32, (P2, P2), 1)
        ti = lax.broadcasted_iota(jnp.int32, (C, C), 0)
        tj = lax.broadcasted_iota(jnp.int32, (C, C), 1)
        same = lambda sh: lax.shift_right_logical(ri, sh) == lax.shift_right_logical(ci, sh)
        same_head = same(LOG2_CHUNK)
        eye = (ri == ci).astype(F32)

        ch = []
        for sub in range(DN_SUB):
            s = i * DN_SUB + sub
            chunk_of = (jnp.where(s < N_CTX_CHUNKS, N_LAT_CHUNKS + s, s - N_CTX_CHUNKS), N_DN_STEPS - 1 - s)
            for d in range(2):
                r0 = pl.multiple_of(chunk_of[d] * C, C)
                zba = zba_ref[0, pl.ds(r0, C), :]
                beta_all = jax.nn.sigmoid(zba)
                g_all = neg_decay_rate * _softplus(zba + dt_bias)
                tri = (ti >= tj) if d == 0 else (ti <= tj)
                gc_all = jnp.dot(tri.astype(F32), g_all, precision=HI, preferred_element_type=F32)
                gc_t = jnp.concatenate([gc_all, gc_all], axis=0).T
                last = C - 1 if d == 0 else 0
                for p in range(DN_PAIRS):
                    lb = (d * DN_HEADS + 2 * p, d * DN_HEADS + 2 * p + 1)
                    la = (2 * DN_HEADS + lb[0], 2 * DN_HEADS + lb[1])
                    col = lambda t, l: t[:, l:l + 1]
                    q2 = qkv_ref[0, pl.ds(r0, C), p * LANES:(p + 1) * LANES].astype(F32)
                    k2 = qkv_ref[0, pl.ds(r0, C), DN_WIDTH + p * LANES:DN_WIDTH + (p + 1) * LANES].astype(F32)
                    v2 = qkv_ref[0, pl.ds(r0, C), 2 * DN_WIDTH + p * LANES:2 * DN_WIDTH + (p + 1) * LANES].astype(F32)
                    beta2 = jnp.where(lo, col(beta_all, lb[0]), col(beta_all, lb[1]))
                    gc2 = jnp.where(lo, col(gc_all, la[0]), col(gc_all, la[1]))
                    gcol = jnp.concatenate([col(gc_all, la[0]), col(gc_all, la[1])], axis=0)
                    grow = jnp.where(lo, gc_t[la[0]:la[0] + 1, :], gc_t[la[1]:la[1] + 1, :])
                    g_last = jnp.where(lo, gc_all[last:last + 1, la[0]:la[0] + 1],
                                       gc_all[last:last + 1, la[1]:la[1] + 1])
                    e_diff = jnp.exp(gcol - grow)
                    order = (ri >= ci) if d == 0 else (ri <= ci)
                    kb2 = k2 * beta2
                    e_gc = jnp.exp(gc2)
                    ch.append(dict(
                        r0=r0, sub=sub, d=d, p=p, g_last=g_last,
                        dec_incl=jnp.where(same_head & order, e_diff, 0.0),
                        dec_strict=jnp.where(same_head & order & (ri != ci), e_diff, 0.0),
                        kq=jnp.concatenate([stack(kb2), stack(q2)], axis=0).astype(BF16),
                        k=stack(k2).astype(BF16),
                        rhs=jnp.concatenate([stack(v2 * beta2), stack(kb2 * e_gc)], axis=1).astype(BF16),
                        kd=stack(k2 * jnp.exp(g_last - gc2)).astype(BF16),
                        qe=stack(q2 * e_gc)))
        for c in ch:
            kk_qk = lax.dot_general(c["kq"], c["k"], (((1,), (1,)), ((), ())), preferred_element_type=F32)
            c["a"] = kk_qk[:P2] * c["dec_strict"]
            c["attn"] = (kk_qk[P2:] * c["dec_incl"]).astype(BF16)
            c["t"] = eye - jnp.where(same(1), c["a"], 0.0)
        for lvl in range(1, LOG2_CHUNK):
            joins = same(lvl + 1) & ~same(lvl)
            for c in ch:
                c["m"] = _bdot(jnp.where(joins, c["a"], 0.0), c["t"])
            for c in ch:
                c["t"] = c["t"] - _bdot(c["t"], c["m"])
        for c in ch:
            c["uw"] = _bdot(c["t"], c["rhs"]).astype(BF16)
        for c in ch:
            oa = jnp.dot(c["attn"], c["uw"], preferred_element_type=F32)
            c["o0"] = oa[:, :LANES]
            c["qp"] = c["qe"] - oa[:, LANES:]
            c["np"] = lax.dot_general(c["kd"], c["uw"], (((0,), (0,)), ((), ())), preferred_element_type=F32)

        states = [s_scr[j] for j in range(2 * DN_PAIRS)]
        for sub in range(DN_SUB):
            cur = [c for c in ch if c["sub"] == sub]
            for c in cur:
                j = c["d"] * DN_PAIRS + c["p"]
                c["r"] = _bdot(jnp.concatenate([c["qp"], c["np"][:, LANES:]], axis=0), states[j])
            for d in range(2):
                outs = []
                for c in cur:
                    if c["d"] != d:
                        continue
                    j = d * DN_PAIRS + c["p"]
                    o_st = c["o0"] + c["r"][:P2]
                    outs.append(o_st[:C] + o_st[C:])
                    states[j] = states[j] * jnp.exp(c["g_last"]) + c["np"][:, :LANES] - c["r"][P2:]
                    r0 = c["r0"]
                o_ref[0, pl.ds(r0, C), :] += jnp.concatenate(outs, axis=1)
        for j in range(2 * DN_PAIRS):
            s_scr[j] = states[j]
        return carry

    lax.fori_loop(0, N_DN_STEPS // DN_SUB, step, 0)


def _deltanet(qkv, zba, a_log, dt_bias):
    pad = lambda t: jnp.zeros((1, LANES), F32).at[0, 2 * DN_HEADS:4 * DN_HEADS].set(t.reshape(-1))
    return pl.pallas_call(
        _dn_kernel,
        grid=(BATCH,),
        in_specs=[pl.BlockSpec((1, TT, 3 * DN_WIDTH), lambda b: (b, 0, 0)),
                  pl.BlockSpec((1, TT, LANES), lambda b: (b, 0, 0)),
                  pl.BlockSpec((1, LANES), lambda b: (0, 0)),
                  pl.BlockSpec((1, LANES), lambda b: (0, 0))],
        out_specs=pl.BlockSpec((1, TT, DN_WIDTH), lambda b: (b, 0, 0)),
        out_shape=jax.ShapeDtypeStruct((BATCH, TT, DN_WIDTH), F32),
        scratch_shapes=[pltpu.VMEM((2 * DN_PAIRS, LANES, LANES), F32)],
        compiler_params=_cparams("parallel"),
        name="deltanet",
    )(qkv, zba, pad(a_log), pad(dt_bias))


def _rope_tables():
    rows = SEQ // GRID_W
    row = jnp.repeat(jnp.arange(rows, dtype=F32), GRID_W)
    col = jnp.tile(jnp.arange(GRID_W, dtype=F32), rows)
    inv = ROPE_THETA ** (-2.0 * jnp.arange(ROPE_PAIRS, dtype=F32) / ROPE_AXIS_DIM)
    ang = jnp.stack([row[:, None] * inv, col[:, None] * inv], axis=1)
    cos, sin = jnp.cos(ang), jnp.sin(ang)
    c = jnp.concatenate([cos[:, 0], cos[:, 0], cos[:, 1], cos[:, 1]], axis=-1)
    s = jnp.concatenate([-sin[:, 0], sin[:, 0], -sin[:, 1], sin[:, 1]], axis=-1)
    c = jnp.concatenate([c, jnp.ones((CTX_LEN, HEAD_DIM), F32)], axis=0)
    s = jnp.concatenate([s, jnp.zeros((CTX_LEN, HEAD_DIM), F32)], axis=0)
    return jnp.tile(c, (1, 2)), jnp.tile(s, (1, 2))


def _attn_prep_kernel(zq_ref, zk_ref, zv_ref, gq_ref, gk_ref, c_ref, s_ref, oq, ok, ov):
    ones = _seg_ones()
    cs, sn = c_ref[...], s_ref[...]
    lane = lax.broadcasted_iota(jnp.int32, (1, LANES), 1)
    first_half = (lane & (2 * ROPE_PAIRS - 1)) < ROPE_PAIRS

    def norm_rope(x, gain):
        y = x * lax.rsqrt(_seg_sum(x * x, ones) * (1.0 / HEAD_DIM) + NORM_EPS) * gain
        partner = jnp.where(first_half, pltpu.roll(y, LANES - ROPE_PAIRS, 1), pltpu.roll(y, ROPE_PAIRS, 1))
        return y * cs + partner * sn

    zq = zq_ref[0].astype(F32)
    for pair in range(ATTN_HEADS // 2):
        q2 = norm_rope(zq[:, pair * LANES:(pair + 1) * LANES], gq_ref[...]) * (ATTN_SCALE * LOG2E)
        oq[0, 2 * pair] = q2[:, :HEAD_DIM].astype(oq.dtype)
        oq[0, 2 * pair + 1] = q2[:, HEAD_DIM:].astype(oq.dtype)
    k2 = norm_rope(zk_ref[0].astype(F32), gk_ref[...])
    v2 = zv_ref[0]
    one_cols = jnp.ones((TM, HEAD_DIM), ov.dtype)
    for h in range(ATTN_KV_HEADS):
        ok[0, h] = k2[:, h * HEAD_DIM:(h + 1) * HEAD_DIM].astype(ok.dtype)
        ov[0, h] = jnp.concatenate([v2[:, h * HEAD_DIM:(h + 1) * HEAD_DIM], one_cols], axis=1)


def _attn_prep(zq, zk, zv, q_gain, k_gain, rope_c, rope_s):
    row = lambda b, i: (b, i, 0)
    hrow = lambda b, i: (b, 0, i, 0)
    tile2 = lambda g: jnp.tile(g.reshape(1, HEAD_DIM), (1, 2))
    return pl.pallas_call(
        _attn_prep_kernel,
        grid=(BATCH, N_TILES),
        in_specs=[pl.BlockSpec((1, TM, ATTN_WIDTH), row),
                  pl.BlockSpec((1, TM, ATTN_KV_WIDTH), row),
                  pl.BlockSpec((1, TM, ATTN_KV_WIDTH), row),
                  pl.BlockSpec((1, LANES), lambda b, i: (0, 0)),
                  pl.BlockSpec((1, LANES), lambda b, i: (0, 0)),
                  pl.BlockSpec((TM, LANES), lambda b, i: (i, 0)),
                  pl.BlockSpec((TM, LANES), lambda b, i: (i, 0))],
        out_specs=[pl.BlockSpec((1, ATTN_HEADS, TM, HEAD_DIM), hrow),
                   pl.BlockSpec((1, ATTN_KV_HEADS, TM, HEAD_DIM), hrow),
                   pl.BlockSpec((1, ATTN_KV_HEADS, TM, LANES), hrow)],
        out_shape=[jax.ShapeDtypeStruct((BATCH, ATTN_HEADS, TT, HEAD_DIM), BF16),
                   jax.ShapeDtypeStruct((BATCH, ATTN_KV_HEADS, TT, HEAD_DIM), BF16),
                   jax.ShapeDtypeStruct((BATCH, ATTN_KV_HEADS, TT, LANES), BF16)],
        compiler_params=_cparams("parallel", "arbitrary"),
        name="attn_prep",
    )(zq, zk, zv, tile2(q_gain), tile2(k_gain), rope_c, rope_s)


ATTN_TQ = 128
ATTN_NQ = 8
ATTN_KB = 512
ATTN_KEY_BLOCKS = tuple((j * ATTN_KB, ATTN_KB) for j in range(SEQ // ATTN_KB)) + ((SEQ, CTX_LEN),)


def _dot_nt(a, b):
    return lax.dot_general(a, b, (((1,), (1,)), ((), ())), preferred_element_type=F32)


def _attn_finish(acc):
    return acc[:, :HEAD_DIM] / acc[:, HEAD_DIM:HEAD_DIM + 1]


def _attn_lat_kernel(q_ref, k_ref, v_ref, o_ref, s0_scr, s1_scr):
    rows = ATTN_GROUP * ATTN_TQ
    s_scr = (s0_scr, s1_scr)
    mpart = [None] * ATTN_NQ
    acc = [None] * ATTN_NQ
    for stage in range(ATTN_NQ + 1):
        a, b = stage, stage - 1
        if a < ATTN_NQ:
            qa = q_ref[0, :, a * ATTN_TQ:(a + 1) * ATTN_TQ, :].reshape(rows, HEAD_DIM)
        if b >= 0:
            m_b = jnp.max(mpart[b], axis=-1, keepdims=True)
        for k0, kn in ATTN_KEY_BLOCKS:
            if a < ATTN_NQ:
                s = _dot_nt(qa, k_ref[0, 0, k0:k0 + kn, :])
                s_scr[a % 2][:, k0:k0 + kn] = s
                blk = functools.reduce(jnp.maximum, [s[:, i:i + LANES] for i in range(0, kn, LANES)])
                mpart[a] = blk if mpart[a] is None else jnp.maximum(mpart[a], blk)
            if b >= 0:
                p = jnp.exp2(s_scr[b % 2][:, k0:k0 + kn] - m_b).astype(BF16)
                pv = jnp.dot(p, v_ref[0, 0, k0:k0 + kn, :], preferred_element_type=F32)
                acc[b] = pv if acc[b] is None else acc[b] + pv
        if b >= 0:
            o = _attn_finish(acc[b]).reshape(ATTN_GROUP, ATTN_TQ, HEAD_DIM)
            o_ref[0, :, b * ATTN_TQ:(b + 1) * ATTN_TQ, :] = o.astype(o_ref.dtype)


def _attn_ctx_kernel(q_ref, k_ref, v_ref, o_ref):
    q = q_ref[0].reshape(ATTN_GROUP * CTX_LEN, HEAD_DIM)
    s = _dot_nt(q, k_ref[0, 0])
    p = jnp.exp2(s - jnp.max(s, axis=-1, keepdims=True)).astype(BF16)
    o = _attn_finish(jnp.dot(p, v_ref[0, 0], preferred_element_type=F32))
    o_ref[0] = o.reshape(ATTN_GROUP, CTX_LEN, HEAD_DIM).astype(o_ref.dtype)


def _attention_lat(qh, kh, vh):
    tq = ATTN_NQ * ATTN_TQ
    rows = ATTN_GROUP * ATTN_TQ
    return pl.pallas_call(
        _attn_lat_kernel,
        grid=(BATCH, ATTN_KV_HEADS, SEQ // tq),
        in_specs=[pl.BlockSpec((1, ATTN_GROUP, tq, HEAD_DIM), lambda b, g, i: (b, g, i, 0)),
                  pl.BlockSpec((1, 1, TT, HEAD_DIM), lambda b, g, i: (b, g, 0, 0)),
                  pl.BlockSpec((1, 1, TT, LANES), lambda b, g, i: (b, g, 0, 0))],
        out_specs=pl.BlockSpec((1, ATTN_GROUP, tq, HEAD_DIM), lambda b, g, i: (b, g, i, 0)),
        out_shape=jax.ShapeDtypeStruct((BATCH, ATTN_HEADS, SEQ, HEAD_DIM), BF16),
        scratch_shapes=[pltpu.VMEM((rows, TT), F32), pltpu.VMEM((rows, TT), F32)],
        compiler_params=_cparams("parallel", "parallel", "arbitrary"),
        name="attention_lat",
    )(qh, kh, vh)


def _attention_ctx(qh, kh, vh):
    ctx_blk = SEQ // CTX_LEN
    return pl.pallas_call(
        _attn_ctx_kernel,
        grid=(BATCH, ATTN_KV_HEADS),
        in_specs=[pl.BlockSpec((1, ATTN_GROUP, CTX_LEN, HEAD_DIM), lambda b, g: (b, g, ctx_blk, 0)),
                  pl.BlockSpec((1, 1, CTX_LEN, HEAD_DIM), lambda b, g: (b, g, ctx_blk, 0)),
                  pl.BlockSpec((1, 1, CTX_LEN, LANES), lambda b, g: (b, g, ctx_blk, 0))],
        out_specs=pl.BlockSpec((1, ATTN_GROUP, CTX_LEN, HEAD_DIM), lambda b, g: (b, g, 0, 0)),
        out_shape=jax.ShapeDtypeStruct((BATCH, ATTN_HEADS, CTX_LEN, HEAD_DIM), BF16),
        compiler_params=_cparams("parallel", "arbitrary"),
        name="attention_ctx",
    )(qh, kh, vh)


def _mix_kernel(x_ref, mod_ref, o_dn_ref, zgate_ref, attn_ref, zu_ref, zv_ref, dn_g_ref, sgu_g_ref, sgu_w_ref,
                sgu_b_ref, w_out_ref, g2_ref, *rest, moe):
    if moe:
        rw_ref, rb_ref, ox, oh, ologit = rest
    else:
        ox, oh = rest
    ones = _seg_ones()
    m = mod_ref[0]
    o = o_dn_ref[0]
    dn = o * lax.rsqrt(_seg_sum(o * o, ones) * (1.0 / HEAD_DIM) + NORM_EPS) * dn_g_ref[...]
    dn = dn * jax.nn.silu(zgate_ref[0].astype(F32))
    at = jnp.concatenate([attn_ref[0, h] for h in range(ATTN_HEADS)], axis=1)
    u = jax.nn.gelu(zu_ref[0].astype(F32))
    v = jax.nn.gelu(zv_ref[0].astype(F32))
    v = (v * lax.rsqrt(_seg_sum(v * v, ones) * (1.0 / HEAD_DIM) + NORM_EPS) * sgu_g_ref[...]).astype(BF16)
    gd = MLP_WIDTH // MLP_GROUPS
    chunks = []
    for ci in range(TM // MLP_CHUNK):
        vc = v[ci * MLP_CHUNK:(ci + 1) * MLP_CHUNK]
        mixed = jnp.concatenate(
            [jnp.dot(sgu_w_ref[g], vc[:, g * gd:(g + 1) * gd], preferred_element_type=F32)
             for g in range(MLP_GROUPS)], axis=1)
        chunks.append(mixed + sgu_b_ref[...])
    sg = u * jnp.concatenate(chunks, axis=0)
    mixed_all = jnp.concatenate([dn.astype(BF16), at, sg.astype(BF16)], axis=1)
    y = jnp.dot(mixed_all, w_out_ref[...], preferred_element_type=F32)
    x = x_ref[0] + m[2:3] * y
    ox[0] = x
    h = _ada_norm(x, g2_ref[...], m[3:4], m[4:5])
    oh[0] = h.astype(oh.dtype)
    if moe:
        h_hi, h_lo = _split_bf16(h)
        r_hi, r_lo = _split_bf16(rw_ref[...])
        ologit[0] = (jnp.dot(h_hi, r_hi, preferred_element_type=F32) + jnp.dot(h_hi, r_lo, preferred_element_type=F32)
                     + jnp.dot(h_lo, r_hi, preferred_element_type=F32)) + rb_ref[...]


def _mix(x_all, mod_l, o_dn, zgate, attn, zu, zv2, dn_g, sgu_g, sgu_w, sgu_b, w_out, gain2, router=None):
    moe = router is not None
    nt = N_TILES - 1 if moe else N_TILES
    rows = nt * TM
    row = lambda b, i: (b, i, 0)
    orow = row
    const2 = lambda b, i: (0, 0)
    gd = MLP_WIDTH // MLP_GROUPS
    sgu_b_x = jnp.repeat(sgu_b.T, gd, axis=1)
    in_specs = [pl.BlockSpec((1, TM, D_MODEL), row),
                pl.BlockSpec((1, 6, D_MODEL), _mod_index),
                pl.BlockSpec((1, TM, DN_WIDTH), row),
                pl.BlockSpec((1, TM, DN_WIDTH), row),
                pl.BlockSpec((1, ATTN_HEADS, TM, HEAD_DIM), lambda b, i: (b, 0, i, 0)),
                pl.BlockSpec((1, TM, MLP_WIDTH), row),
                pl.BlockSpec((1, TM, MLP_WIDTH), row),
                pl.BlockSpec((1, DN_WIDTH), const2),
                pl.BlockSpec((1, MLP_WIDTH), const2),
                pl.BlockSpec((MLP_GROUPS, MLP_CHUNK, MLP_CHUNK), lambda b, i: (0, 0, 0)),
                pl.BlockSpec((MLP_CHUNK, MLP_WIDTH), const2),
                pl.BlockSpec((D_MIX, D_MODEL), const2),
                pl.BlockSpec((1, D_MODEL), const2)]
    args = [x_all, mod_l, o_dn, zgate, attn, zu, zv2, jnp.tile(dn_g.reshape(1, HEAD_DIM), (1, DN_HEADS)),
            sgu_g.reshape(1, MLP_WIDTH), sgu_w.astype(BF16), sgu_b_x, w_out, gain2.reshape(1, D_MODEL)]
    out_specs = [pl.BlockSpec((1, TM, D_MODEL), orow), pl.BlockSpec((1, TM, D_MODEL), orow)]
    out_shape = [jax.ShapeDtypeStruct((BATCH, rows, D_MODEL), F32),
                 jax.ShapeDtypeStruct((BATCH, rows, D_MODEL), F32 if moe else BF16)]
    if moe:
        rw, rb = router
        in_specs += [pl.BlockSpec((D_MODEL, LANES), const2), pl.BlockSpec((1, LANES), const2)]
        args += [jnp.pad(rw, ((0, 0), (0, LANES - MOE_EXPERTS))),
                 jnp.pad(rb.reshape(1, MOE_EXPERTS), ((0, 0), (0, LANES - MOE_EXPERTS)))]
        out_specs.append(pl.BlockSpec((1, TM, LANES), orow))
        out_shape.append(jax.ShapeDtypeStruct((BATCH, rows, LANES), F32))
    return pl.pallas_call(
        functools.partial(_mix_kernel, moe=moe),
        grid=(BATCH, nt),
        in_specs=in_specs, out_specs=out_specs, out_shape=out_shape,
        compiler_params=_cparams("parallel", "arbitrary"),
        name="mix_moe" if moe else "mix",
    )(*args)


FFN_SPLIT = 2
FFN_BLK = D_FF // FFN_SPLIT


def _ffn_kernel(x_ref, h_ref, mod_ref, w1_ref, w3_ref, w2_ref, o_ref):
    h = h_ref[0]
    y = None
    for c in range(FFN_SPLIT):
        sl = slice(c * FFN_BLK, (c + 1) * FFN_BLK)
        a = jnp.dot(h, w1_ref[:, sl], preferred_element_type=F32)
        b = jnp.dot(h, w3_ref[:, sl], preferred_element_type=F32)
        part = jnp.dot((jax.nn.silu(a) * b).astype(BF16), w2_ref[sl, :], preferred_element_type=F32)
        y = part if y is None else y + part
    o_ref[0] = x_ref[0] + mod_ref[0][5:6] * y


def _ffn(x_all, h_all, mod_l, w1, w3, w2):
    row = lambda b, i: (b, i, 0)
    const2 = lambda b, i: (0, 0)
    return pl.pallas_call(
        _ffn_kernel,
        grid=(BATCH, N_TILES),
        in_specs=[pl.BlockSpec((1, TM, D_MODEL), row),
                  pl.BlockSpec((1, TM, D_MODEL), row),
                  pl.BlockSpec((1, 6, D_MODEL), _mod_index),
                  pl.BlockSpec((D_MODEL, D_FF), const2),
                  pl.BlockSpec((D_MODEL, D_FF), const2),
                  pl.BlockSpec((D_FF, D_MODEL), const2)],
        out_specs=pl.BlockSpec((1, TM, D_MODEL), row),
        out_shape=jax.ShapeDtypeStruct((BATCH, TT, D_MODEL), F32),
        compiler_params=_cparams("parallel", "arbitrary"),
        name="ffn",
    )(x_all, h_all, mod_l, w1, w3, w2)


def _route_kernel(logit_ref, dest_ref, gate_ref, count_ref, run_scr):
    phase, t = pl.program_id(0), pl.program_id(1)

    @pl.when((phase == 0) & (t == 0))
    def _():
        run_scr[...] = jnp.zeros_like(run_scr)

    lane = lax.broadcasted_iota(jnp.int32, (ROUTE_TILE, LANES), 1).astype(F32)
    logits = jnp.where(lane < MOE_EXPERTS, logit_ref[...], -jnp.inf)
    m1 = jnp.max(logits, axis=-1, keepdims=True)
    e1 = jnp.min(jnp.where(logits == m1, lane, float(LANES)), axis=-1, keepdims=True)
    rest = jnp.where(lane == e1, -jnp.inf, logits)
    m2 = jnp.max(rest, axis=-1, keepdims=True)
    e2 = jnp.min(jnp.where(rest == m2, lane, float(LANES)), axis=-1, keepdims=True)
    hot1 = (lane == e1).astype(F32)
    hot2 = (lane == e2).astype(F32)
    hot = hot1 + hot2
    tile_count = jnp.sum(hot, axis=0, keepdims=True)

    @pl.when(phase == 0)
    def _():
        run_scr[0:1] = run_scr[0:1] + tile_count

    @pl.when((phase == 1) & (t == 0))
    def _():
        counts = jnp.broadcast_to(run_scr[0:1], (8, LANES))
        count_ref[...] = counts.astype(jnp.int32)
        padded = jnp.ceil(counts * (1.0 / MOE_BLOCK)) * MOE_BLOCK
        ei = lax.broadcasted_iota(jnp.int32, (LANES, LANES), 0)
        ej = lax.broadcasted_iota(jnp.int32, (LANES, LANES), 1)
        before = (ei < ej).astype(F32)
        run_scr[1:2] = jnp.dot(padded, before, precision=HI, preferred_element_type=F32)[0:1]

    @pl.when(phase == 1)
    def _():
        ri = lax.broadcasted_iota(jnp.int32, (ROUTE_TILE, ROUTE_TILE), 0)
        rj = lax.broadcasted_iota(jnp.int32, (ROUTE_TILE, ROUTE_TILE), 1)
        earlier = (ri > rj).astype(BF16)
        within = jnp.dot(earlier, hot.astype(BF16), preferred_element_type=F32)
        pos = within + run_scr[1:2]
        d1 = jnp.sum(pos * hot1, axis=-1, keepdims=True)
        d2 = jnp.sum(pos * hot2, axis=-1, keepdims=True)
        dest_ref[...] = jnp.concatenate([d1, d2], axis=1).astype(jnp.int32)
        w2 = jnp.exp(m2 - m1)
        gate_ref[...] = jnp.concatenate([1.0 / (1.0 + w2), w2 / (1.0 + w2)], axis=1)
        run_scr[1:2] = run_scr[1:2] + tile_count


def _route(logits):
    nt = N_LAT // ROUTE_TILE
    return pl.pallas_call(
        _route_kernel,
        grid=(2, nt),
        in_specs=[pl.BlockSpec((ROUTE_TILE, LANES), lambda p, t: (t, 0))],
        out_specs=[pl.BlockSpec((ROUTE_TILE, MOE_TOP_K), lambda p, t: (t * p, 0)),
                   pl.BlockSpec((ROUTE_TILE, MOE_TOP_K), lambda p, t: (t * p, 0)),
                   pl.BlockSpec((8, LANES), lambda p, t: (0, 0))],
        out_shape=[jax.ShapeDtypeStruct((N_LAT, MOE_TOP_K), jnp.int32),
                   jax.ShapeDtypeStruct((N_LAT, MOE_TOP_K), F32),
                   jax.ShapeDtypeStruct((8, LANES), jnp.int32)],
        scratch_shapes=[pltpu.VMEM((8, LANES), F32)],
        compiler_params=_cparams("arbitrary", "arbitrary"),
        name="moe_route",
    )(logits)


N_PAD_SLOTS = MOE_ROWS - N_ASSIGN
Y2_ROWS = N_LAT + N_PAD_SLOTS // MOE_TOP_K


def _invert_kernel(dest_ref, bounds_ref, slot_ref):
    def real(a, c):
        slot_ref[dest_ref[a]] = a
        return c

    lax.fori_loop(0, N_ASSIGN, real, 0, unroll=8)

    def pad_range(e, count):
        def pad(s, cnt):
            slot_ref[s] = N_ASSIGN + cnt
            return cnt + 1

        return lax.fori_loop(bounds_ref[2 * e], bounds_ref[2 * e + 1], pad, count)

    lax.fori_loop(0, MOE_EXPERTS + 1, pad_range, 0)


def _invert(dest_flat, pad_bounds):
    return pl.pallas_call(
        _invert_kernel,
        grid_spec=pltpu.PrefetchScalarGridSpec(
            num_scalar_prefetch=2, grid=(1,), in_specs=[],
            out_specs=pl.BlockSpec(memory_space=pltpu.SMEM)),
        out_shape=jax.ShapeDtypeStruct((MOE_ROWS,), jnp.int32),
        compiler_params=_cparams("arbitrary"),
        name="moe_invert",
    )(dest_flat, pad_bounds)


LAST_BLOCK = MOE_N_BLOCKS - 1
MOE_NBUF = 3


def _swiglu_half(x, w1_ref, w3_ref, w2_ref):
    a = jnp.dot(x, w1_ref[0], preferred_element_type=F32)
    b = jnp.dot(x, w3_ref[0], preferred_element_type=F32)
    return jnp.dot((jax.nn.silu(a) * b).astype(BF16), w2_ref[0], preferred_element_type=F32)


def _expert_gather_kernel(be_ref, slot_ref, h_hbm, w1_ref, w3_ref, w2_ref, xs_ref, yb_ref, xbuf, sem):
    del be_ref
    j = pl.program_id(0)
    cur = j % MOE_NBUF

    def gather(blk, buf):
        base = blk * MOE_BLOCK
        for r in range(MOE_BLOCK):
            row = jnp.minimum(lax.shift_right_logical(slot_ref[base + r], 1), N_LAT - 1)
            pltpu.make_async_copy(h_hbm.at[pl.ds(row, 1)], xbuf.at[buf, pl.ds(r, 1)], sem.at[buf]).start()

    def wait(buf):
        pltpu.make_async_copy(h_hbm.at[pl.ds(0, MOE_BLOCK)], xbuf.at[buf], sem.at[buf]).wait()

    @pl.when(j == 0)
    def _():
        gather(0, 0)
        gather(1, 1)

    wait(cur)
    xs_ref[...] = xbuf[cur]
    gather(jnp.minimum(j + 2, LAST_BLOCK), (j + 2) % MOE_NBUF)
    yb_ref[...] = _swiglu_half(xs_ref[...].astype(BF16), w1_ref, w3_ref, w2_ref)

    @pl.when(j == LAST_BLOCK)
    def _():
        wait((j + 1) % MOE_NBUF)
        wait((j + 2) % MOE_NBUF)


def _expert_scatter_kernel(be_ref, slot_ref, xs_ref, w1_ref, w3_ref, w2_ref, yb_ref, y2_hbm, obuf, sem):
    del be_ref
    j = pl.program_id(0)
    cur = j % MOE_NBUF
    prev = (j + MOE_NBUF - 1) % MOE_NBUF

    def scatter(blk, buf):
        base = blk * MOE_BLOCK
        for r in range(MOE_BLOCK):
            a = slot_ref[base + r]
            pltpu.make_async_copy(obuf.at[buf, pl.ds(r, 1)],
                                  y2_hbm.at[a & 1, pl.ds(lax.shift_right_logical(a, 1), 1)], sem.at[buf]).start()

    def wait(buf):
        pltpu.make_async_copy(obuf.at[buf], y2_hbm.at[0, pl.ds(0, MOE_BLOCK)], sem.at[buf]).wait()

    def compute():
        obuf[cur] = yb_ref[...] + _swiglu_half(xs_ref[...].astype(BF16), w1_ref, w3_ref, w2_ref)

    @pl.when(j >= MOE_NBUF)
    def _():
        wait(cur)

    @pl.when(j == 0)
    def _():
        compute()

    @pl.when(j > 0)
    def _():
        scatter(j - 1, prev)
        compute()

    @pl.when(j == LAST_BLOCK)
    def _():
        scatter(j, cur)
        for b in range(MOE_NBUF):
            wait(b)


def _experts(block_expert, slot_src, h_lat, w1, w3, w2):
    def w_specs(c):
        return [pl.BlockSpec((1, D_MODEL, MOE_FF_BLK), lambda j, be, sl: (be[j], 0, c)),
                pl.BlockSpec((1, D_MODEL, MOE_FF_BLK), lambda j, be, sl: (be[j], 0, c)),
                pl.BlockSpec((1, MOE_FF_BLK, D_MODEL), lambda j, be, sl: (be[j], c, 0))]

    blk = pl.BlockSpec((MOE_BLOCK, D_MODEL), lambda j, be, sl: (j, 0))
    xs, yb = pl.pallas_call(
        _expert_gather_kernel,
        grid_spec=pltpu.PrefetchScalarGridSpec(
            num_scalar_prefetch=2, grid=(MOE_N_BLOCKS,),
            in_specs=[pl.BlockSpec(memory_space=pl.ANY)] + w_specs(0),
            out_specs=[blk, blk],
            scratch_shapes=[pltpu.VMEM((MOE_NBUF, MOE_BLOCK, D_MODEL), F32), pltpu.SemaphoreType.DMA((MOE_NBUF,))]),
        out_shape=[jax.ShapeDtypeStruct((MOE_ROWS, D_MODEL), F32)] * 2,
        compiler_params=_cparams("arbitrary"),
        name="moe_experts_gather",
    )(block_expert, slot_src, h_lat, w1, w3, w2)
    return pl.pallas_call(
        _expert_scatter_kernel,
        grid_spec=pltpu.PrefetchScalarGridSpec(
            num_scalar_prefetch=2, grid=(MOE_N_BLOCKS,),
            in_specs=[blk] + w_specs(1) + [blk],
            out_specs=pl.BlockSpec(memory_space=pl.ANY),
            scratch_shapes=[pltpu.VMEM((MOE_NBUF, MOE_BLOCK, D_MODEL), F32), pltpu.SemaphoreType.DMA((MOE_NBUF,))]),
        out_shape=jax.ShapeDtypeStruct((MOE_TOP_K, Y2_ROWS, D_MODEL), F32),
        compiler_params=_cparams("arbitrary"),
        name="moe_experts_scatter",
    )(block_expert, slot_src, xs, w1, w3, w2, yb)


def _combine_kernel(x_ref, gate_ref, mod_ref, g_ref, y2_ref, o_ref):
    gates = gate_ref[0]
    y = y2_ref[0] * gates[:, 0:1] + y2_ref[1] * gates[:, 1:2]
    x = x_ref[0] + mod_ref[0][5:6] * y
    o_ref[0] = x * lax.rsqrt(jnp.mean(x * x, axis=-1, keepdims=True) + NORM_EPS) * g_ref[...]


def _combine(x_lat, gates, mod_l, final_g, y2):
    nt = SEQ // TM
    return pl.pallas_call(
        _combine_kernel,
        grid=(BATCH, nt),
        in_specs=[pl.BlockSpec((1, TM, D_MODEL), lambda b, i: (b, i, 0)),
                  pl.BlockSpec((1, TM, MOE_TOP_K), lambda b, i: (b, i, 0)),
                  pl.BlockSpec((1, 6, D_MODEL), lambda b, i: (b, 0, 0)),
                  pl.BlockSpec((1, D_MODEL), lambda b, i: (0, 0)),
                  pl.BlockSpec((MOE_TOP_K, TM, D_MODEL), lambda b, i: (0, b * nt + i, 0))],
        out_specs=pl.BlockSpec((1, TM, D_MODEL), lambda b, i: (b, i, 0)),
        out_shape=jax.ShapeDtypeStruct((BATCH, SEQ, D_MODEL), F32),
        compiler_params=_cparams("parallel", "arbitrary"),
        name="moe_combine",
    )(x_lat, gates.reshape(BATCH, SEQ, MOE_TOP_K), mod_l, final_g.reshape(1, D_MODEL), y2)


def _reorder_w_in(w):
    s = np.cumsum((3 * DN_WIDTH, DN_WIDTH, 2 * DN_HEADS, 2 * DN_HEADS, ATTN_WIDTH, ATTN_KV_WIDTH, ATTN_KV_WIDTH,
                   MLP_WIDTH, MLP_WIDTH)).tolist()
    ba = w[:, s[1]:s[3]]
    return jnp.concatenate([w[:, :s[1]], w[:, s[3]:], ba,
                            jnp.zeros((D_MODEL, LANES - 4 * DN_HEADS), w.dtype)], axis=1).astype(BF16)


def kernel(x, c, ctx, c_ctx, mod_w, mod_b, norm1_g, norm2_g, w_in, conv_w, dn_a_log, dn_dt_bias, dn_norm_g,
           q_norm_g, k_norm_g, sgu_norm_g, sgu_w, sgu_b, w_out, ffn_w1, ffn_w3, ffn_w2, router_w, router_b,
           moe_w1, moe_w3, moe_w2, final_norm_g):
    assert DEPTH == 2 and x.shape == (BATCH, SEQ, D_MODEL) and ctx.shape == (BATCH, CTX_LEN, D_MODEL)
    cond8 = jnp.concatenate([c, c_ctx[None], jnp.zeros((8 - BATCH - 1, D_MODEL), F32)], axis=0)
    mod = _modulation(cond8, mod_w, mod_b)
    rope_c, rope_s = _rope_tables()
    x_all = jnp.concatenate([x, ctx], axis=1)
    for layer in range(DEPTH):
        last = layer == DEPTH - 1
        zqkv, zgate, zq, zk, zv, zu, zv2, zba = _in_proj(x_all, mod[layer], norm1_g[layer],
                                                         _reorder_w_in(w_in[layer]))
        qkv = _dn_prep(zqkv, conv_w[layer])
        o_dn = _deltanet(qkv, zba, dn_a_log[layer], dn_dt_bias[layer])
        qh, kh, vh = _attn_prep(zq, zk, zv, q_norm_g[layer], k_norm_g[layer], rope_c, rope_s)
        attn = _attention_lat(qh, kh, vh)
        if not last:
            attn = jnp.concatenate([attn, _attention_ctx(qh, kh, vh)], axis=2)
        mix_args = (x_all, mod[layer], o_dn, zgate, attn, zu, zv2, dn_norm_g[layer], sgu_norm_g[layer],
                    sgu_w[layer], sgu_b[layer], w_out[layer].astype(BF16), norm2_g[layer])
        if not last:
            i = layer // 2
            x_mid, h_mid = _mix(*mix_args)
            x_all = _ffn(x_mid, h_mid, mod[layer], ffn_w1[i].astype(BF16), ffn_w3[i].astype(BF16),
                         ffn_w2[i].astype(BF16))
        else:
            i = layer // 2
            x_lat, h_lat, logits = _mix(*mix_args, router=(router_w[i], router_b[i]))
            dest, gates, counts = _route(logits.reshape(N_LAT, LANES))
            cnt = counts[0, :MOE_EXPERTS]
            padded = (cnt + MOE_BLOCK - 1) // MOE_BLOCK * MOE_BLOCK
            pad_ends = jnp.cumsum(padded)
            block_expert = jnp.minimum(
                jnp.sum(pad_ends[None, :] <= (jnp.arange(MOE_N_BLOCKS) * MOE_BLOCK)[:, None], axis=1),
                MOE_EXPERTS - 1).astype(jnp.int32)
            lo = jnp.concatenate([pad_ends - padded + cnt, pad_ends[-1:]])
            hi = jnp.concatenate([pad_ends, jnp.full((1,), MOE_ROWS, pad_ends.dtype)])
            pad_bounds = jnp.stack([lo, hi], axis=1).reshape(-1).astype(jnp.int32)
            slot_src = _invert(dest.reshape(N_ASSIGN), pad_bounds)
            y2 = _experts(block_expert, slot_src, h_lat.reshape(N_LAT, D_MODEL), moe_w1[i].astype(BF16),
                          moe_w3[i].astype(BF16), moe_w2[i].astype(BF16))
            return _combine(x_lat, gates, mod[layer], final_norm_g, y2)
```

```python
import functools

import jax
import jax.numpy as jnp
import numpy as np
from jax import lax
from jax.experimental import pallas as pl
from jax.experimental.pallas import tpu as pltpu

D_MODEL = 1024
BATCH = 4
SEQ = 4096
DEPTH = 2
GRID_W = 64
CTX_LEN = 256
HEAD_DIM = 64
DN_HEADS = 6
ATTN_HEADS = 6
ATTN_KV_HEADS = 2
ATTN_GROUP = ATTN_HEADS // ATTN_KV_HEADS
MLP_GROUPS = 4
DN_WIDTH = DN_HEADS * HEAD_DIM
ATTN_WIDTH = ATTN_HEADS * HEAD_DIM
ATTN_KV_WIDTH = ATTN_KV_HEADS * HEAD_DIM
MLP_WIDTH = MLP_GROUPS * HEAD_DIM
D_MIX = DN_WIDTH + ATTN_WIDTH + MLP_WIDTH
CONV_K = 3
DN_CHUNK = 64
ATTN_SCALE = HEAD_DIM ** -0.5
LOG2E = 1.4426950408889634
MLP_CHUNK = 128
ROPE_THETA = 10000.0
ROPE_AXIS_DIM = HEAD_DIM // 2
ROPE_PAIRS = ROPE_AXIS_DIM // 2
D_FF = 2816
MOE_EXPERTS = 8
MOE_TOP_K = 2
MOE_D_FF = 3584
MOE_BLOCK = 256
NORM_EPS = 1e-6

LANES = 128
TT = SEQ + CTX_LEN
TM = 256
N_TILES = TT // TM
CTX_TILE = N_TILES - 1
CTX_ROW = BATCH
N_DN_STEPS = TT // DN_CHUNK
N_CTX_CHUNKS = CTX_LEN // DN_CHUNK
N_LAT_CHUNKS = SEQ // DN_CHUNK
IN_PAD = 3 * DN_WIDTH + DN_WIDTH + ATTN_WIDTH + 2 * ATTN_KV_WIDTH + 2 * MLP_WIDTH + LANES
N_LAT = BATCH * SEQ
N_ASSIGN = N_LAT * MOE_TOP_K
MOE_N_BLOCKS = -(-(N_ASSIGN + MOE_EXPERTS * (MOE_BLOCK - 1)) // MOE_BLOCK)
MOE_ROWS = MOE_N_BLOCKS * MOE_BLOCK
MOE_FF_SPLIT = 2
MOE_FF_BLK = MOE_D_FF // MOE_FF_SPLIT
ROUTE_TILE = 512
VMEM_LIMIT = 56 * 2 ** 20

F32 = jnp.float32
BF16 = jnp.bfloat16
HI = lax.Precision.HIGHEST


def _cparams(*sem):
    return pltpu.CompilerParams(dimension_semantics=sem, vmem_limit_bytes=VMEM_LIMIT)


def _bdot(a, b):
    return jnp.dot(a.astype(BF16), b.astype(BF16), preferred_element_type=F32)


def _bdot_nt(a, b):
    return lax.dot_general(a.astype(BF16), b.astype(BF16), (((1,), (1,)), ((), ())), preferred_element_type=F32)


def _bdot_tn(a, b):
    return lax.dot_general(a.astype(BF16), b.astype(BF16), (((0,), (0,)), ((), ())), preferred_element_type=F32)


def _seg_ones():
    r = lax.shift_right_logical(lax.broadcasted_iota(jnp.int32, (LANES, LANES), 0), 6)
    c = lax.shift_right_logical(lax.broadcasted_iota(jnp.int32, (LANES, LANES), 1), 6)
    return (r == c).astype(F32).astype(BF16)


def _split_bf16(x):
    hi = x.astype(BF16)
    return hi, (x - hi.astype(F32)).astype(BF16)


def _seg_sum(y, ones):
    parts = []
    for i in range(0, y.shape[-1], LANES):
        hi, lo = _split_bf16(y[:, i:i + LANES])
        parts.append(jnp.dot(hi, ones, preferred_element_type=F32) + jnp.dot(lo, ones, preferred_element_type=F32))
    return parts[0] if len(parts) == 1 else jnp.concatenate(parts, axis=-1)


def _softplus(x):
    return jnp.maximum(x, 0.0) + jnp.log1p(jnp.exp(-jnp.abs(x)))


def _ada_norm(x, gain, shift, scale):
    y = x * lax.rsqrt(jnp.mean(x * x, axis=-1, keepdims=True) + NORM_EPS) * gain
    return y * (1.0 + scale) + shift


def _mod_index(b, i):
    return (jnp.where(i == CTX_TILE, CTX_ROW, b), 0, 0)


def _mod_kernel(c_ref, w_ref, b_ref, o_ref):
    cond = jax.nn.silu(c_ref[...])
    o_ref[0] = jnp.dot(cond, w_ref[0], precision=HI, preferred_element_type=F32) + b_ref[0]


def _modulation(cond8, mod_w, mod_b):
    nblk = 4
    bn = 6 * D_MODEL // nblk
    out = pl.pallas_call(
        _mod_kernel,
        grid=(DEPTH, nblk),
        in_specs=[pl.BlockSpec((8, D_MODEL), lambda l, j: (0, 0)),
                  pl.BlockSpec((1, D_MODEL, bn), lambda l, j: (l, 0, j)),
                  pl.BlockSpec((1, 1, bn), lambda l, j: (l, 0, j))],
        out_specs=pl.BlockSpec((1, 8, bn), lambda l, j: (l, 0, j)),
        out_shape=jax.ShapeDtypeStruct((DEPTH, 8, 6 * D_MODEL), F32),
        compiler_params=_cparams("arbitrary", "arbitrary"),
        name="modulation",
    )(cond8, mod_w, mod_b.reshape(DEPTH, 1, 6 * D_MODEL))
    return out.reshape(DEPTH, 8, 6, D_MODEL)


_IN_SPLITS = (3 * DN_WIDTH, DN_WIDTH, ATTN_WIDTH, ATTN_KV_WIDTH, ATTN_KV_WIDTH, MLP_WIDTH, MLP_WIDTH, LANES)


def _tile_rows(refs, x_scr):
    if len(refs) == 1:
        return refs[0][0]
    x_ref, ctx_ref = refs
    i = pl.program_id(1)

    @pl.when(i != CTX_TILE)
    def _():
        x_scr[...] = x_ref[0]

    @pl.when(i == CTX_TILE)
    def _():
        x_scr[...] = ctx_ref[0]

    return x_scr[...]


def _row_specs(split):
    if not split:
        return [pl.BlockSpec((1, TM, D_MODEL), lambda b, i: (b, i, 0))]
    return [pl.BlockSpec((1, TM, D_MODEL), lambda b, i: (b, jnp.minimum(i, CTX_TILE - 1), 0)),
            pl.BlockSpec((1, CTX_LEN, D_MODEL), lambda b, i: (b, 0, 0))]


def _in_proj_kernel(*refs, split):
    n_x = 2 if split else 1
    mod_ref, g_ref, w_ref, gq_ref, gk_ref, c_ref, s_ref = refs[n_x:n_x + 7]
    oqkv, ogate, ou, ov2, oba, oq, ok, ov = refs[n_x + 7:n_x + 15]
    m = mod_ref[0]
    x = _tile_rows(refs[:n_x], refs[-1] if split else None)
    h = _ada_norm(x, g_ref[...], m[0:1], m[1:2]).astype(BF16)
    z = jnp.dot(h, w_ref[...], preferred_element_type=F32)
    cols = {}
    off = 0
    for name, width in zip(("qkv", "gate", "q", "k", "v", "u", "v2", "ba"), _IN_SPLITS):
        cols[name] = z[:, off:off + width]
        off += width
    for ref, name in ((oqkv, "qkv"), (ogate, "gate"), (ou, "u"), (ov2, "v2"), (oba, "ba")):
        ref[0] = cols[name].astype(ref.dtype)

    ones = _seg_ones()
    cs, sn = c_ref[...], s_ref[...]
    lane = lax.broadcasted_iota(jnp.int32, (1, LANES), 1)
    first_half = (lane & (2 * ROPE_PAIRS - 1)) < ROPE_PAIRS

    def norm_rope(t, gain):
        y = t * lax.rsqrt(_seg_sum(t * t, ones) * (1.0 / HEAD_DIM) + NORM_EPS) * gain
        partner = jnp.where(first_half, pltpu.roll(y, LANES - ROPE_PAIRS, 1), pltpu.roll(y, ROPE_PAIRS, 1))
        return y * cs + partner * sn

    for pair in range(ATTN_HEADS // 2):
        q2 = norm_rope(cols["q"][:, pair * LANES:(pair + 1) * LANES], gq_ref[...]) * (ATTN_SCALE * LOG2E)
        oq[0, 2 * pair] = q2[:, :HEAD_DIM].astype(oq.dtype)
        oq[0, 2 * pair + 1] = q2[:, HEAD_DIM:].astype(oq.dtype)
    k2 = norm_rope(cols["k"], gk_ref[...])
    v2 = cols["v"].astype(ov.dtype)
    one_cols = jnp.ones((TM, HEAD_DIM), ov.dtype)
    for hd in range(ATTN_KV_HEADS):
        ok[0, hd] = k2[:, hd * HEAD_DIM:(hd + 1) * HEAD_DIM].astype(ok.dtype)
        ov[0, hd] = jnp.concatenate([v2[:, hd * HEAD_DIM:(hd + 1) * HEAD_DIM], one_cols], axis=1)


def _in_proj(xs, mod_l, gain, w_in_r, q_gain, k_gain, rope_c, rope_s):
    split = len(xs) == 2
    row = lambda b, i: (b, i, 0)
    hrow = lambda b, i: (b, 0, i, 0)
    const2 = lambda b, i: (0, 0)
    tile2 = lambda g: jnp.tile(g.reshape(1, HEAD_DIM), (1, 2))
    widths = (3 * DN_WIDTH, DN_WIDTH, MLP_WIDTH, MLP_WIDTH, LANES)
    dts = (BF16,) * 4 + (F32,)
    return pl.pallas_call(
        functools.partial(_in_proj_kernel, split=split),
        grid=(BATCH, N_TILES),
        in_specs=_row_specs(split) + [
            pl.BlockSpec((1, 6, D_MODEL), _mod_index),
            pl.BlockSpec((1, D_MODEL), const2),
            pl.BlockSpec((D_MODEL, IN_PAD), const2),
            pl.BlockSpec((1, LANES), const2),
            pl.BlockSpec((1, LANES), const2),
            pl.BlockSpec((TM, LANES), lambda b, i: (i, 0)),
            pl.BlockSpec((TM, LANES), lambda b, i: (i, 0))],
        out_specs=[pl.BlockSpec((1, TM, w), row) for w in widths] + [
            pl.BlockSpec((1, ATTN_HEADS, TM, HEAD_DIM), hrow),
            pl.BlockSpec((1, ATTN_KV_HEADS, TM, HEAD_DIM), hrow),
            pl.BlockSpec((1, ATTN_KV_HEADS, TM, LANES), hrow)],
        out_shape=[jax.ShapeDtypeStruct((BATCH, TT, w), dt) for w, dt in zip(widths, dts)] + [
            jax.ShapeDtypeStruct((BATCH, ATTN_HEADS, TT, HEAD_DIM), BF16),
            jax.ShapeDtypeStruct((BATCH, ATTN_KV_HEADS, TT, HEAD_DIM), BF16),
            jax.ShapeDtypeStruct((BATCH, ATTN_KV_HEADS, TT, LANES), BF16)],
        scratch_shapes=[pltpu.VMEM((TM, D_MODEL), F32)] if split else [],
        compiler_params=_cparams("parallel", "arbitrary"),
        name="in_proj",
    )(*xs, mod_l, gain.reshape(1, D_MODEL), w_in_r, tile2(q_gain), tile2(k_gain), rope_c, rope_s)


def _dn_prep_kernel(z_ref, w_ref, o_ref):
    j = pl.program_id(1)
    z = z_ref[0].astype(F32)
    w = w_ref[...]
    row = lax.broadcasted_iota(jnp.int32, (TT, 1), 0)
    first = (row == 0) | (row == SEQ)
    last = (row == SEQ - 1) | (row == TT - 1)
    zp = jnp.where(first, 0.0, pltpu.roll(z, 1, 0))
    zn = jnp.where(last, 0.0, pltpu.roll(z, TT - 1, 0))
    y = jax.nn.silu(w[0:1] * zp + w[1:2] * z + w[2:3] * zn)
    n_qk = 2 * DN_WIDTH // LANES
    n_q = DN_WIDTH // LANES

    @pl.when(j < n_qk)
    def _():
        inv = lax.rsqrt(_seg_sum(y * y, _seg_ones()) + NORM_EPS)
        o_ref[0] = (y * inv * jnp.where(j < n_q, HEAD_DIM ** -0.5, 1.0)).astype(o_ref.dtype)

    @pl.when(j >= n_qk)
    def _():
        o_ref[0] = y.astype(o_ref.dtype)


def _dn_prep(zqkv, conv_w):
    nb = 3 * DN_WIDTH // LANES
    return pl.pallas_call(
        _dn_prep_kernel,
        grid=(BATCH, nb),
        in_specs=[pl.BlockSpec((1, TT, LANES), lambda b, j: (b, 0, j)),
                  pl.BlockSpec((CONV_K, LANES), lambda b, j: (0, j))],
        out_specs=pl.BlockSpec((1, TT, LANES), lambda b, j: (b, 0, j)),
        out_shape=jax.ShapeDtypeStruct((BATCH, TT, 3 * DN_WIDTH), BF16),
        compiler_params=_cparams("parallel", "arbitrary"),
        name="dn_prep",
    )(zqkv, conv_w)


DN_PAIRS = DN_HEADS // 2
DN_SUB = 4
LOG2_CHUNK = DN_CHUNK.bit_length() - 1


def _dn_kernel(qkv_ref, zba_ref, a_ref, dt_ref, o_ref, s_scr):
    C, P2 = DN_CHUNK, 2 * DN_CHUNK
    o_ref[...] = jnp.zeros_like(o_ref)
    s_scr[...] = jnp.zeros_like(s_scr)
    neg_decay_rate = -jnp.exp(a_ref[...])
    dt_bias = dt_ref[...]
    lo = lax.broadcasted_iota(jnp.int32, (1, P2), 1) < C

    def stack(a):
        return jnp.concatenate([jnp.where(lo, a, 0.0), jnp.where(lo, 0.0, a)], axis=0)

    def step(i, carry):
        ri = lax.broadcasted_iota(jnp.int32, (P2, P2), 0)
        ci = lax.broadcasted_iota(jnp.int32, (P2, P2), 1)
        ti = lax.broadcasted_iota(jnp.int32, (C, C), 0)
        tj = lax.broadcasted_iota(jnp.int32, (C, C), 1)
        same = lambda sh: lax.shift_right_logical(ri, sh) == lax.shift_right_logical(ci, sh)
        same_head = same(LOG2_CHUNK)
        eye = (ri == ci).astype(F32)

        ch = []
        for sub in range(DN_SUB):
            s = i * DN_SUB + sub
            chunk_of = (jnp.where(s < N_CTX_CHUNKS, N_LAT_CHUNKS + s, s - N_CTX_CHUNKS), N_DN_STEPS - 1 - s)
            for d in range(2):
                r0 = pl.multiple_of(chunk_of[d] * C, C)
                zba = zba_ref[0, pl.ds(r0, C), :]
                beta_all = jax.nn.sigmoid(zba)
                g_all = neg_decay_rate * _softplus(zba + dt_bias)
                tri = (ti >= tj) if d == 0 else (ti <= tj)
                gc_all = jnp.dot(tri.astype(F32), g_all, precision=HI, preferred_element_type=F32)
                gc_t = jnp.concatenate([gc_all, gc_all], axis=0).T
                last = C - 1 if d == 0 else 0
                for p in range(DN_PAIRS):
                    lb = (d * DN_HEADS + 2 * p, d * DN_HEADS + 2 * p + 1)
                    la = (2 * DN_HEADS + lb[0], 2 * DN_HEADS + lb[1])
                    col = lambda t, l: t[:, l:l + 1]
                    q2 = qkv_ref[0, pl.ds(r0, C), p * LANES:(p + 1) * LANES].astype(F32)
                    k2 = qkv_ref[0, pl.ds(r0, C), DN_WIDTH + p * LANES:DN_WIDTH + (p + 1) * LANES].astype(F32)
                    v2 = qkv_ref[0, pl.ds(r0, C), 2 * DN_WIDTH + p * LANES:2 * DN_WIDTH + (p + 1) * LANES].astype(F32)
                    beta2 = jnp.where(lo, col(beta_all, lb[0]), col(beta_all, lb[1]))
                    gc2 = jnp.where(lo, col(gc_all, la[0]), col(gc_all, la[1]))
                    gcol = jnp.concatenate([col(gc_all, la[0]), col(gc_all, la[1])], axis=0)
                    grow = jnp.where(lo, gc_t[la[0]:la[0] + 1, :], gc_t[la[1]:la[1] + 1, :])
                    g_last = jnp.where(lo, gc_all[last:last + 1, la[0]:la[0] + 1],
                                       gc_all[last:last + 1, la[1]:la[1] + 1])
                    e_diff = jnp.exp(gcol - grow)
                    order = (ri >= ci) if d == 0 else (ri <= ci)
                    kb2 = k2 * beta2
                    e_gc = jnp.exp(gc2)
                    ch.append(dict(
                        r0=r0, sub=sub, d=d, p=p, g_last=g_last,
                        dec_incl=jnp.where(same_head & order, e_diff, 0.0),
                        dec_strict=jnp.where(same_head & order & (ri != ci), e_diff, 0.0),
                        kq=jnp.concatenate([stack(kb2), stack(q2)], axis=0).astype(BF16),
                        k=stack(k2).astype(BF16),
                        rhs=jnp.concatenate([stack(v2 * beta2), stack(kb2 * e_gc)], axis=1).astype(BF16),
                        kd=stack(k2 * jnp.exp(g_last - gc2)).astype(BF16),
                        qe=stack(q2 * e_gc)))
        for c in ch:
            kk_qk = lax.dot_general(c["kq"], c["k"], (((1,), (1,)), ((), ())), preferred_element_type=F32)
            c["a"] = kk_qk[:P2] * c["dec_strict"]
            c["attn"] = (kk_qk[P2:] * c["dec_incl"]).astype(BF16)
            c["t"] = eye - jnp.where(same(1), c["a"], 0.0)
        for lvl in range(1, LOG2_CHUNK):
            joins = same(lvl + 1) & ~same(lvl)
            for c in ch:
                c["m"] = _bdot(jnp.where(joins, c["a"], 0.0), c["t"])
            for c in ch:
                c["t"] = c["t"] - _bdot(c["t"], c["m"])
        for c in ch:
            c["uw"] = _bdot(c["t"], c["rhs"]).astype(BF16)
        for c in ch:
            oa = jnp.dot(c["attn"], c["uw"], preferred_element_type=F32)
            c["o0"] = oa[:, :LANES]
            c["qp"] = c["qe"] - oa[:, LANES:]
            c["np"] = lax.dot_general(c["kd"], c["uw"], (((0,), (0,)), ((), ())), preferred_element_type=F32)

        states = [s_scr[j] for j in range(2 * DN_PAIRS)]
        for sub in range(DN_SUB):
            cur = [c for c in ch if c["sub"] == sub]
            for c in cur:
                j = c["d"] * DN_PAIRS + c["p"]
                c["r"] = _bdot(jnp.concatenate([c["qp"], c["np"][:, LANES:]], axis=0), states[j])
            for d in range(2):
                outs = []
                for c in cur:
                    if c["d"] != d:
                        continue
                    j = d * DN_PAIRS + c["p"]
                    o_st = c["o0"] + c["r"][:P2]
                    outs.append(o_st[:C] + o_st[C:])
                    states[j] = states[j] * jnp.exp(c["g_last"]) + c["np"][:, :LANES] - c["r"][P2:]
                    r0 = c["r0"]
                o_ref[0, pl.ds(r0, C), :] += jnp.concatenate(outs, axis=1)
        for j in range(2 * DN_PAIRS):
            s_scr[j] = states[j]
        return carry

    lax.fori_loop(0, N_DN_STEPS // DN_SUB, step, 0)


def _deltanet(qkv, zba, a_log, dt_bias):
    pad = lambda t: jnp.zeros((1, LANES), F32).at[0, 2 * DN_HEADS:4 * DN_HEADS].set(t.reshape(-1))
    return pl.pallas_call(
        _dn_kernel,
        grid=(BATCH,),
        in_specs=[pl.BlockSpec((1, TT, 3 * DN_WIDTH), lambda b: (b, 0, 0)),
                  pl.BlockSpec((1, TT, LANES), lambda b: (b, 0, 0)),
                  pl.BlockSpec((1, LANES), lambda b: (0, 0)),
                  pl.BlockSpec((1, LANES), lambda b: (0, 0))],
        out_specs=pl.BlockSpec((1, TT, DN_WIDTH), lambda b: (b, 0, 0)),
        out_shape=jax.ShapeDtypeStruct((BATCH, TT, DN_WIDTH), F32),
        scratch_shapes=[pltpu.VMEM((2 * DN_PAIRS, LANES, LANES), F32)],
        compiler_params=_cparams("parallel"),
        name="deltanet",
    )(qkv, zba, pad(a_log), pad(dt_bias))


def _rope_tables():
    rows = SEQ // GRID_W
    row = jnp.repeat(jnp.arange(rows, dtype=F32), GRID_W)
    col = jnp.tile(jnp.arange(GRID_W, dtype=F32), rows)
    inv = ROPE_THETA ** (-2.0 * jnp.arange(ROPE_PAIRS, dtype=F32) / ROPE_AXIS_DIM)
    ang = jnp.stack([row[:, None] * inv, col[:, None] * inv], axis=1)
    cos, sin = jnp.cos(ang), jnp.sin(ang)
    c = jnp.concatenate([cos[:, 0], cos[:, 0], cos[:, 1], cos[:, 1]], axis=-1)
    s = jnp.concatenate([-sin[:, 0], sin[:, 0], -sin[:, 1], sin[:, 1]], axis=-1)
    c = jnp.concatenate([c, jnp.ones((CTX_LEN, HEAD_DIM), F32)], axis=0)
    s = jnp.concatenate([s, jnp.zeros((CTX_LEN, HEAD_DIM), F32)], axis=0)
    return jnp.tile(c, (1, 2)), jnp.tile(s, (1, 2))


ATTN_TQ = 128
ATTN_NQ = 8
ATTN_KB = 512
ATTN_KEY_BLOCKS = tuple((j * ATTN_KB, ATTN_KB) for j in range(SEQ // ATTN_KB)) + ((SEQ, CTX_LEN),)


def _dot_nt(a, b):
    return lax.dot_general(a, b, (((1,), (1,)), ((), ())), preferred_element_type=F32)


def _attn_finish(acc):
    return acc[:, :HEAD_DIM] / acc[:, HEAD_DIM:HEAD_DIM + 1]


def _attn_lat_kernel(q_ref, k_ref, v_ref, o_ref, s0_scr, s1_scr):
    rows = ATTN_GROUP * ATTN_TQ
    s_scr = (s0_scr, s1_scr)
    mpart = [None] * ATTN_NQ
    acc = [None] * ATTN_NQ
    for stage in range(ATTN_NQ + 1):
        a, b = stage, stage - 1
        if a < ATTN_NQ:
            qa = q_ref[0, :, a * ATTN_TQ:(a + 1) * ATTN_TQ, :].reshape(rows, HEAD_DIM)
        if b >= 0:
            m_b = jnp.max(mpart[b], axis=-1, keepdims=True)
        for k0, kn in ATTN_KEY_BLOCKS:
            if a < ATTN_NQ:
                s = _dot_nt(qa, k_ref[0, 0, k0:k0 + kn, :])
                s_scr[a % 2][:, k0:k0 + kn] = s
                blk = functools.reduce(jnp.maximum, [s[:, i:i + LANES] for i in range(0, kn, LANES)])
                mpart[a] = blk if mpart[a] is None else jnp.maximum(mpart[a], blk)
            if b >= 0:
                p = jnp.exp2(s_scr[b % 2][:, k0:k0 + kn] - m_b).astype(BF16)
                pv = jnp.dot(p, v_ref[0, 0, k0:k0 + kn, :], preferred_element_type=F32)
                acc[b] = pv if acc[b] is None else acc[b] + pv
        if b >= 0:
            o = _attn_finish(acc[b]).reshape(ATTN_GROUP, ATTN_TQ, HEAD_DIM)
            o_ref[0, :, b * ATTN_TQ:(b + 1) * ATTN_TQ, :] = o.astype(o_ref.dtype)


def _attn_ctx_kernel(q_ref, k_ref, v_ref, o_ref):
    q = q_ref[0].reshape(ATTN_GROUP * CTX_LEN, HEAD_DIM)
    s = _dot_nt(q, k_ref[0, 0])
    p = jnp.exp2(s - jnp.max(s, axis=-1, keepdims=True)).astype(BF16)
    o = _attn_finish(jnp.dot(p, v_ref[0, 0], preferred_element_type=F32))
    o_ref[0] = o.reshape(ATTN_GROUP, CTX_LEN, HEAD_DIM).astype(o_ref.dtype)


def _attention_lat(qh, kh, vh):
    tq = ATTN_NQ * ATTN_TQ
    rows = ATTN_GROUP * ATTN_TQ
    return pl.pallas_call(
        _attn_lat_kernel,
        grid=(BATCH, ATTN_KV_HEADS, SEQ // tq),
        in_specs=[pl.BlockSpec((1, ATTN_GROUP, tq, HEAD_DIM), lambda b, g, i: (b, g, i, 0)),
                  pl.BlockSpec((1, 1, TT, HEAD_DIM), lambda b, g, i: (b, g, 0, 0)),
                  pl.BlockSpec((1, 1, TT, LANES), lambda b, g, i: (b, g, 0, 0))],
        out_specs=pl.BlockSpec((1, ATTN_GROUP, tq, HEAD_DIM), lambda b, g, i: (b, g, i, 0)),
        out_shape=jax.ShapeDtypeStruct((BATCH, ATTN_HEADS, SEQ, HEAD_DIM), BF16),
        scratch_shapes=[pltpu.VMEM((rows, TT), F32), pltpu.VMEM((rows, TT), F32)],
        compiler_params=_cparams("parallel", "parallel", "arbitrary"),
        name="attention_lat",
    )(qh, kh, vh)


def _attention_ctx(qh, kh, vh):
    ctx_blk = SEQ // CTX_LEN
    return pl.pallas_call(
        _attn_ctx_kernel,
        grid=(BATCH, ATTN_KV_HEADS),
        in_specs=[pl.BlockSpec((1, ATTN_GROUP, CTX_LEN, HEAD_DIM), lambda b, g: (b, g, ctx_blk, 0)),
                  pl.BlockSpec((1, 1, CTX_LEN, HEAD_DIM), lambda b, g: (b, g, ctx_blk, 0)),
                  pl.BlockSpec((1, 1, CTX_LEN, LANES), lambda b, g: (b, g, ctx_blk, 0))],
        out_specs=pl.BlockSpec((1, ATTN_GROUP, CTX_LEN, HEAD_DIM), lambda b, g: (b, g, 0, 0)),
        out_shape=jax.ShapeDtypeStruct((BATCH, ATTN_HEADS, CTX_LEN, HEAD_DIM), BF16),
        compiler_params=_cparams("parallel", "arbitrary"),
        name="attention_ctx",
    )(qh, kh, vh)


def _mix_kernel(*refs, moe, split):
    n = 2 if split else 1
    x_refs, refs = refs[:n], refs[n:]
    mod_ref, o_dn_ref, zgate_ref = refs[:3]
    attn_refs, refs = refs[3:3 + n], refs[3 + n:]
    zu_ref, zv_ref, dn_g_ref, sgu_g_ref, sgu_w_ref, sgu_b_ref, w_out_ref, g2_ref = refs[:8]
    rest = refs[8:]
    if moe:
        rw_ref, rb_ref, ox, oh, ologit = rest[:5]
    else:
        ox, oh = rest[:2]
    ones = _seg_ones()
    m = mod_ref[0]
    o = o_dn_ref[0]
    dn = o * lax.rsqrt(_seg_sum(o * o, ones) * (1.0 / HEAD_DIM) + NORM_EPS) * dn_g_ref[...]
    dn = dn * jax.nn.silu(zgate_ref[0].astype(F32))
    heads = lambda ref: jnp.concatenate([ref[0, h] for h in range(ATTN_HEADS)], axis=1)
    if split:
        x_scr, at_scr = rest[-2:]
        is_ctx = pl.program_id(1) == CTX_TILE

        @pl.when(jnp.logical_not(is_ctx))
        def _():
            at_scr[...] = heads(attn_refs[0])

        @pl.when(is_ctx)
        def _():
            at_scr[...] = heads(attn_refs[1])

        at = at_scr[...]
        x_in = _tile_rows(x_refs, x_scr)
    else:
        at = heads(attn_refs[0])
        x_in = x_refs[0][0]
    u = jax.nn.gelu(zu_ref[0].astype(F32))
    v = jax.nn.gelu(zv_ref[0].astype(F32))
    v = (v * lax.rsqrt(_seg_sum(v * v, ones) * (1.0 / HEAD_DIM) + NORM_EPS) * sgu_g_ref[...]).astype(BF16)
    gd = MLP_WIDTH // MLP_GROUPS
    chunks = []
    for ci in range(TM // MLP_CHUNK):
        vc = v[ci * MLP_CHUNK:(ci + 1) * MLP_CHUNK]
        mixed = jnp.concatenate(
            [jnp.dot(sgu_w_ref[g], vc[:, g * gd:(g + 1) * gd], preferred_element_type=F32)
             for g in range(MLP_GROUPS)], axis=1)
        chunks.append(mixed + sgu_b_ref[...])
    sg = u * jnp.concatenate(chunks, axis=0)
    mixed_all = jnp.concatenate([dn.astype(BF16), at, sg.astype(BF16)], axis=1)
    y = jnp.dot(mixed_all, w_out_ref[...], preferred_element_type=F32)
    x = x_in + m[2:3] * y
    ox[0] = x
    h = _ada_norm(x, g2_ref[...], m[3:4], m[4:5])
    oh[0] = h.astype(oh.dtype)
    if moe:
        h_hi, h_lo = _split_bf16(h)
        r_hi, r_lo = _split_bf16(rw_ref[...])
        ologit[0] = (jnp.dot(h_hi, r_hi, preferred_element_type=F32) + jnp.dot(h_hi, r_lo, preferred_element_type=F32)
                     + jnp.dot(h_lo, r_hi, preferred_element_type=F32)) + rb_ref[...]


def _mix(xs, mod_l, o_dn, zgate, attns, zu, zv2, dn_g, sgu_g, sgu_w, sgu_b, w_out, gain2, router=None):
    moe = router is not None
    split = len(xs) == 2
    assert len(attns) == len(xs)
    nt = N_TILES - 1 if moe else N_TILES
    rows = nt * TM
    row = lambda b, i: (b, i, 0)
    orow = row
    const2 = lambda b, i: (0, 0)
    gd = MLP_WIDTH // MLP_GROUPS
    sgu_b_x = jnp.repeat(sgu_b.T, gd, axis=1)
    if split:
        attn_specs = [pl.BlockSpec((1, ATTN_HEADS, TM, HEAD_DIM), lambda b, i: (b, 0, jnp.minimum(i, CTX_TILE - 1), 0)),
                      pl.BlockSpec((1, ATTN_HEADS, CTX_LEN, HEAD_DIM), lambda b, i: (b, 0, 0, 0))]
    else:
        attn_specs = [pl.BlockSpec((1, ATTN_HEADS, TM, HEAD_DIM), lambda b, i: (b, 0, i, 0))]
    in_specs = _row_specs(split) + [
                pl.BlockSpec((1, 6, D_MODEL), _mod_index),
                pl.BlockSpec((1, TM, DN_WIDTH), row),
                pl.BlockSpec((1, TM, DN_WIDTH), row)] + attn_specs + [
                pl.BlockSpec((1, TM, MLP_WIDTH), row),
                pl.BlockSpec((1, TM, MLP_WIDTH), row),
                pl.BlockSpec((1, DN_WIDTH), const2),
                pl.BlockSpec((1, MLP_WIDTH), const2),
                pl.BlockSpec((MLP_GROUPS, MLP_CHUNK, MLP_CHUNK), lambda b, i: (0, 0, 0)),
                pl.BlockSpec((MLP_CHUNK, MLP_WIDTH), const2),
                pl.BlockSpec((D_MIX, D_MODEL), const2),
                pl.BlockSpec((1, D_MODEL), const2)]
    args = [*xs, mod_l, o_dn, zgate, *attns, zu, zv2, jnp.tile(dn_g.reshape(1, HEAD_DIM), (1, DN_HEADS)),
            sgu_g.reshape(1, MLP_WIDTH), sgu_w.astype(BF16), sgu_b_x, w_out, gain2.reshape(1, D_MODEL)]
    out_specs = [pl.BlockSpec((1, TM, D_MODEL), orow), pl.BlockSpec((1, TM, D_MODEL), orow)]
    out_shape = [jax.ShapeDtypeStruct((BATCH, rows, D_MODEL), F32),
                 jax.ShapeDtypeStruct((BATCH, rows, D_MODEL), F32 if moe else BF16)]
    if moe:
        rw, rb = router
        in_specs += [pl.BlockSpec((D_MODEL, LANES), const2), pl.BlockSpec((1, LANES), const2)]
        args += [jnp.pad(rw, ((0, 0), (0, LANES - MOE_EXPERTS))),
                 jnp.pad(rb.reshape(1, MOE_EXPERTS), ((0, 0), (0, LANES - MOE_EXPERTS)))]
        out_specs.append(pl.BlockSpec((1, TM, LANES), orow))
        out_shape.append(jax.ShapeDtypeStruct((BATCH, rows, LANES), F32))
    return pl.pallas_call(
        functools.partial(_mix_kernel, moe=moe, split=split),
        grid=(BATCH, nt),
        in_specs=in_specs, out_specs=out_specs, out_shape=out_shape,
        scratch_shapes=[pltpu.VMEM((TM, D_MODEL), F32), pltpu.VMEM((TM, ATTN_WIDTH), BF16)] if split else [],
        compiler_params=_cparams("parallel", "arbitrary"),
        name="mix_moe" if moe else "mix",
    )(*args)


FFN_SPLIT = 2
FFN_BLK = D_FF // FFN_SPLIT


def _ffn_kernel(x_ref, h_ref, mod_ref, w1_ref, w3_ref, w2_ref, o_ref):
    h = h_ref[0]
    y = None
    for c in range(FFN_SPLIT):
        sl = slice(c * FFN_BLK, (c + 1) * FFN_BLK)
        a = jnp.dot(h, w1_ref[:, sl], preferred_element_type=F32)
        b = jnp.dot(h, w3_ref[:, sl], preferred_element_type=F32)
        part = jnp.dot((jax.nn.silu(a) * b).astype(BF16), w2_ref[sl, :], preferred_element_type=F32)
        y = part if y is None else y + part
    o_ref[0] = x_ref[0] + mod_ref[0][5:6] * y


def _ffn(x_all, h_all, mod_l, w1, w3, w2):
    row = lambda b, i: (b, i, 0)
    const2 = lambda b, i: (0, 0)
    return pl.pallas_call(
        _ffn_kernel,
        grid=(BATCH, N_TILES),
        in_specs=[pl.BlockSpec((1, TM, D_MODEL), row),
                  pl.BlockSpec((1, TM, D_MODEL), row),
                  pl.BlockSpec((1, 6, D_MODEL), _mod_index),
                  pl.BlockSpec((D_MODEL, D_FF), const2),
                  pl.BlockSpec((D_MODEL, D_FF), const2),
                  pl.BlockSpec((D_FF, D_MODEL), const2)],
        out_specs=pl.BlockSpec((1, TM, D_MODEL), row),
        out_shape=jax.ShapeDtypeStruct((BATCH, TT, D_MODEL), F32),
        compiler_params=_cparams("parallel", "arbitrary"),
        name="ffn",
    )(x_all, h_all, mod_l, w1, w3, w2)


def _route_kernel(logit_ref, dest_ref, gate_ref, count_ref, run_scr):
    phase, t = pl.program_id(0), pl.program_id(1)

    @pl.when((phase == 0) & (t == 0))
    def _():
        run_scr[...] = jnp.zeros_like(run_scr)

    lane = lax.broadcasted_iota(jnp.int32, (ROUTE_TILE, LANES), 1).astype(F32)
    logits = jnp.where(lane < MOE_EXPERTS, logit_ref[...], -jnp.inf)
    m1 = jnp.max(logits, axis=-1, keepdims=True)
    e1 = jnp.min(jnp.where(logits == m1, lane, float(LANES)), axis=-1, keepdims=True)
    rest = jnp.where(lane == e1, -jnp.inf, logits)
    m2 = jnp.max(rest, axis=-1, keepdims=True)
    e2 = jnp.min(jnp.where(rest == m2, lane, float(LANES)), axis=-1, keepdims=True)
    hot1 = (lane == e1).astype(F32)
    hot2 = (lane == e2).astype(F32)
    hot = hot1 + hot2
    tile_count = jnp.sum(hot, axis=0, keepdims=True)

    @pl.when(phase == 0)
    def _():
        run_scr[0:1] = run_scr[0:1] + tile_count

    @pl.when((phase == 1) & (t == 0))
    def _():
        counts = jnp.broadcast_to(run_scr[0:1], (8, LANES))
        count_ref[...] = counts.astype(jnp.int32)
        padded = jnp.ceil(counts * (1.0 / MOE_BLOCK)) * MOE_BLOCK
        ei = lax.broadcasted_iota(jnp.int32, (LANES, LANES), 0)
        ej = lax.broadcasted_iota(jnp.int32, (LANES, LANES), 1)
        before = (ei < ej).astype(F32)
        run_scr[1:2] = jnp.dot(padded, before, precision=HI, preferred_element_type=F32)[0:1]

    @pl.when(phase == 1)
    def _():
        ri = lax.broadcasted_iota(jnp.int32, (ROUTE_TILE, ROUTE_TILE), 0)
        rj = lax.broadcasted_iota(jnp.int32, (ROUTE_TILE, ROUTE_TILE), 1)
        earlier = (ri > rj).astype(BF16)
        within = jnp.dot(earlier, hot.astype(BF16), preferred_element_type=F32)
        pos = within + run_scr[1:2]
        d1 = jnp.sum(pos * hot1, axis=-1, keepdims=True)
        d2 = jnp.sum(pos * hot2, axis=-1, keepdims=True)
        dest_ref[...] = jnp.concatenate([d1, d2], axis=1).astype(jnp.int32)
        w2 = jnp.exp(m2 - m1)
        gate_ref[...] = jnp.concatenate([1.0 / (1.0 + w2), w2 / (1.0 + w2)], axis=1)
        run_scr[1:2] = run_scr[1:2] + tile_count


def _route(logits):
    nt = N_LAT // ROUTE_TILE
    return pl.pallas_call(
        _route_kernel,
        grid=(2, nt),
        in_specs=[pl.BlockSpec((ROUTE_TILE, LANES), lambda p, t: (t, 0))],
        out_specs=[pl.BlockSpec((ROUTE_TILE, MOE_TOP_K), lambda p, t: (t * p, 0)),
                   pl.BlockSpec((ROUTE_TILE, MOE_TOP_K), lambda p, t: (t * p, 0)),
                   pl.BlockSpec((8, LANES), lambda p, t: (0, 0))],
        out_shape=[jax.ShapeDtypeStruct((N_LAT, MOE_TOP_K), jnp.int32),
                   jax.ShapeDtypeStruct((N_LAT, MOE_TOP_K), F32),
                   jax.ShapeDtypeStruct((8, LANES), jnp.int32)],
        scratch_shapes=[pltpu.VMEM((8, LANES), F32)],
        compiler_params=_cparams("arbitrary", "arbitrary"),
        name="moe_route",
    )(logits)


N_PAD_SLOTS = MOE_ROWS - N_ASSIGN
Y2_ROWS = N_LAT + N_PAD_SLOTS // MOE_TOP_K


def _invert_kernel(dest_ref, bounds_ref, slot_ref):
    def real(a, c):
        slot_ref[dest_ref[a]] = a
        return c

    lax.fori_loop(0, N_ASSIGN, real, 0, unroll=8)

    def pad_range(e, count):
        def pad(s, cnt):
            slot_ref[s] = N_ASSIGN + cnt
            return cnt + 1

        return lax.fori_loop(bounds_ref[2 * e], bounds_ref[2 * e + 1], pad, count)

    lax.fori_loop(0, MOE_EXPERTS + 1, pad_range, 0)


def _invert(dest_flat, pad_bounds):
    return pl.pallas_call(
        _invert_kernel,
        grid_spec=pltpu.PrefetchScalarGridSpec(
            num_scalar_prefetch=2, grid=(1,), in_specs=[],
            out_specs=pl.BlockSpec(memory_space=pltpu.SMEM)),
        out_shape=jax.ShapeDtypeStruct((MOE_ROWS,), jnp.int32),
        compiler_params=_cparams("arbitrary"),
        name="moe_invert",
    )(dest_flat, pad_bounds)


LAST_BLOCK = MOE_N_BLOCKS - 1
MOE_NBUF = 3


def _swiglu_half(x, w1_ref, w3_ref, w2_ref):
    a = jnp.dot(x, w1_ref[0], preferred_element_type=F32)
    b = jnp.dot(x, w3_ref[0], preferred_element_type=F32)
    return jnp.dot((jax.nn.silu(a) * b).astype(BF16), w2_ref[0], preferred_element_type=F32)


def _expert_gather_kernel(be_ref, slot_ref, h_hbm, w1_ref, w3_ref, w2_ref, xs_ref, yb_ref, xbuf, sem):
    del be_ref
    j = pl.program_id(0)
    cur = j % MOE_NBUF

    def gather(blk, buf):
        base = blk * MOE_BLOCK
        for r in range(MOE_BLOCK):
            row = jnp.minimum(lax.shift_right_logical(slot_ref[base + r], 1), N_LAT - 1)
            pltpu.make_async_copy(h_hbm.at[pl.ds(row, 1)], xbuf.at[buf, pl.ds(r, 1)], sem.at[buf]).start()

    def wait(buf):
        pltpu.make_async_copy(h_hbm.at[pl.ds(0, MOE_BLOCK)], xbuf.at[buf], sem.at[buf]).wait()

    @pl.when(j == 0)
    def _():
        gather(0, 0)
        gather(1, 1)

    wait(cur)
    xs_ref[...] = xbuf[cur]
    gather(jnp.minimum(j + 2, LAST_BLOCK), (j + 2) % MOE_NBUF)
    yb_ref[...] = _swiglu_half(xs_ref[...].astype(BF16), w1_ref, w3_ref, w2_ref)

    @pl.when(j == LAST_BLOCK)
    def _():
        wait((j + 1) % MOE_NBUF)
        wait((j + 2) % MOE_NBUF)


def _expert_scatter_kernel(be_ref, slot_ref, xs_ref, w1_ref, w3_ref, w2_ref, yb_ref, y2_hbm, obuf, sem):
    del be_ref
    j = pl.program_id(0)
    cur = j % MOE_NBUF
    prev = (j + MOE_NBUF - 1) % MOE_NBUF

    def scatter(blk, buf):
        base = blk * MOE_BLOCK
        for r in range(MOE_BLOCK):
            a = slot_ref[base + r]
            pltpu.make_async_copy(obuf.at[buf, pl.ds(r, 1)],
                                  y2_hbm.at[a & 1, pl.ds(lax.shift_right_logical(a, 1), 1)], sem.at[buf]).start()

    def wait(buf):
        pltpu.make_async_copy(obuf.at[buf], y2_hbm.at[0, pl.ds(0, MOE_BLOCK)], sem.at[buf]).wait()

    def compute():
        obuf[cur] = yb_ref[...] + _swiglu_half(xs_ref[...].astype(BF16), w1_ref, w3_ref, w2_ref)

    @pl.when(j >= MOE_NBUF)
    def _():
        wait(cur)

    @pl.when(j == 0)
    def _():
        compute()

    @pl.when(j > 0)
    def _():
        scatter(j - 1, prev)
        compute()

    @pl.when(j == LAST_BLOCK)
    def _():
        scatter(j, cur)
        for b in range(MOE_NBUF):
            wait(b)


def _experts(block_expert, slot_src, h_lat, w1, w3, w2):
    def w_specs(c):
        return [pl.BlockSpec((1, D_MODEL, MOE_FF_BLK), lambda j, be, sl: (be[j], 0, c)),
                pl.BlockSpec((1, D_MODEL, MOE_FF_BLK), lambda j, be, sl: (be[j], 0, c)),
                pl.BlockSpec((1, MOE_FF_BLK, D_MODEL), lambda j, be, sl: (be[j], c, 0))]

    blk = pl.BlockSpec((MOE_BLOCK, D_MODEL), lambda j, be, sl: (j, 0))
    xs, yb = pl.pallas_call(
        _expert_gather_kernel,
        grid_spec=pltpu.PrefetchScalarGridSpec(
            num_scalar_prefetch=2, grid=(MOE_N_BLOCKS,),
            in_specs=[pl.BlockSpec(memory_space=pl.ANY)] + w_specs(0),
            out_specs=[blk, blk],
            scratch_shapes=[pltpu.VMEM((MOE_NBUF, MOE_BLOCK, D_MODEL), F32), pltpu.SemaphoreType.DMA((MOE_NBUF,))]),
        out_shape=[jax.ShapeDtypeStruct((MOE_ROWS, D_MODEL), F32)] * 2,
        compiler_params=_cparams("arbitrary"),
        name="moe_experts_gather",
    )(block_expert, slot_src, h_lat, w1, w3, w2)
    return pl.pallas_call(
        _expert_scatter_kernel,
        grid_spec=pltpu.PrefetchScalarGridSpec(
            num_scalar_prefetch=2, grid=(MOE_N_BLOCKS,),
            in_specs=[blk] + w_specs(1) + [blk],
            out_specs=pl.BlockSpec(memory_space=pl.ANY),
            scratch_shapes=[pltpu.VMEM((MOE_NBUF, MOE_BLOCK, D_MODEL), F32), pltpu.SemaphoreType.DMA((MOE_NBUF,))]),
        out_shape=jax.ShapeDtypeStruct((MOE_TOP_K, Y2_ROWS, D_MODEL), F32),
        compiler_params=_cparams("arbitrary"),
        name="moe_experts_scatter",
    )(block_expert, slot_src, xs, w1, w3, w2, yb)


def _combine_kernel(x_ref, gate_ref, mod_ref, g_ref, y2_ref, o_ref):
    gates = gate_ref[0]
    y = y2_ref[0] * gates[:, 0:1] + y2_ref[1] * gates[:, 1:2]
    x = x_ref[0] + mod_ref[0][5:6] * y
    o_ref[0] = x * lax.rsqrt(jnp.mean(x * x, axis=-1, keepdims=True) + NORM_EPS) * g_ref[...]


def _combine(x_lat, gates, mod_l, final_g, y2):
    nt = SEQ // TM
    return pl.pallas_call(
        _combine_kernel,
        grid=(BATCH, nt),
        in_specs=[pl.BlockSpec((1, TM, D_MODEL), lambda b, i: (b, i, 0)),
                  pl.BlockSpec((1, TM, MOE_TOP_K), lambda b, i: (b, i, 0)),
                  pl.BlockSpec((1, 6, D_MODEL), lambda b, i: (b, 0, 0)),
                  pl.BlockSpec((1, D_MODEL), lambda b, i: (0, 0)),
                  pl.BlockSpec((MOE_TOP_K, TM, D_MODEL), lambda b, i: (0, b * nt + i, 0))],
        out_specs=pl.BlockSpec((1, TM, D_MODEL), lambda b, i: (b, i, 0)),
        out_shape=jax.ShapeDtypeStruct((BATCH, SEQ, D_MODEL), F32),
        compiler_params=_cparams("parallel", "arbitrary"),
        name="moe_combine",
    )(x_lat, gates.reshape(BATCH, SEQ, MOE_TOP_K), mod_l, final_g.reshape(1, D_MODEL), y2)


def _reorder_w_in(w):
    s = np.cumsum((3 * DN_WIDTH, DN_WIDTH, 2 * DN_HEADS, 2 * DN_HEADS, ATTN_WIDTH, ATTN_KV_WIDTH, ATTN_KV_WIDTH,
                   MLP_WIDTH, MLP_WIDTH)).tolist()
    ba = w[:, s[1]:s[3]]
    return jnp.concatenate([w[:, :s[1]], w[:, s[3]:], ba,
                            jnp.zeros((D_MODEL, LANES - 4 * DN_HEADS), w.dtype)], axis=1).astype(BF16)


def kernel(x, c, ctx, c_ctx, mod_w, mod_b, norm1_g, norm2_g, w_in, conv_w, dn_a_log, dn_dt_bias, dn_norm_g,
           q_norm_g, k_norm_g, sgu_norm_g, sgu_w, sgu_b, w_out, ffn_w1, ffn_w3, ffn_w2, router_w, router_b,
           moe_w1, moe_w3, moe_w2, final_norm_g):
    assert DEPTH == 2 and x.shape == (BATCH, SEQ, D_MODEL) and ctx.shape == (BATCH, CTX_LEN, D_MODEL)
    cond8 = jnp.concatenate([c, c_ctx[None], jnp.zeros((8 - BATCH - 1, D_MODEL), F32)], axis=0)
    mod = _modulation(cond8, mod_w, mod_b)
    rope_c, rope_s = _rope_tables()
    xs = (x, ctx)
    for layer in range(DEPTH):
        last = layer == DEPTH - 1
        zqkv, zgate, zu, zv2, zba, qh, kh, vh = _in_proj(xs, mod[layer], norm1_g[layer], _reorder_w_in(w_in[layer]),
                                                         q_norm_g[layer], k_norm_g[layer], rope_c, rope_s)
        qkv = _dn_prep(zqkv, conv_w[layer])
        o_dn = _deltanet(qkv, zba, dn_a_log[layer], dn_dt_bias[layer])
        attns = (_attention_lat(qh, kh, vh),)
        if len(xs) == 2:
            attns += (_attention_ctx(qh, kh, vh),)
        elif not last:
            attns = (jnp.concatenate([attns[0], _attention_ctx(qh, kh, vh)], axis=2),)
        mix_args = (xs, mod[layer], o_dn, zgate, attns, zu, zv2, dn_norm_g[layer], sgu_norm_g[layer],
                    sgu_w[layer], sgu_b[layer], w_out[layer].astype(BF16), norm2_g[layer])
        if not last:
            i = layer // 2
            x_mid, h_mid = _mix(*mix_args)
            xs = (_ffn(x_mid, h_mid, mod[layer], ffn_w1[i].astype(BF16), ffn_w3[i].astype(BF16),
                       ffn_w2[i].astype(BF16)),)
        else:
            i = layer // 2
            x_lat, h_lat, logits = _mix(*mix_args, router=(router_w[i], router_b[i]))
            dest, gates, counts = _route(logits.reshape(N_LAT, LANES))
            cnt = counts[0, :MOE_EXPERTS]
            padded = (cnt + MOE_BLOCK - 1) // MOE_BLOCK * MOE_BLOCK
            pad_ends = jnp.cumsum(padded)
            block_expert = jnp.minimum(
                jnp.sum(pad_ends[None, :] <= (jnp.arange(MOE_N_BLOCKS) * MOE_BLOCK)[:, None], axis=1),
                MOE_EXPERTS - 1).astype(jnp.int32)
            lo = jnp.concatenate([pad_ends - padded + cnt, pad_ends[-1:]])
            hi = jnp.concatenate([pad_ends, jnp.full((1,), MOE_ROWS, pad_ends.dtype)])
            pad_bounds = jnp.stack([lo, hi], axis=1).reshape(-1).astype(jnp.int32)
            slot_src = _invert(dest.reshape(N_ASSIGN), pad_bounds)
            y2 = _experts(block_expert, slot_src, h_lat.reshape(N_LAT, D_MODEL), moe_w1[i].astype(BF16),
                          moe_w3[i].astype(BF16), moe_w2[i].astype(BF16))
            return _combine(x_lat, gates, mod[layer], final_norm_g, y2)
```

```python
import functools

import jax
import jax.numpy as jnp
import numpy as np
from jax import lax
from jax.experimental import pallas as pl
from jax.experimental.pallas import tpu as pltpu

D_MODEL = 1024
BATCH = 4
SEQ = 4096
DEPTH = 2
GRID_W = 64
CTX_LEN = 256
HEAD_DIM = 64
DN_HEADS = 6
ATTN_HEADS = 6
ATTN_KV_HEADS = 2
ATTN_GROUP = ATTN_HEADS // ATTN_KV_HEADS
MLP_GROUPS = 4
DN_WIDTH = DN_HEADS * HEAD_DIM
ATTN_WIDTH = ATTN_HEADS * HEAD_DIM
ATTN_KV_WIDTH = ATTN_KV_HEADS * HEAD_DIM
MLP_WIDTH = MLP_GROUPS * HEAD_DIM
D_MIX = DN_WIDTH + ATTN_WIDTH + MLP_WIDTH
CONV_K = 3
DN_CHUNK = 64
ATTN_SCALE = HEAD_DIM ** -0.5
LOG2E = 1.4426950408889634
MLP_CHUNK = 128
ROPE_THETA = 10000.0
ROPE_AXIS_DIM = HEAD_DIM // 2
ROPE_PAIRS = ROPE_AXIS_DIM // 2
D_FF = 2816
MOE_EXPERTS = 8
MOE_TOP_K = 2
MOE_D_FF = 3584
MOE_BLOCK = 256
NORM_EPS = 1e-6

LANES = 128
SLAB = 8
TT = SEQ + CTX_LEN
TM = 256
N_TILES = TT // TM
CTX_TILE = N_TILES - 1
CTX_ROW = BATCH
N_DN_STEPS = TT // DN_CHUNK
N_CTX_CHUNKS = CTX_LEN // DN_CHUNK
N_LAT_CHUNKS = SEQ // DN_CHUNK
IN_PAD = 3 * DN_WIDTH + DN_WIDTH + ATTN_WIDTH + 2 * ATTN_KV_WIDTH + 2 * MLP_WIDTH + LANES
N_LAT = BATCH * SEQ
N_ASSIGN = N_LAT * MOE_TOP_K
MOE_N_BLOCKS = -(-(N_ASSIGN + MOE_EXPERTS * (MOE_BLOCK - 1)) // MOE_BLOCK)
MOE_ROWS = MOE_N_BLOCKS * MOE_BLOCK
MOE_FF_SPLIT = 2
MOE_FF_BLK = MOE_D_FF // MOE_FF_SPLIT
assert SLAB * LANES == D_MODEL
ROUTE_TILE = 512
VMEM_LIMIT = 56 * 2 ** 20

F32 = jnp.float32
BF16 = jnp.bfloat16
HI = lax.Precision.HIGHEST


def _cparams(*sem):
    return pltpu.CompilerParams(dimension_semantics=sem, vmem_limit_bytes=VMEM_LIMIT)


def _bdot(a, b):
    return jnp.dot(a.astype(BF16), b.astype(BF16), preferred_element_type=F32)


def _bdot_nt(a, b):
    return lax.dot_general(a.astype(BF16), b.astype(BF16), (((1,), (1,)), ((), ())), preferred_element_type=F32)


def _bdot_tn(a, b):
    return lax.dot_general(a.astype(BF16), b.astype(BF16), (((0,), (0,)), ((), ())), preferred_element_type=F32)


def _seg_ones():
    r = lax.shift_right_logical(lax.broadcasted_iota(jnp.int32, (LANES, LANES), 0), 6)
    c = lax.shift_right_logical(lax.broadcasted_iota(jnp.int32, (LANES, LANES), 1), 6)
    return (r == c).astype(F32).astype(BF16)


def _split_bf16(x):
    hi = x.astype(BF16)
    return hi, (x - hi.astype(F32)).astype(BF16)


def _seg_sum(y, ones):
    parts = []
    for i in range(0, y.shape[-1], LANES):
        hi, lo = _split_bf16(y[:, i:i + LANES])
        parts.append(jnp.dot(hi, ones, preferred_element_type=F32) + jnp.dot(lo, ones, preferred_element_type=F32))
    return parts[0] if len(parts) == 1 else jnp.concatenate(parts, axis=-1)


def _softplus(x):
    return jnp.maximum(x, 0.0) + jnp.log1p(jnp.exp(-jnp.abs(x)))


def _ada_norm(x, gain, shift, scale):
    y = x * lax.rsqrt(jnp.mean(x * x, axis=-1, keepdims=True) + NORM_EPS) * gain
    return y * (1.0 + scale) + shift


def _mod_index(b, i):
    return (jnp.where(i == CTX_TILE, CTX_ROW, b), 0, 0)


def _mod_kernel(c_ref, w_ref, b_ref, o_ref):
    cond = jax.nn.silu(c_ref[...])
    o_ref[0] = jnp.dot(cond, w_ref[0], precision=HI, preferred_element_type=F32) + b_ref[0]


def _modulation(cond8, mod_w, mod_b):
    nblk = 4
    bn = 6 * D_MODEL // nblk
    out = pl.pallas_call(
        _mod_kernel,
        grid=(DEPTH, nblk),
        in_specs=[pl.BlockSpec((8, D_MODEL), lambda l, j: (0, 0)),
                  pl.BlockSpec((1, D_MODEL, bn), lambda l, j: (l, 0, j)),
                  pl.BlockSpec((1, 1, bn), lambda l, j: (l, 0, j))],
        out_specs=pl.BlockSpec((1, 8, bn), lambda l, j: (l, 0, j)),
        out_shape=jax.ShapeDtypeStruct((DEPTH, 8, 6 * D_MODEL), F32),
        compiler_params=_cparams("arbitrary", "arbitrary"),
        name="modulation",
    )(cond8, mod_w, mod_b.reshape(DEPTH, 1, 6 * D_MODEL))
    return out.reshape(DEPTH, 8, 6, D_MODEL)


_IN_SPLITS = (3 * DN_WIDTH, DN_WIDTH, ATTN_WIDTH, ATTN_KV_WIDTH, ATTN_KV_WIDTH, MLP_WIDTH, MLP_WIDTH, LANES)


def _tile_rows(refs, x_scr):
    if len(refs) == 1:
        return refs[0][0]
    x_ref, ctx_ref = refs
    i = pl.program_id(1)

    @pl.when(i != CTX_TILE)
    def _():
        x_scr[...] = x_ref[0]

    @pl.when(i == CTX_TILE)
    def _():
        x_scr[...] = ctx_ref[0]

    return x_scr[...]


def _row_specs(split):
    if not split:
        return [pl.BlockSpec((1, TM, D_MODEL), lambda b, i: (b, i, 0))]
    return [pl.BlockSpec((1, TM, D_MODEL), lambda b, i: (b, jnp.minimum(i, CTX_TILE - 1), 0)),
            pl.BlockSpec((1, CTX_LEN, D_MODEL), lambda b, i: (b, 0, 0))]


def _in_proj_kernel(*refs, split):
    n_x = 2 if split else 1
    mod_ref, g_ref, w_ref, gq_ref, gk_ref, c_ref, s_ref = refs[n_x:n_x + 7]
    oqkv, ogate, ou, ov2, oba, oq, ok, ov = refs[n_x + 7:n_x + 15]
    m = mod_ref[0]
    x = _tile_rows(refs[:n_x], refs[-1] if split else None)
    h = _ada_norm(x, g_ref[...], m[0:1], m[1:2]).astype(BF16)
    z = jnp.dot(h, w_ref[...], preferred_element_type=F32)
    cols = {}
    off = 0
    for name, width in zip(("qkv", "gate", "q", "k", "v", "u", "v2", "ba"), _IN_SPLITS):
        cols[name] = z[:, off:off + width]
        off += width
    for ref, name in ((oqkv, "qkv"), (ogate, "gate"), (ou, "u"), (ov2, "v2"), (oba, "ba")):
        ref[0] = cols[name].astype(ref.dtype)

    ones = _seg_ones()
    cs, sn = c_ref[...], s_ref[...]
    lane = lax.broadcasted_iota(jnp.int32, (1, LANES), 1)
    first_half = (lane & (2 * ROPE_PAIRS - 1)) < ROPE_PAIRS

    def norm_rope(t, gain):
        y = t * lax.rsqrt(_seg_sum(t * t, ones) * (1.0 / HEAD_DIM) + NORM_EPS) * gain
        partner = jnp.where(first_half, pltpu.roll(y, LANES - ROPE_PAIRS, 1), pltpu.roll(y, ROPE_PAIRS, 1))
        return y * cs + partner * sn

    for pair in range(ATTN_HEADS // 2):
        q2 = norm_rope(cols["q"][:, pair * LANES:(pair + 1) * LANES], gq_ref[...]) * (ATTN_SCALE * LOG2E)
        oq[0, 2 * pair] = q2[:, :HEAD_DIM].astype(oq.dtype)
        oq[0, 2 * pair + 1] = q2[:, HEAD_DIM:].astype(oq.dtype)
    k2 = norm_rope(cols["k"], gk_ref[...])
    v2 = cols["v"].astype(ov.dtype)
    one_cols = jnp.ones((TM, HEAD_DIM), ov.dtype)
    for hd in range(ATTN_KV_HEADS):
        ok[0, hd] = k2[:, hd * HEAD_DIM:(hd + 1) * HEAD_DIM].astype(ok.dtype)
        ov[0, hd] = jnp.concatenate([v2[:, hd * HEAD_DIM:(hd + 1) * HEAD_DIM], one_cols], axis=1)


def _in_proj(xs, mod_l, gain, w_in_r, q_gain, k_gain, rope_c, rope_s):
    split = len(xs) == 2
    row = lambda b, i: (b, i, 0)
    hrow = lambda b, i: (b, 0, i, 0)
    const2 = lambda b, i: (0, 0)
    tile2 = lambda g: jnp.tile(g.reshape(1, HEAD_DIM), (1, 2))
    widths = (3 * DN_WIDTH, DN_WIDTH, MLP_WIDTH, MLP_WIDTH, LANES)
    dts = (BF16,) * 4 + (F32,)
    return pl.pallas_call(
        functools.partial(_in_proj_kernel, split=split),
        grid=(BATCH, N_TILES),
        in_specs=_row_specs(split) + [
            pl.BlockSpec((1, 6, D_MODEL), _mod_index),
            pl.BlockSpec((1, D_MODEL), const2),
            pl.BlockSpec((D_MODEL, IN_PAD), const2),
            pl.BlockSpec((1, LANES), const2),
            pl.BlockSpec((1, LANES), const2),
            pl.BlockSpec((TM, LANES), lambda b, i: (i, 0)),
            pl.BlockSpec((TM, LANES), lambda b, i: (i, 0))],
        out_specs=[pl.BlockSpec((1, TM, w), row) for w in widths] + [
            pl.BlockSpec((1, ATTN_HEADS, TM, HEAD_DIM), hrow),
            pl.BlockSpec((1, ATTN_KV_HEADS, TM, HEAD_DIM), hrow),
            pl.BlockSpec((1, ATTN_KV_HEADS, TM, LANES), hrow)],
        out_shape=[jax.ShapeDtypeStruct((BATCH, TT, w), dt) for w, dt in zip(widths, dts)] + [
            jax.ShapeDtypeStruct((BATCH, ATTN_HEADS, TT, HEAD_DIM), BF16),
            jax.ShapeDtypeStruct((BATCH, ATTN_KV_HEADS, TT, HEAD_DIM), BF16),
            jax.ShapeDtypeStruct((BATCH, ATTN_KV_HEADS, TT, LANES), BF16)],
        scratch_shapes=[pltpu.VMEM((TM, D_MODEL), F32)] if split else [],
        compiler_params=_cparams("parallel", "arbitrary"),
        name="in_proj",
    )(*xs, mod_l, gain.reshape(1, D_MODEL), w_in_r, tile2(q_gain), tile2(k_gain), rope_c, rope_s)


def _dn_prep_kernel(z_ref, w_ref, o_ref):
    j = pl.program_id(1)
    z = z_ref[0].astype(F32)
    w = w_ref[...]
    row = lax.broadcasted_iota(jnp.int32, (TT, 1), 0)
    first = (row == 0) | (row == SEQ)
    last = (row == SEQ - 1) | (row == TT - 1)
    zp = jnp.where(first, 0.0, pltpu.roll(z, 1, 0))
    zn = jnp.where(last, 0.0, pltpu.roll(z, TT - 1, 0))
    y = jax.nn.silu(w[0:1] * zp + w[1:2] * z + w[2:3] * zn)
    n_qk = 2 * DN_WIDTH // LANES
    n_q = DN_WIDTH // LANES

    @pl.when(j < n_qk)
    def _():
        inv = lax.rsqrt(_seg_sum(y * y, _seg_ones()) + NORM_EPS)
        o_ref[0] = (y * inv * jnp.where(j < n_q, HEAD_DIM ** -0.5, 1.0)).astype(o_ref.dtype)

    @pl.when(j >= n_qk)
    def _():
        o_ref[0] = y.astype(o_ref.dtype)


def _dn_prep(zqkv, conv_w):
    nb = 3 * DN_WIDTH // LANES
    return pl.pallas_call(
        _dn_prep_kernel,
        grid=(BATCH, nb),
        in_specs=[pl.BlockSpec((1, TT, LANES), lambda b, j: (b, 0, j)),
                  pl.BlockSpec((CONV_K, LANES), lambda b, j: (0, j))],
        out_specs=pl.BlockSpec((1, TT, LANES), lambda b, j: (b, 0, j)),
        out_shape=jax.ShapeDtypeStruct((BATCH, TT, 3 * DN_WIDTH), BF16),
        compiler_params=_cparams("parallel", "arbitrary"),
        name="dn_prep",
    )(zqkv, conv_w)


DN_PAIRS = DN_HEADS // 2
DN_SUB = 4
LOG2_CHUNK = DN_CHUNK.bit_length() - 1


def _dn_kernel(qkv_ref, zba_ref, a_ref, dt_ref, o_ref, s_scr):
    C, P2 = DN_CHUNK, 2 * DN_CHUNK
    o_ref[...] = jnp.zeros_like(o_ref)
    s_scr[...] = jnp.zeros_like(s_scr)
    neg_decay_rate = -jnp.exp(a_ref[...])
    dt_bias = dt_ref[...]
    lo = lax.broadcasted_iota(jnp.int32, (1, P2), 1) < C

    def stack(a):
        return jnp.concatenate([jnp.where(lo, a, 0.0), jnp.where(lo, 0.0, a)], axis=0)

    def step(i, carry):
        ri = lax.broadcasted_iota(jnp.int32, (P2, P2), 0)
        ci = lax.broadcasted_iota(jnp.int32, (P2, P2), 1)
        ti = lax.broadcasted_iota(jnp.int32, (C, C), 0)
        tj = lax.broadcasted_iota(jnp.int32, (C, C), 1)
        same = lambda sh: lax.shift_right_logical(ri, sh) == lax.shift_right_logical(ci, sh)
        same_head = same(LOG2_CHUNK)
        eye = (ri == ci).astype(F32)

        ch = []
        for sub in range(DN_SUB):
            s = i * DN_SUB + sub
            chunk_of = (jnp.where(s < N_CTX_CHUNKS, N_LAT_CHUNKS + s, s - N_CTX_CHUNKS), N_DN_STEPS - 1 - s)
            for d in range(2):
                r0 = pl.multiple_of(chunk_of[d] * C, C)
                zba = zba_ref[0, pl.ds(r0, C), :]
                beta_all = jax.nn.sigmoid(zba)
                g_all = neg_decay_rate * _softplus(zba + dt_bias)
                tri = (ti >= tj) if d == 0 else (ti <= tj)
                gc_all = jnp.dot(tri.astype(F32), g_all, precision=HI, preferred_element_type=F32)
                gc_t = jnp.concatenate([gc_all, gc_all], axis=0).T
                last = C - 1 if d == 0 else 0
                for p in range(DN_PAIRS):
                    lb = (d * DN_HEADS + 2 * p, d * DN_HEADS + 2 * p + 1)
                    la = (2 * DN_HEADS + lb[0], 2 * DN_HEADS + lb[1])
                    col = lambda t, l: t[:, l:l + 1]
                    q2 = qkv_ref[0, pl.ds(r0, C), p * LANES:(p + 1) * LANES].astype(F32)
                    k2 = qkv_ref[0, pl.ds(r0, C), DN_WIDTH + p * LANES:DN_WIDTH + (p + 1) * LANES].astype(F32)
                    v2 = qkv_ref[0, pl.ds(r0, C), 2 * DN_WIDTH + p * LANES:2 * DN_WIDTH + (p + 1) * LANES].astype(F32)
                    beta2 = jnp.where(lo, col(beta_all, lb[0]), col(beta_all, lb[1]))
                    gc2 = jnp.where(lo, col(gc_all, la[0]), col(gc_all, la[1]))
                    gcol = jnp.concatenate([col(gc_all, la[0]), col(gc_all, la[1])], axis=0)
                    grow = jnp.where(lo, gc_t[la[0]:la[0] + 1, :], gc_t[la[1]:la[1] + 1, :])
                    g_last = jnp.where(lo, gc_all[last:last + 1, la[0]:la[0] + 1],
                                       gc_all[last:last + 1, la[1]:la[1] + 1])
                    e_diff = jnp.exp(gcol - grow)
                    order = (ri >= ci) if d == 0 else (ri <= ci)
                    kb2 = k2 * beta2
                    e_gc = jnp.exp(gc2)
                    ch.append(dict(
                        r0=r0, sub=sub, d=d, p=p, g_last=g_last,
                        dec_incl=jnp.where(same_head & order, e_diff, 0.0),
                        dec_strict=jnp.where(same_head & order & (ri != ci), e_diff, 0.0),
                        kq=jnp.concatenate([stack(kb2), stack(q2)], axis=0).astype(BF16),
                        k=stack(k2).astype(BF16),
                        rhs=jnp.concatenate([stack(v2 * beta2), stack(kb2 * e_gc)], axis=1).astype(BF16),
                        kd=stack(k2 * jnp.exp(g_last - gc2)).astype(BF16),
                        qe=stack(q2 * e_gc)))
        for c in ch:
            kk_qk = lax.dot_general(c["kq"], c["k"], (((1,), (1,)), ((), ())), preferred_element_type=F32)
            c["a"] = kk_qk[:P2] * c["dec_strict"]
            c["attn"] = (kk_qk[P2:] * c["dec_incl"]).astype(BF16)
            c["t"] = eye - jnp.where(same(1), c["a"], 0.0)
        for lvl in range(1, LOG2_CHUNK):
            joins = same(lvl + 1) & ~same(lvl)
            for c in ch:
                c["m"] = _bdot(jnp.where(joins, c["a"], 0.0), c["t"])
            for c in ch:
                c["t"] = c["t"] - _bdot(c["t"], c["m"])
        for c in ch:
            c["uw"] = _bdot(c["t"], c["rhs"]).astype(BF16)
        for c in ch:
            oa = jnp.dot(c["attn"], c["uw"], preferred_element_type=F32)
            c["o0"] = oa[:, :LANES]
            c["qp"] = c["qe"] - oa[:, LANES:]
            c["np"] = lax.dot_general(c["kd"], c["uw"], (((0,), (0,)), ((), ())), preferred_element_type=F32)

        states = [s_scr[j] for j in range(2 * DN_PAIRS)]
        for sub in range(DN_SUB):
            cur = [c for c in ch if c["sub"] == sub]
            for c in cur:
                j = c["d"] * DN_PAIRS + c["p"]
                c["r"] = _bdot(jnp.concatenate([c["qp"], c["np"][:, LANES:]], axis=0), states[j])
            for d in range(2):
                outs = []
                for c in cur:
                    if c["d"] != d:
                        continue
                    j = d * DN_PAIRS + c["p"]
                    o_st = c["o0"] + c["r"][:P2]
                    outs.append(o_st[:C] + o_st[C:])
                    states[j] = states[j] * jnp.exp(c["g_last"]) + c["np"][:, :LANES] - c["r"][P2:]
                    r0 = c["r0"]
                o_ref[0, pl.ds(r0, C), :] += jnp.concatenate(outs, axis=1)
        for j in range(2 * DN_PAIRS):
            s_scr[j] = states[j]
        return carry

    lax.fori_loop(0, N_DN_STEPS // DN_SUB, step, 0)


def _deltanet(qkv, zba, a_log, dt_bias):
    pad = lambda t: jnp.zeros((1, LANES), F32).at[0, 2 * DN_HEADS:4 * DN_HEADS].set(t.reshape(-1))
    return pl.pallas_call(
        _dn_kernel,
        grid=(BATCH,),
        in_specs=[pl.BlockSpec((1, TT, 3 * DN_WIDTH), lambda b: (b, 0, 0)),
                  pl.BlockSpec((1, TT, LANES), lambda b: (b, 0, 0)),
                  pl.BlockSpec((1, LANES), lambda b: (0, 0)),
                  pl.BlockSpec((1, LANES), lambda b: (0, 0))],
        out_specs=pl.BlockSpec((1, TT, DN_WIDTH), lambda b: (b, 0, 0)),
        out_shape=jax.ShapeDtypeStruct((BATCH, TT, DN_WIDTH), F32),
        scratch_shapes=[pltpu.VMEM((2 * DN_PAIRS, LANES, LANES), F32)],
        compiler_params=_cparams("parallel"),
        name="deltanet",
    )(qkv, zba, pad(a_log), pad(dt_bias))


def _rope_tables():
    rows = SEQ // GRID_W
    row = jnp.repeat(jnp.arange(rows, dtype=F32), GRID_W)
    col = jnp.tile(jnp.arange(GRID_W, dtype=F32), rows)
    inv = ROPE_THETA ** (-2.0 * jnp.arange(ROPE_PAIRS, dtype=F32) / ROPE_AXIS_DIM)
    ang = jnp.stack([row[:, None] * inv, col[:, None] * inv], axis=1)
    cos, sin = jnp.cos(ang), jnp.sin(ang)
    c = jnp.concatenate([cos[:, 0], cos[:, 0], cos[:, 1], cos[:, 1]], axis=-1)
    s = jnp.concatenate([-sin[:, 0], sin[:, 0], -sin[:, 1], sin[:, 1]], axis=-1)
    c = jnp.concatenate([c, jnp.ones((CTX_LEN, HEAD_DIM), F32)], axis=0)
    s = jnp.concatenate([s, jnp.zeros((CTX_LEN, HEAD_DIM), F32)], axis=0)
    return jnp.tile(c, (1, 2)), jnp.tile(s, (1, 2))


ATTN_TQ = 128
ATTN_NQ = 8
ATTN_KB = 512
ATTN_KEY_BLOCKS = tuple((j * ATTN_KB, ATTN_KB) for j in range(SEQ // ATTN_KB)) + ((SEQ, CTX_LEN),)


def _dot_nt(a, b):
    return lax.dot_general(a, b, (((1,), (1,)), ((), ())), preferred_element_type=F32)


def _attn_finish(acc):
    return acc[:, :HEAD_DIM] / acc[:, HEAD_DIM:HEAD_DIM + 1]


def _attn_lat_kernel(q_ref, k_ref, v_ref, *rest, n_cast):
    cast_in, o_ref, cast_out = rest[:n_cast], rest[n_cast], rest[n_cast + 1:2 * n_cast + 1]
    s0_scr, s1_scr = rest[2 * n_cast + 1:]
    for src, dst in zip(cast_in, cast_out):
        dst[...] = src[...].astype(dst.dtype)
    rows = ATTN_GROUP * ATTN_TQ
    s_scr = (s0_scr, s1_scr)
    mpart = [None] * ATTN_NQ
    acc = [None] * ATTN_NQ
    for stage in range(ATTN_NQ + 1):
        a, b = stage, stage - 1
        if a < ATTN_NQ:
            qa = q_ref[0, :, a * ATTN_TQ:(a + 1) * ATTN_TQ, :].reshape(rows, HEAD_DIM)
        if b >= 0:
            m_b = jnp.max(mpart[b], axis=-1, keepdims=True)
        for k0, kn in ATTN_KEY_BLOCKS:
            if a < ATTN_NQ:
                s = _dot_nt(qa, k_ref[0, 0, k0:k0 + kn, :])
                s_scr[a % 2][:, k0:k0 + kn] = s
                blk = functools.reduce(jnp.maximum, [s[:, i:i + LANES] for i in range(0, kn, LANES)])
                mpart[a] = blk if mpart[a] is None else jnp.maximum(mpart[a], blk)
            if b >= 0:
                p = jnp.exp2(s_scr[b % 2][:, k0:k0 + kn] - m_b).astype(BF16)
                pv = jnp.dot(p, v_ref[0, 0, k0:k0 + kn, :], preferred_element_type=F32)
                acc[b] = pv if acc[b] is None else acc[b] + pv
        if b >= 0:
            o = _attn_finish(acc[b]).reshape(ATTN_GROUP, ATTN_TQ, HEAD_DIM)
            o_ref[0, :, b * ATTN_TQ:(b + 1) * ATTN_TQ, :] = o.astype(o_ref.dtype)


def _attn_ctx_kernel(q_ref, k_ref, v_ref, o_ref):
    q = q_ref[0].reshape(ATTN_GROUP * CTX_LEN, HEAD_DIM)
    s = _dot_nt(q, k_ref[0, 0])
    p = jnp.exp2(s - jnp.max(s, axis=-1, keepdims=True)).astype(BF16)
    o = _attn_finish(jnp.dot(p, v_ref[0, 0], preferred_element_type=F32))
    o_ref[0] = o.reshape(ATTN_GROUP, CTX_LEN, HEAD_DIM).astype(o_ref.dtype)


def _attention_lat(qh, kh, vh, cast=()):
    tq = ATTN_NQ * ATTN_TQ
    rows = ATTN_GROUP * ATTN_TQ
    nq = SEQ // tq
    n_steps = BATCH * ATTN_KV_HEADS * nq
    step = lambda b, g, i: ((b * ATTN_KV_HEADS + g) * nq + i, 0)
    cast_specs = [pl.BlockSpec((w.shape[0] // n_steps, w.shape[1]), step) for w in cast]
    out = pl.pallas_call(
        functools.partial(_attn_lat_kernel, n_cast=len(cast)),
        grid=(BATCH, ATTN_KV_HEADS, nq),
        in_specs=[pl.BlockSpec((1, ATTN_GROUP, tq, HEAD_DIM), lambda b, g, i: (b, g, i, 0)),
                  pl.BlockSpec((1, 1, TT, HEAD_DIM), lambda b, g, i: (b, g, 0, 0)),
                  pl.BlockSpec((1, 1, TT, LANES), lambda b, g, i: (b, g, 0, 0))] + cast_specs,
        out_specs=[pl.BlockSpec((1, ATTN_GROUP, tq, HEAD_DIM), lambda b, g, i: (b, g, i, 0))] + cast_specs,
        out_shape=[jax.ShapeDtypeStruct((BATCH, ATTN_HEADS, SEQ, HEAD_DIM), BF16)]
        + [jax.ShapeDtypeStruct(w.shape, BF16) for w in cast],
        scratch_shapes=[pltpu.VMEM((rows, TT), F32), pltpu.VMEM((rows, TT), F32)],
        compiler_params=_cparams("arbitrary", "arbitrary", "arbitrary"),
        name="attention_lat",
    )(qh, kh, vh, *cast)
    return out[0], tuple(out[1:])


def _attention_ctx(qh, kh, vh):
    ctx_blk = SEQ // CTX_LEN
    return pl.pallas_call(
        _attn_ctx_kernel,
        grid=(BATCH, ATTN_KV_HEADS),
        in_specs=[pl.BlockSpec((1, ATTN_GROUP, CTX_LEN, HEAD_DIM), lambda b, g: (b, g, ctx_blk, 0)),
                  pl.BlockSpec((1, 1, CTX_LEN, HEAD_DIM), lambda b, g: (b, g, ctx_blk, 0)),
                  pl.BlockSpec((1, 1, CTX_LEN, LANES), lambda b, g: (b, g, ctx_blk, 0))],
        out_specs=pl.BlockSpec((1, ATTN_GROUP, CTX_LEN, HEAD_DIM), lambda b, g: (b, g, 0, 0)),
        out_shape=jax.ShapeDtypeStruct((BATCH, ATTN_HEADS, CTX_LEN, HEAD_DIM), BF16),
        compiler_params=_cparams("parallel", "arbitrary"),
        name="attention_ctx",
    )(qh, kh, vh)


def _mix_kernel(*refs, moe, split):
    n = 2 if split else 1
    x_refs, refs = refs[:n], refs[n:]
    mod_ref, o_dn_ref, zgate_ref = refs[:3]
    attn_refs, refs = refs[3:3 + n], refs[3 + n:]
    zu_ref, zv_ref, dn_g_ref, sgu_g_ref, sgu_w_ref, sgu_b_ref, w_out_ref, g2_ref = refs[:8]
    rest = refs[8:]
    if moe:
        rw_ref, rb_ref, ox, oh, ologit = rest[:5]
    else:
        ox, oh = rest[:2]
    ones = _seg_ones()
    m = mod_ref[0]
    o = o_dn_ref[0]
    dn = o * lax.rsqrt(_seg_sum(o * o, ones) * (1.0 / HEAD_DIM) + NORM_EPS) * dn_g_ref[...]
    dn = dn * jax.nn.silu(zgate_ref[0].astype(F32))
    heads = lambda ref: jnp.concatenate([ref[0, h] for h in range(ATTN_HEADS)], axis=1)
    if split:
        x_scr, at_scr = rest[-2:]
        is_ctx = pl.program_id(1) == CTX_TILE

        @pl.when(jnp.logical_not(is_ctx))
        def _():
            at_scr[...] = heads(attn_refs[0])

        @pl.when(is_ctx)
        def _():
            at_scr[...] = heads(attn_refs[1])

        at = at_scr[...]
        x_in = _tile_rows(x_refs, x_scr)
    else:
        at = heads(attn_refs[0])
        x_in = x_refs[0][0]
    u = jax.nn.gelu(zu_ref[0].astype(F32))
    v = jax.nn.gelu(zv_ref[0].astype(F32))
    v = (v * lax.rsqrt(_seg_sum(v * v, ones) * (1.0 / HEAD_DIM) + NORM_EPS) * sgu_g_ref[...]).astype(BF16)
    gd = MLP_WIDTH // MLP_GROUPS
    chunks = []
    for ci in range(TM // MLP_CHUNK):
        vc = v[ci * MLP_CHUNK:(ci + 1) * MLP_CHUNK]
        mixed = jnp.concatenate(
            [jnp.dot(sgu_w_ref[g], vc[:, g * gd:(g + 1) * gd], preferred_element_type=F32)
             for g in range(MLP_GROUPS)], axis=1)
        chunks.append(mixed + sgu_b_ref[...])
    sg = u * jnp.concatenate(chunks, axis=0)
    mixed_all = jnp.concatenate([dn.astype(BF16), at, sg.astype(BF16)], axis=1)
    y = jnp.dot(mixed_all, w_out_ref[...], preferred_element_type=F32)
    x = x_in + m[2:3] * y
    ox[0] = x
    h = _ada_norm(x, g2_ref[...], m[3:4], m[4:5])
    if moe:
        for kk in range(D_MODEL // LANES):
            oh[0, pl.ds(kk, TM, stride=SLAB), :] = h[:, kk * LANES:(kk + 1) * LANES]
    else:
        oh[0] = h.astype(oh.dtype)
    if moe:
        h_hi, h_lo = _split_bf16(h)
        r_hi, r_lo = _split_bf16(rw_ref[...])
        ologit[0] = (jnp.dot(h_hi, r_hi, preferred_element_type=F32) + jnp.dot(h_hi, r_lo, preferred_element_type=F32)
                     + jnp.dot(h_lo, r_hi, preferred_element_type=F32)) + rb_ref[...]


def _mix(xs, mod_l, o_dn, zgate, attns, zu, zv2, dn_g, sgu_g, sgu_w, sgu_b, w_out, gain2, router=None):
    moe = router is not None
    split = len(xs) == 2
    assert len(attns) == len(xs)
    nt = N_TILES - 1 if moe else N_TILES
    rows = nt * TM
    row = lambda b, i: (b, i, 0)
    orow = row
    const2 = lambda b, i: (0, 0)
    gd = MLP_WIDTH // MLP_GROUPS
    sgu_b_x = jnp.repeat(sgu_b.T, gd, axis=1)
    if split:
        attn_specs = [pl.BlockSpec((1, ATTN_HEADS, TM, HEAD_DIM), lambda b, i: (b, 0, jnp.minimum(i, CTX_TILE - 1), 0)),
                      pl.BlockSpec((1, ATTN_HEADS, CTX_LEN, HEAD_DIM), lambda b, i: (b, 0, 0, 0))]
    else:
        attn_specs = [pl.BlockSpec((1, ATTN_HEADS, TM, HEAD_DIM), lambda b, i: (b, 0, i, 0))]
    in_specs = _row_specs(split) + [
                pl.BlockSpec((1, 6, D_MODEL), _mod_index),
                pl.BlockSpec((1, TM, DN_WIDTH), row),
                pl.BlockSpec((1, TM, DN_WIDTH), row)] + attn_specs + [
                pl.BlockSpec((1, TM, MLP_WIDTH), row),
                pl.BlockSpec((1, TM, MLP_WIDTH), row),
                pl.BlockSpec((1, DN_WIDTH), const2),
                pl.BlockSpec((1, MLP_WIDTH), const2),
                pl.BlockSpec((MLP_GROUPS, MLP_CHUNK, MLP_CHUNK), lambda b, i: (0, 0, 0)),
                pl.BlockSpec((MLP_CHUNK, MLP_WIDTH), const2),
                pl.BlockSpec((D_MIX, D_MODEL), const2),
                pl.BlockSpec((1, D_MODEL), const2)]
    args = [*xs, mod_l, o_dn, zgate, *attns, zu, zv2, jnp.tile(dn_g.reshape(1, HEAD_DIM), (1, DN_HEADS)),
            sgu_g.reshape(1, MLP_WIDTH), sgu_w.astype(BF16), sgu_b_x, w_out, gain2.reshape(1, D_MODEL)]
    if moe:
        h_spec = pl.BlockSpec((1, TM * SLAB, LANES), orow)
        h_shape = jax.ShapeDtypeStruct((BATCH, rows * SLAB, LANES), F32)
    else:
        h_spec = pl.BlockSpec((1, TM, D_MODEL), orow)
        h_shape = jax.ShapeDtypeStruct((BATCH, rows, D_MODEL), BF16)
    out_specs = [pl.BlockSpec((1, TM, D_MODEL), orow), h_spec]
    out_shape = [jax.ShapeDtypeStruct((BATCH, rows, D_MODEL), F32), h_shape]
    if moe:
        rw, rb = router
        in_specs += [pl.BlockSpec((D_MODEL, LANES), const2), pl.BlockSpec((1, LANES), const2)]
        args += [jnp.pad(rw, ((0, 0), (0, LANES - MOE_EXPERTS))),
                 jnp.pad(rb.reshape(1, MOE_EXPERTS), ((0, 0), (0, LANES - MOE_EXPERTS)))]
        out_specs.append(pl.BlockSpec((1, TM, LANES), orow))
        out_shape.append(jax.ShapeDtypeStruct((BATCH, rows, LANES), F32))
    return pl.pallas_call(
        functools.partial(_mix_kernel, moe=moe, split=split),
        grid=(BATCH, nt),
        in_specs=in_specs, out_specs=out_specs, out_shape=out_shape,
        scratch_shapes=[pltpu.VMEM((TM, D_MODEL), F32), pltpu.VMEM((TM, ATTN_WIDTH), BF16)] if split else [],
        compiler_params=_cparams("parallel", "arbitrary"),
        name="mix_moe" if moe else "mix",
    )(*args)


FFN_SPLIT = 2
FFN_BLK = D_FF // FFN_SPLIT


def _ffn_kernel(x_ref, h_ref, mod_ref, w1_ref, w3_ref, w2_ref, o_ref):
    h = h_ref[0]
    y = None
    for c in range(FFN_SPLIT):
        sl = slice(c * FFN_BLK, (c + 1) * FFN_BLK)
        a = jnp.dot(h, w1_ref[:, sl], preferred_element_type=F32)
        b = jnp.dot(h, w3_ref[:, sl], preferred_element_type=F32)
        part = jnp.dot((jax.nn.silu(a) * b).astype(BF16), w2_ref[sl, :], preferred_element_type=F32)
        y = part if y is None else y + part
    o_ref[0] = x_ref[0] + mod_ref[0][5:6] * y


def _ffn(x_all, h_all, mod_l, w1, w3, w2):
    row = lambda b, i: (b, i, 0)
    const2 = lambda b, i: (0, 0)
    return pl.pallas_call(
        _ffn_kernel,
        grid=(BATCH, N_TILES),
        in_specs=[pl.BlockSpec((1, TM, D_MODEL), row),
                  pl.BlockSpec((1, TM, D_MODEL), row),
                  pl.BlockSpec((1, 6, D_MODEL), _mod_index),
                  pl.BlockSpec((D_MODEL, D_FF), const2),
                  pl.BlockSpec((D_MODEL, D_FF), const2),
                  pl.BlockSpec((D_FF, D_MODEL), const2)],
        out_specs=pl.BlockSpec((1, TM, D_MODEL), row),
        out_shape=jax.ShapeDtypeStruct((BATCH, TT, D_MODEL), F32),
        compiler_params=_cparams("parallel", "arbitrary"),
        name="ffn",
    )(x_all, h_all, mod_l, w1, w3, w2)


def _route_kernel(logit_ref, dest_ref, gate_ref, count_ref, run_scr):
    phase, t = pl.program_id(0), pl.program_id(1)

    @pl.when((phase == 0) & (t == 0))
    def _():
        run_scr[...] = jnp.zeros_like(run_scr)

    lane = lax.broadcasted_iota(jnp.int32, (ROUTE_TILE, LANES), 1).astype(F32)
    logits = jnp.where(lane < MOE_EXPERTS, logit_ref[...], -jnp.inf)
    m1 = jnp.max(logits, axis=-1, keepdims=True)
    e1 = jnp.min(jnp.where(logits == m1, lane, float(LANES)), axis=-1, keepdims=True)
    rest = jnp.where(lane == e1, -jnp.inf, logits)
    m2 = jnp.max(rest, axis=-1, keepdims=True)
    e2 = jnp.min(jnp.where(rest == m2, lane, float(LANES)), axis=-1, keepdims=True)
    hot1 = (lane == e1).astype(F32)
    hot2 = (lane == e2).astype(F32)
    hot = hot1 + hot2
    tile_count = jnp.sum(hot, axis=0, keepdims=True)

    @pl.when(phase == 0)
    def _():
        run_scr[0:1] = run_scr[0:1] + tile_count

    @pl.when((phase == 1) & (t == 0))
    def _():
        counts = jnp.broadcast_to(run_scr[0:1], (8, LANES))
        count_ref[...] = counts.astype(jnp.int32)
        padded = jnp.ceil(counts * (1.0 / MOE_BLOCK)) * MOE_BLOCK
        ei = lax.broadcasted_iota(jnp.int32, (LANES, LANES), 0)
        ej = lax.broadcasted_iota(jnp.int32, (LANES, LANES), 1)
        before = (ei < ej).astype(F32)
        run_scr[1:2] = jnp.dot(padded, before, precision=HI, preferred_element_type=F32)[0:1]

    @pl.when(phase == 1)
    def _():
        ri = lax.broadcasted_iota(jnp.int32, (ROUTE_TILE, ROUTE_TILE), 0)
        rj = lax.broadcasted_iota(jnp.int32, (ROUTE_TILE, ROUTE_TILE), 1)
        earlier = (ri > rj).astype(BF16)
        within = jnp.dot(earlier, hot.astype(BF16), preferred_element_type=F32)
        pos = within + run_scr[1:2]
        d1 = jnp.sum(pos * hot1, axis=-1, keepdims=True)
        d2 = jnp.sum(pos * hot2, axis=-1, keepdims=True)
        dest_ref[...] = jnp.concatenate([d1, d2], axis=1).astype(jnp.int32)
        w2 = jnp.exp(m2 - m1)
        gate_ref[...] = jnp.concatenate([1.0 / (1.0 + w2), w2 / (1.0 + w2)], axis=1)
        run_scr[1:2] = run_scr[1:2] + tile_count


def _route(logits):
    nt = N_LAT // ROUTE_TILE
    return pl.pallas_call(
        _route_kernel,
        grid=(2, nt),
        in_specs=[pl.BlockSpec((ROUTE_TILE, LANES), lambda p, t: (t, 0))],
        out_specs=[pl.BlockSpec((ROUTE_TILE, MOE_TOP_K), lambda p, t: (t * p, 0)),
                   pl.BlockSpec((ROUTE_TILE, MOE_TOP_K), lambda p, t: (t * p, 0)),
                   pl.BlockSpec((8, LANES), lambda p, t: (0, 0))],
        out_shape=[jax.ShapeDtypeStruct((N_LAT, MOE_TOP_K), jnp.int32),
                   jax.ShapeDtypeStruct((N_LAT, MOE_TOP_K), F32),
                   jax.ShapeDtypeStruct((8, LANES), jnp.int32)],
        scratch_shapes=[pltpu.VMEM((8, LANES), F32)],
        compiler_params=_cparams("arbitrary", "arbitrary"),
        name="moe_route",
    )(logits)


N_PAD_SLOTS = MOE_ROWS - N_ASSIGN
Y2_ROWS = N_LAT + N_PAD_SLOTS // MOE_TOP_K


def _invert_kernel(dest_ref, bounds_ref, slot_ref):
    def real(a, c):
        slot_ref[dest_ref[a]] = a
        return c

    lax.fori_loop(0, N_ASSIGN, real, 0, unroll=8)

    def pad_range(e, count):
        def pad(s, cnt):
            slot_ref[s] = N_ASSIGN + cnt
            return cnt + 1

        return lax.fori_loop(bounds_ref[2 * e], bounds_ref[2 * e + 1], pad, count)

    lax.fori_loop(0, MOE_EXPERTS + 1, pad_range, 0)


def _invert(dest_flat, pad_bounds):
    return pl.pallas_call(
        _invert_kernel,
        grid_spec=pltpu.PrefetchScalarGridSpec(
            num_scalar_prefetch=2, grid=(1,), in_specs=[],
            out_specs=pl.BlockSpec(memory_space=pltpu.SMEM)),
        out_shape=jax.ShapeDtypeStruct((MOE_ROWS,), jnp.int32),
        compiler_params=_cparams("arbitrary"),
        name="moe_invert",
    )(dest_flat, pad_bounds)


LAST_BLOCK = MOE_N_BLOCKS - 1
MOE_NBUF = 3


def _swiglu_half(x, w1_ref, w3_ref, w2_ref):
    a = jnp.dot(x, w1_ref[0], preferred_element_type=F32)
    b = jnp.dot(x, w3_ref[0], preferred_element_type=F32)
    return jnp.dot((jax.nn.silu(a) * b).astype(BF16), w2_ref[0], preferred_element_type=F32)


def _expert_gather_kernel(be_ref, slot_ref, h_hbm, w1_ref, w3_ref, w2_ref, xs_ref, yb_ref, xbuf, sem):
    del be_ref
    j = pl.program_id(0)
    cur = j % MOE_NBUF

    def gather(blk, buf):
        base = blk * MOE_BLOCK
        for r in range(MOE_BLOCK):
            row = pl.multiple_of(slot_ref[base + r], SLAB)
            pltpu.make_async_copy(h_hbm.at[pl.ds(row, SLAB)], xbuf.at[buf, pl.ds(r * SLAB, SLAB)],
                                  sem.at[buf]).start()

    def wait(buf):
        pltpu.make_async_copy(h_hbm.at[pl.ds(0, MOE_BLOCK * SLAB)], xbuf.at[buf], sem.at[buf]).wait()

    @pl.when(j == 0)
    def _():
        gather(0, 0)
        gather(1, 1)

    wait(cur)
    for kk in range(D_MODEL // LANES):
        xs_ref[:, kk * LANES:(kk + 1) * LANES] = xbuf[cur, pl.ds(kk, MOE_BLOCK, stride=SLAB), :]
    gather(jnp.minimum(j + 2, LAST_BLOCK), (j + 2) % MOE_NBUF)
    yb_ref[...] = _swiglu_half(xs_ref[...].astype(BF16), w1_ref, w3_ref, w2_ref)

    @pl.when(j == LAST_BLOCK)
    def _():
        wait((j + 1) % MOE_NBUF)
        wait((j + 2) % MOE_NBUF)


def _expert_scatter_kernel(be_ref, slot_ref, xs_ref, w1_ref, w3_ref, w2_ref, yb_ref, y2_hbm, obuf, sem):
    del be_ref
    j = pl.program_id(0)
    cur = j % MOE_NBUF
    prev = (j + MOE_NBUF - 1) % MOE_NBUF

    def scatter(blk, buf):
        base = blk * MOE_BLOCK
        for r in range(MOE_BLOCK):
            row = pl.multiple_of(slot_ref[base + r], SLAB)
            pltpu.make_async_copy(obuf.at[buf, pl.ds(r * SLAB, SLAB)], y2_hbm.at[pl.ds(row, SLAB)],
                                  sem.at[buf]).start()

    def wait(buf):
        pltpu.make_async_copy(obuf.at[buf], y2_hbm.at[pl.ds(0, MOE_BLOCK * SLAB)], sem.at[buf]).wait()

    def compute():
        out = yb_ref[...] + _swiglu_half(xs_ref[...].astype(BF16), w1_ref, w3_ref, w2_ref)
        for kk in range(D_MODEL // LANES):
            obuf[cur, pl.ds(kk, MOE_BLOCK, stride=SLAB), :] = out[:, kk * LANES:(kk + 1) * LANES]

    @pl.when(j >= MOE_NBUF)
    def _():
        wait(cur)

    @pl.when(j == 0)
    def _():
        compute()

    @pl.when(j > 0)
    def _():
        scatter(j - 1, prev)
        compute()

    @pl.when(j == LAST_BLOCK)
    def _():
        scatter(j, cur)
        for b in range(MOE_NBUF):
            wait(b)


def _experts(block_expert, gather_rows, scatter_rows, h_lat, w1, w3, w2):
    def w_specs(c):
        return [pl.BlockSpec((1, D_MODEL, MOE_FF_BLK), lambda j, be, sl: (be[j], 0, c)),
                pl.BlockSpec((1, D_MODEL, MOE_FF_BLK), lambda j, be, sl: (be[j], 0, c)),
                pl.BlockSpec((1, MOE_FF_BLK, D_MODEL), lambda j, be, sl: (be[j], c, 0))]

    blk = pl.BlockSpec((MOE_BLOCK, D_MODEL), lambda j, be, sl: (j, 0))
    xs, yb = pl.pallas_call(
        _expert_gather_kernel,
        grid_spec=pltpu.PrefetchScalarGridSpec(
            num_scalar_prefetch=2, grid=(MOE_N_BLOCKS,),
            in_specs=[pl.BlockSpec(memory_space=pl.ANY)] + w_specs(0),
            out_specs=[blk, blk],
            scratch_shapes=[pltpu.VMEM((MOE_NBUF, MOE_BLOCK * SLAB, LANES), F32),
                            pltpu.SemaphoreType.DMA((MOE_NBUF,))]),
        out_shape=[jax.ShapeDtypeStruct((MOE_ROWS, D_MODEL), F32)] * 2,
        compiler_params=_cparams("arbitrary"),
        name="moe_experts_gather",
    )(block_expert, gather_rows, h_lat, w1, w3, w2)
    return pl.pallas_call(
        _expert_scatter_kernel,
        grid_spec=pltpu.PrefetchScalarGridSpec(
            num_scalar_prefetch=2, grid=(MOE_N_BLOCKS,),
            in_specs=[blk] + w_specs(1) + [blk],
            out_specs=pl.BlockSpec(memory_space=pl.ANY),
            scratch_shapes=[pltpu.VMEM((MOE_NBUF, MOE_BLOCK * SLAB, LANES), F32),
                            pltpu.SemaphoreType.DMA((MOE_NBUF,))]),
        out_shape=jax.ShapeDtypeStruct((MOE_TOP_K * Y2_ROWS * SLAB, LANES), F32),
        compiler_params=_cparams("arbitrary"),
        name="moe_experts_scatter",
    )(block_expert, scatter_rows, xs, w1, w3, w2, yb)


def _combine_kernel(x_ref, gate_ref, mod_ref, g_ref, ya_ref, yb_ref, o_ref):
    gates = gate_ref[0]
    rows = lambda ref: jnp.concatenate(
        [ref[pl.ds(kk, TM, stride=SLAB), :] for kk in range(D_MODEL // LANES)], axis=1)
    y = rows(ya_ref) * gates[:, 0:1] + rows(yb_ref) * gates[:, 1:2]
    x = x_ref[0] + mod_ref[0][5:6] * y
    o_ref[0] = x * lax.rsqrt(jnp.mean(x * x, axis=-1, keepdims=True) + NORM_EPS) * g_ref[...]


def _combine(x_lat, gates, mod_l, final_g, y2):
    nt = SEQ // TM
    return pl.pallas_call(
        _combine_kernel,
        grid=(BATCH, nt),
        in_specs=[pl.BlockSpec((1, TM, D_MODEL), lambda b, i: (b, i, 0)),
                  pl.BlockSpec((1, TM, MOE_TOP_K), lambda b, i: (b, i, 0)),
                  pl.BlockSpec((1, 6, D_MODEL), lambda b, i: (b, 0, 0)),
                  pl.BlockSpec((1, D_MODEL), lambda b, i: (0, 0)),
                  pl.BlockSpec((TM * SLAB, LANES), lambda b, i: (b * nt + i, 0)),
                  pl.BlockSpec((TM * SLAB, LANES), lambda b, i: (Y2_ROWS // TM + b * nt + i, 0))],
        out_specs=pl.BlockSpec((1, TM, D_MODEL), lambda b, i: (b, i, 0)),
        out_shape=jax.ShapeDtypeStruct((BATCH, SEQ, D_MODEL), F32),
        compiler_params=_cparams("parallel", "arbitrary"),
        name="moe_combine",
    )(x_lat, gates.reshape(BATCH, SEQ, MOE_TOP_K), mod_l, final_g.reshape(1, D_MODEL), y2, y2)


def _reorder_w_in(w):
    s = np.cumsum((3 * DN_WIDTH, DN_WIDTH, 2 * DN_HEADS, 2 * DN_HEADS, ATTN_WIDTH, ATTN_KV_WIDTH, ATTN_KV_WIDTH,
                   MLP_WIDTH, MLP_WIDTH)).tolist()
    ba = w[:, s[1]:s[3]]
    return jnp.concatenate([w[:, :s[1]], w[:, s[3]:], ba,
                            jnp.zeros((D_MODEL, LANES - 4 * DN_HEADS), w.dtype)], axis=1).astype(BF16)


def kernel(x, c, ctx, c_ctx, mod_w, mod_b, norm1_g, norm2_g, w_in, conv_w, dn_a_log, dn_dt_bias, dn_norm_g,
           q_norm_g, k_norm_g, sgu_norm_g, sgu_w, sgu_b, w_out, ffn_w1, ffn_w3, ffn_w2, router_w, router_b,
           moe_w1, moe_w3, moe_w2, final_norm_g):
    assert DEPTH == 2 and x.shape == (BATCH, SEQ, D_MODEL) and ctx.shape == (BATCH, CTX_LEN, D_MODEL)
    cond8 = jnp.concatenate([c, c_ctx[None], jnp.zeros((8 - BATCH - 1, D_MODEL), F32)], axis=0)
    mod = _modulation(cond8, mod_w, mod_b)
    rope_c, rope_s = _rope_tables()
    xs = (x, ctx)
    for layer in range(DEPTH):
        last = layer == DEPTH - 1
        zqkv, zgate, zu, zv2, zba, qh, kh, vh = _in_proj(xs, mod[layer], norm1_g[layer], _reorder_w_in(w_in[layer]),
                                                         q_norm_g[layer], k_norm_g[layer], rope_c, rope_s)
        qkv = _dn_prep(zqkv, conv_w[layer])
        o_dn = _deltanet(qkv, zba, dn_a_log[layer], dn_dt_bias[layer])
        if not last:
            i = (layer + 1) // 2
            attn_lat, moe_w = _attention_lat(qh, kh, vh, cast=(
                moe_w1[i].reshape(MOE_EXPERTS * D_MODEL, MOE_D_FF), moe_w3[i].reshape(MOE_EXPERTS * D_MODEL, MOE_D_FF),
                moe_w2[i].reshape(MOE_EXPERTS * MOE_D_FF, D_MODEL)))
        else:
            attn_lat, _ = _attention_lat(qh, kh, vh)
        attns = (attn_lat,)
        if len(xs) == 2:
            attns += (_attention_ctx(qh, kh, vh),)
        elif not last:
            attns = (jnp.concatenate([attns[0], _attention_ctx(qh, kh, vh)], axis=2),)
        mix_args = (xs, mod[layer], o_dn, zgate, attns, zu, zv2, dn_norm_g[layer], sgu_norm_g[layer],
                    sgu_w[layer], sgu_b[layer], w_out[layer].astype(BF16), norm2_g[layer])
        if not last:
            i = layer // 2
            x_mid, h_mid = _mix(*mix_args)
            xs = (_ffn(x_mid, h_mid, mod[layer], ffn_w1[i].astype(BF16), ffn_w3[i].astype(BF16),
                       ffn_w2[i].astype(BF16)),)
        else:
            i = layer // 2
            x_lat, h_lat, logits = _mix(*mix_args, router=(router_w[i], router_b[i]))
            dest, gates, counts = _route(logits.reshape(N_LAT, LANES))
            cnt = counts[0, :MOE_EXPERTS]
            padded = (cnt + MOE_BLOCK - 1) // MOE_BLOCK * MOE_BLOCK
            pad_ends = jnp.cumsum(padded)
            block_expert = jnp.minimum(
                jnp.sum(pad_ends[None, :] <= (jnp.arange(MOE_N_BLOCKS) * MOE_BLOCK)[:, None], axis=1),
                MOE_EXPERTS - 1).astype(jnp.int32)
            lo = jnp.concatenate([pad_ends - padded + cnt, pad_ends[-1:]])
            hi = jnp.concatenate([pad_ends, jnp.full((1,), MOE_ROWS, pad_ends.dtype)])
            pad_bounds = jnp.stack([lo, hi], axis=1).reshape(-1).astype(jnp.int32)
            slot_src = _invert(dest.reshape(N_ASSIGN), pad_bounds)
            tok = lax.shift_right_logical(slot_src, 1)
            gather_rows = jnp.minimum(tok, N_LAT - 1) * SLAB
            scatter_rows = ((slot_src & 1) * Y2_ROWS + tok) * SLAB
            y2 = _experts(block_expert, gather_rows, scatter_rows, h_lat.reshape(N_LAT * SLAB, LANES),
                          moe_w[0].reshape(MOE_EXPERTS, D_MODEL, MOE_D_FF),
                          moe_w[1].reshape(MOE_EXPERTS, D_MODEL, MOE_D_FF),
                          moe_w[2].reshape(MOE_EXPERTS, MOE_D_FF, D_MODEL))
            return _combine(x_lat, gates, mod[layer], final_norm_g, y2)
```

```python
import functools

import jax
import jax.numpy as jnp
import numpy as np
from jax import lax
from jax.experimental import pallas as pl
from jax.experimental.pallas import tpu as pltpu

D_MODEL = 1024
BATCH = 4
SEQ = 4096
DEPTH = 2
GRID_W = 64
CTX_LEN = 256
HEAD_DIM = 64
DN_HEADS = 6
ATTN_HEADS = 6
ATTN_KV_HEADS = 2
ATTN_GROUP = ATTN_HEADS // ATTN_KV_HEADS
MLP_GROUPS = 4
DN_WIDTH = DN_HEADS * HEAD_DIM
ATTN_WIDTH = ATTN_HEADS * HEAD_DIM
ATTN_KV_WIDTH = ATTN_KV_HEADS * HEAD_DIM
MLP_WIDTH = MLP_GROUPS * HEAD_DIM
D_MIX = DN_WIDTH + ATTN_WIDTH + MLP_WIDTH
CONV_K = 3
DN_CHUNK = 64
ATTN_SCALE = HEAD_DIM ** -0.5
LOG2E = 1.4426950408889634
MLP_CHUNK = 128
ROPE_THETA = 10000.0
ROPE_AXIS_DIM = HEAD_DIM // 2
ROPE_PAIRS = ROPE_AXIS_DIM // 2
D_FF = 2816
MOE_EXPERTS = 8
MOE_TOP_K = 2
MOE_D_FF = 3584
MOE_BLOCK = 256
NORM_EPS = 1e-6

LANES = 128
SLAB = 8
TT = SEQ + CTX_LEN
TM = 256
N_TILES = TT // TM
CTX_TILE = N_TILES - 1
CTX_ROW = BATCH
N_DN_STEPS = TT // DN_CHUNK
N_CTX_CHUNKS = CTX_LEN // DN_CHUNK
N_LAT_CHUNKS = SEQ // DN_CHUNK
IN_PAD = 3 * DN_WIDTH + DN_WIDTH + ATTN_WIDTH + 2 * ATTN_KV_WIDTH + 2 * MLP_WIDTH + LANES
N_LAT = BATCH * SEQ
N_ASSIGN = N_LAT * MOE_TOP_K
MOE_N_BLOCKS = -(-(N_ASSIGN + MOE_EXPERTS * (MOE_BLOCK - 1)) // MOE_BLOCK)
MOE_ROWS = MOE_N_BLOCKS * MOE_BLOCK
MOE_FF_SPLIT = 2
MOE_FF_BLK = MOE_D_FF // MOE_FF_SPLIT
assert SLAB * LANES == D_MODEL
ROUTE_TILE = 512
VMEM_LIMIT = 56 * 2 ** 20

F32 = jnp.float32
BF16 = jnp.bfloat16
HI = lax.Precision.HIGHEST


def _cparams(*sem):
    return pltpu.CompilerParams(dimension_semantics=sem, vmem_limit_bytes=VMEM_LIMIT)


def _bdot(a, b):
    return jnp.dot(a.astype(BF16), b.astype(BF16), preferred_element_type=F32)


def _bdot_nt(a, b):
    return lax.dot_general(a.astype(BF16), b.astype(BF16), (((1,), (1,)), ((), ())), preferred_element_type=F32)


def _bdot_tn(a, b):
    return lax.dot_general(a.astype(BF16), b.astype(BF16), (((0,), (0,)), ((), ())), preferred_element_type=F32)


def _seg_ones():
    r = lax.shift_right_logical(lax.broadcasted_iota(jnp.int32, (LANES, LANES), 0), 6)
    c = lax.shift_right_logical(lax.broadcasted_iota(jnp.int32, (LANES, LANES), 1), 6)
    return (r == c).astype(F32).astype(BF16)


def _split_bf16(x):
    hi = x.astype(BF16)
    return hi, (x - hi.astype(F32)).astype(BF16)


def _seg_sum(y, ones):
    parts = [jnp.dot(y[:, i:i + LANES].astype(BF16), ones, preferred_element_type=F32)
             for i in range(0, y.shape[-1], LANES)]
    return parts[0] if len(parts) == 1 else jnp.concatenate(parts, axis=-1)


def _softplus(x):
    return jnp.maximum(x, 0.0) + jnp.log1p(jnp.exp(-jnp.abs(x)))


def _ada_norm(x, gain, shift, scale):
    y = x * lax.rsqrt(jnp.mean(x * x, axis=-1, keepdims=True) + NORM_EPS) * gain
    return y * (1.0 + scale) + shift


def _mod_index(b, i):
    return (jnp.where(i == CTX_TILE, CTX_ROW, b), 0, 0)


def _mod_kernel(c_ref, w_ref, b_ref, o_ref):
    cond = jax.nn.silu(c_ref[...])
    o_ref[0] = jnp.dot(cond, w_ref[0], precision=HI, preferred_element_type=F32) + b_ref[0]


def _modulation(cond8, mod_w, mod_b):
    nblk = 4
    bn = 6 * D_MODEL // nblk
    out = pl.pallas_call(
        _mod_kernel,
        grid=(DEPTH, nblk),
        in_specs=[pl.BlockSpec((8, D_MODEL), lambda l, j: (0, 0)),
                  pl.BlockSpec((1, D_MODEL, bn), lambda l, j: (l, 0, j)),
                  pl.BlockSpec((1, 1, bn), lambda l, j: (l, 0, j))],
        out_specs=pl.BlockSpec((1, 8, bn), lambda l, j: (l, 0, j)),
        out_shape=jax.ShapeDtypeStruct((DEPTH, 8, 6 * D_MODEL), F32),
        compiler_params=_cparams("arbitrary", "arbitrary"),
        name="modulation",
    )(cond8, mod_w, mod_b.reshape(DEPTH, 1, 6 * D_MODEL))
    return out.reshape(DEPTH, 8, 6, D_MODEL)


_IN_SPLITS = (3 * DN_WIDTH, DN_WIDTH, ATTN_WIDTH, ATTN_KV_WIDTH, ATTN_KV_WIDTH, MLP_WIDTH, MLP_WIDTH, LANES)


def _tile_rows(refs, x_scr):
    if len(refs) == 1:
        return refs[0][0]
    x_ref, ctx_ref = refs
    i = pl.program_id(1)

    @pl.when(i != CTX_TILE)
    def _():
        x_scr[...] = x_ref[0]

    @pl.when(i == CTX_TILE)
    def _():
        x_scr[...] = ctx_ref[0]

    return x_scr[...]


def _row_specs(split):
    if not split:
        return [pl.BlockSpec((1, TM, D_MODEL), lambda b, i: (b, i, 0))]
    return [pl.BlockSpec((1, TM, D_MODEL), lambda b, i: (b, jnp.minimum(i, CTX_TILE - 1), 0)),
            pl.BlockSpec((1, CTX_LEN, D_MODEL), lambda b, i: (b, 0, 0))]


def _in_proj_kernel(*refs, split):
    n_x = 2 if split else 1
    mod_ref, g_ref, w_ref, gq_ref, gk_ref, c_ref, s_ref = refs[n_x:n_x + 7]
    oqkv, ogate, ou, ov2, oba, oq, ok, ov = refs[n_x + 7:n_x + 15]
    m = mod_ref[0]
    x = _tile_rows(refs[:n_x], refs[-1] if split else None)
    h = _ada_norm(x, g_ref[...], m[0:1], m[1:2]).astype(BF16)
    z = jnp.dot(h, w_ref[...], preferred_element_type=F32)
    cols = {}
    off = 0
    for name, width in zip(("qkv", "gate", "q", "k", "v", "u", "v2", "ba"), _IN_SPLITS):
        cols[name] = z[:, off:off + width]
        off += width
    for ref, name in ((oqkv, "qkv"), (ogate, "gate"), (ou, "u"), (ov2, "v2"), (oba, "ba")):
        ref[0] = cols[name].astype(ref.dtype)

    ones = _seg_ones()
    cs, sn = c_ref[...], s_ref[...]
    lane = lax.broadcasted_iota(jnp.int32, (1, LANES), 1)
    first_half = (lane & (2 * ROPE_PAIRS - 1)) < ROPE_PAIRS

    def norm_rope(t, gain):
        y = t * lax.rsqrt(_seg_sum(t * t, ones) * (1.0 / HEAD_DIM) + NORM_EPS) * gain
        partner = jnp.where(first_half, pltpu.roll(y, LANES - ROPE_PAIRS, 1), pltpu.roll(y, ROPE_PAIRS, 1))
        return y * cs + partner * sn

    for pair in range(ATTN_HEADS // 2):
        q2 = norm_rope(cols["q"][:, pair * LANES:(pair + 1) * LANES], gq_ref[...]) * (ATTN_SCALE * LOG2E)
        oq[0, 2 * pair] = q2[:, :HEAD_DIM].astype(oq.dtype)
        oq[0, 2 * pair + 1] = q2[:, HEAD_DIM:].astype(oq.dtype)
    k2 = norm_rope(cols["k"], gk_ref[...])
    v2 = cols["v"].astype(ov.dtype)
    one_cols = jnp.ones((TM, HEAD_DIM), ov.dtype)
    for hd in range(ATTN_KV_HEADS):
        ok[0, hd] = k2[:, hd * HEAD_DIM:(hd + 1) * HEAD_DIM].astype(ok.dtype)
        ov[0, hd] = jnp.concatenate([v2[:, hd * HEAD_DIM:(hd + 1) * HEAD_DIM], one_cols], axis=1)


def _in_proj(xs, mod_l, gain, w_in_r, q_gain, k_gain, rope_c, rope_s):
    split = len(xs) == 2
    row = lambda b, i: (b, i, 0)
    hrow = lambda b, i: (b, 0, i, 0)
    const2 = lambda b, i: (0, 0)
    tile2 = lambda g: jnp.tile(g.reshape(1, HEAD_DIM), (1, 2))
    widths = (3 * DN_WIDTH, DN_WIDTH, MLP_WIDTH, MLP_WIDTH, LANES)
    dts = (BF16,) * 4 + (F32,)
    return pl.pallas_call(
        functools.partial(_in_proj_kernel, split=split),
        grid=(BATCH, N_TILES),
        in_specs=_row_specs(split) + [
            pl.BlockSpec((1, 6, D_MODEL), _mod_index),
            pl.BlockSpec((1, D_MODEL), const2),
            pl.BlockSpec((D_MODEL, IN_PAD), const2),
            pl.BlockSpec((1, LANES), const2),
            pl.BlockSpec((1, LANES), const2),
            pl.BlockSpec((TM, LANES), lambda b, i: (i, 0)),
            pl.BlockSpec((TM, LANES), lambda b, i: (i, 0))],
        out_specs=[pl.BlockSpec((1, TM, w), row) for w in widths] + [
            pl.BlockSpec((1, ATTN_HEADS, TM, HEAD_DIM), hrow),
            pl.BlockSpec((1, ATTN_KV_HEADS, TM, HEAD_DIM), hrow),
            pl.BlockSpec((1, ATTN_KV_HEADS, TM, LANES), hrow)],
        out_shape=[jax.ShapeDtypeStruct((BATCH, TT, w), dt) for w, dt in zip(widths, dts)] + [
            jax.ShapeDtypeStruct((BATCH, ATTN_HEADS, TT, HEAD_DIM), BF16),
            jax.ShapeDtypeStruct((BATCH, ATTN_KV_HEADS, TT, HEAD_DIM), BF16),
            jax.ShapeDtypeStruct((BATCH, ATTN_KV_HEADS, TT, LANES), BF16)],
        scratch_shapes=[pltpu.VMEM((TM, D_MODEL), F32)] if split else [],
        compiler_params=_cparams("parallel", "arbitrary"),
        name="in_proj",
    )(*xs, mod_l, gain.reshape(1, D_MODEL), w_in_r, tile2(q_gain), tile2(k_gain), rope_c, rope_s)


def _dn_prep_kernel(z_ref, w_ref, o_ref):
    j = pl.program_id(1)
    z = z_ref[0].astype(F32)
    w = w_ref[...]
    row = lax.broadcasted_iota(jnp.int32, (TT, 1), 0)
    first = (row == 0) | (row == SEQ)
    last = (row == SEQ - 1) | (row == TT - 1)
    zp = jnp.where(first, 0.0, pltpu.roll(z, 1, 0))
    zn = jnp.where(last, 0.0, pltpu.roll(z, TT - 1, 0))
    y = jax.nn.silu(w[0:1] * zp + w[1:2] * z + w[2:3] * zn)
    n_qk = 2 * DN_WIDTH // LANES
    n_q = DN_WIDTH // LANES

    @pl.when(j < n_qk)
    def _():
        inv = lax.rsqrt(_seg_sum(y * y, _seg_ones()) + NORM_EPS)
        o_ref[0] = (y * inv * jnp.where(j < n_q, HEAD_DIM ** -0.5, 1.0)).astype(o_ref.dtype)

    @pl.when(j >= n_qk)
    def _():
        o_ref[0] = y.astype(o_ref.dtype)


def _dn_prep(zqkv, conv_w):
    nb = 3 * DN_WIDTH // LANES
    return pl.pallas_call(
        _dn_prep_kernel,
        grid=(BATCH, nb),
        in_specs=[pl.BlockSpec((1, TT, LANES), lambda b, j: (b, 0, j)),
                  pl.BlockSpec((CONV_K, LANES), lambda b, j: (0, j))],
        out_specs=pl.BlockSpec((1, TT, LANES), lambda b, j: (b, 0, j)),
        out_shape=jax.ShapeDtypeStruct((BATCH, TT, 3 * DN_WIDTH), BF16),
        compiler_params=_cparams("parallel", "arbitrary"),
        name="dn_prep",
    )(zqkv, conv_w)


DN_PAIRS = DN_HEADS // 2
DN_SUB = 4
LOG2_CHUNK = DN_CHUNK.bit_length() - 1


def _dn_kernel(qkv_ref, zba_ref, a_ref, dt_ref, o_ref, s_scr):
    C, P2 = DN_CHUNK, 2 * DN_CHUNK
    o_ref[...] = jnp.zeros_like(o_ref)
    s_scr[...] = jnp.zeros_like(s_scr)
    neg_decay_rate = -jnp.exp(a_ref[...])
    dt_bias = dt_ref[...]
    lo = lax.broadcasted_iota(jnp.int32, (1, P2), 1) < C

    def stack(a):
        return jnp.concatenate([jnp.where(lo, a, 0.0), jnp.where(lo, 0.0, a)], axis=0)

    def step(i, carry):
        ri = lax.broadcasted_iota(jnp.int32, (P2, P2), 0)
        ci = lax.broadcasted_iota(jnp.int32, (P2, P2), 1)
        ti = lax.broadcasted_iota(jnp.int32, (C, C), 0)
        tj = lax.broadcasted_iota(jnp.int32, (C, C), 1)
        same = lambda sh: lax.shift_right_logical(ri, sh) == lax.shift_right_logical(ci, sh)
        same_head = same(LOG2_CHUNK)
        eye = (ri == ci).astype(F32)

        ch = []
        for sub in range(DN_SUB):
            s = i * DN_SUB + sub
            chunk_of = (jnp.where(s < N_CTX_CHUNKS, N_LAT_CHUNKS + s, s - N_CTX_CHUNKS), N_DN_STEPS - 1 - s)
            for d in range(2):
                r0 = pl.multiple_of(chunk_of[d] * C, C)
                zba = zba_ref[0, pl.ds(r0, C), :]
                beta_all = jax.nn.sigmoid(zba)
                g_all = neg_decay_rate * _softplus(zba + dt_bias)
                tri = (ti >= tj) if d == 0 else (ti <= tj)
                gc_all = jnp.dot(tri.astype(F32), g_all, precision=HI, preferred_element_type=F32)
                gc_t = jnp.concatenate([gc_all, gc_all], axis=0).T
                last = C - 1 if d == 0 else 0
                for p in range(DN_PAIRS):
                    lb = (d * DN_HEADS + 2 * p, d * DN_HEADS + 2 * p + 1)
                    la = (2 * DN_HEADS + lb[0], 2 * DN_HEADS + lb[1])
                    col = lambda t, l: t[:, l:l + 1]
                    q2 = qkv_ref[0, pl.ds(r0, C), p * LANES:(p + 1) * LANES].astype(F32)
                    k2 = qkv_ref[0, pl.ds(r0, C), DN_WIDTH + p * LANES:DN_WIDTH + (p + 1) * LANES].astype(F32)
                    v2 = qkv_ref[0, pl.ds(r0, C), 2 * DN_WIDTH + p * LANES:2 * DN_WIDTH + (p + 1) * LANES].astype(F32)
                    beta2 = jnp.where(lo, col(beta_all, lb[0]), col(beta_all, lb[1]))
                    gc2 = jnp.where(lo, col(gc_all, la[0]), col(gc_all, la[1]))
                    gcol = jnp.concatenate([col(gc_all, la[0]), col(gc_all, la[1])], axis=0)
                    grow = jnp.where(lo, gc_t[la[0]:la[0] + 1, :], gc_t[la[1]:la[1] + 1, :])
                    g_last = jnp.where(lo, gc_all[last:last + 1, la[0]:la[0] + 1],
                                       gc_all[last:last + 1, la[1]:la[1] + 1])
                    e_diff = jnp.exp(gcol - grow)
                    order = (ri >= ci) if d == 0 else (ri <= ci)
                    kb2 = k2 * beta2
                    e_gc = jnp.exp(gc2)
                    ch.append(dict(
                        r0=r0, sub=sub, d=d, p=p, g_last=g_last,
                        dec_incl=jnp.where(same_head & order, e_diff, 0.0),
                        dec_strict=jnp.where(same_head & order & (ri != ci), e_diff, 0.0),
                        kq=jnp.concatenate([stack(kb2), stack(q2)], axis=0).astype(BF16),
                        k=stack(k2).astype(BF16),
                        rhs=jnp.concatenate([stack(v2 * beta2), stack(kb2 * e_gc)], axis=1).astype(BF16),
                        kd=stack(k2 * jnp.exp(g_last - gc2)).astype(BF16),
                        qe=stack(q2 * e_gc)))
        for c in ch:
            kk_qk = lax.dot_general(c["kq"], c["k"], (((1,), (1,)), ((), ())), preferred_element_type=F32)
            c["a"] = kk_qk[:P2] * c["dec_strict"]
            c["attn"] = (kk_qk[P2:] * c["dec_incl"]).astype(BF16)
            c["t"] = eye - jnp.where(same(1), c["a"], 0.0)
        for lvl in range(1, LOG2_CHUNK):
            joins = same(lvl + 1) & ~same(lvl)
            for c in ch:
                c["m"] = _bdot(jnp.where(joins, c["a"], 0.0), c["t"])
            for c in ch:
                c["t"] = c["t"] - _bdot(c["t"], c["m"])
        for c in ch:
            c["uw"] = _bdot(c["t"], c["rhs"]).astype(BF16)
        for c in ch:
            oa = jnp.dot(c["attn"], c["uw"], preferred_element_type=F32)
            c["o0"] = oa[:, :LANES]
            c["qp"] = c["qe"] - oa[:, LANES:]
            c["np"] = lax.dot_general(c["kd"], c["uw"], (((0,), (0,)), ((), ())), preferred_element_type=F32)

        states = [s_scr[j] for j in range(2 * DN_PAIRS)]
        for sub in range(DN_SUB):
            cur = [c for c in ch if c["sub"] == sub]
            for c in cur:
                j = c["d"] * DN_PAIRS + c["p"]
                c["r"] = _bdot(jnp.concatenate([c["qp"], c["np"][:, LANES:]], axis=0), states[j])
            for d in range(2):
                outs = []
                for c in cur:
                    if c["d"] != d:
                        continue
                    j = d * DN_PAIRS + c["p"]
                    o_st = c["o0"] + c["r"][:P2]
                    outs.append(o_st[:C] + o_st[C:])
                    states[j] = states[j] * jnp.exp(c["g_last"]) + c["np"][:, :LANES] - c["r"][P2:]
                    r0 = c["r0"]
                o_ref[0, pl.ds(r0, C), :] += jnp.concatenate(outs, axis=1)
        for j in range(2 * DN_PAIRS):
            s_scr[j] = states[j]
        return carry

    lax.fori_loop(0, N_DN_STEPS // DN_SUB, step, 0)


def _deltanet(qkv, zba, a_log, dt_bias):
    pad = lambda t: jnp.zeros((1, LANES), F32).at[0, 2 * DN_HEADS:4 * DN_HEADS].set(t.reshape(-1))
    return pl.pallas_call(
        _dn_kernel,
        grid=(BATCH,),
        in_specs=[pl.BlockSpec((1, TT, 3 * DN_WIDTH), lambda b: (b, 0, 0)),
                  pl.BlockSpec((1, TT, LANES), lambda b: (b, 0, 0)),
                  pl.BlockSpec((1, LANES), lambda b: (0, 0)),
                  pl.BlockSpec((1, LANES), lambda b: (0, 0))],
        out_specs=pl.BlockSpec((1, TT, DN_WIDTH), lambda b: (b, 0, 0)),
        out_shape=jax.ShapeDtypeStruct((BATCH, TT, DN_WIDTH), F32),
        scratch_shapes=[pltpu.VMEM((2 * DN_PAIRS, LANES, LANES), F32)],
        compiler_params=_cparams("parallel"),
        name="deltanet",
    )(qkv, zba, pad(a_log), pad(dt_bias))


def _rope_tables():
    rows = SEQ // GRID_W
    row = jnp.repeat(jnp.arange(rows, dtype=F32), GRID_W)
    col = jnp.tile(jnp.arange(GRID_W, dtype=F32), rows)
    inv = ROPE_THETA ** (-2.0 * jnp.arange(ROPE_PAIRS, dtype=F32) / ROPE_AXIS_DIM)
    ang = jnp.stack([row[:, None] * inv, col[:, None] * inv], axis=1)
    cos, sin = jnp.cos(ang), jnp.sin(ang)
    c = jnp.concatenate([cos[:, 0], cos[:, 0], cos[:, 1], cos[:, 1]], axis=-1)
    s = jnp.concatenate([-sin[:, 0], sin[:, 0], -sin[:, 1], sin[:, 1]], axis=-1)
    c = jnp.concatenate([c, jnp.ones((CTX_LEN, HEAD_DIM), F32)], axis=0)
    s = jnp.concatenate([s, jnp.zeros((CTX_LEN, HEAD_DIM), F32)], axis=0)
    return jnp.tile(c, (1, 2)), jnp.tile(s, (1, 2))


ATTN_TILING = (128, 8)
ATTN_KB = 512
ATTN_KEY_BLOCKS = tuple((j * ATTN_KB, ATTN_KB) for j in range(SEQ // ATTN_KB)) + ((SEQ, CTX_LEN),)


def _dot_nt(a, b):
    return lax.dot_general(a, b, (((1,), (1,)), ((), ())), preferred_element_type=F32)


def _attn_finish(acc):
    return acc[:, :HEAD_DIM] / acc[:, HEAD_DIM:HEAD_DIM + 1]


def _attn_lat_kernel(q_ref, k_ref, v_ref, *rest, n_cast, tq_sub, n_sub):
    cast_in, o_ref, cast_out = rest[:n_cast], rest[n_cast], rest[n_cast + 1:2 * n_cast + 1]
    s0_scr, s1_scr = rest[2 * n_cast + 1:]
    for src, dst in zip(cast_in, cast_out):
        dst[...] = src[...].astype(dst.dtype)
    rows = ATTN_GROUP * tq_sub
    s_scr = (s0_scr, s1_scr)
    mpart = [None] * n_sub
    acc = [None] * n_sub
    for stage in range(n_sub + 1):
        a, b = stage, stage - 1
        if a < n_sub:
            qa = q_ref[0, :, a * tq_sub:(a + 1) * tq_sub, :].reshape(rows, HEAD_DIM)
        if b >= 0:
            m_b = jnp.max(mpart[b], axis=-1, keepdims=True)
        for k0, kn in ATTN_KEY_BLOCKS:
            if a < n_sub:
                s = _dot_nt(qa, k_ref[0, 0, k0:k0 + kn, :])
                s_scr[a % 2][:, k0:k0 + kn] = s
                blk = functools.reduce(jnp.maximum, [s[:, i:i + LANES] for i in range(0, kn, LANES)])
                mpart[a] = blk if mpart[a] is None else jnp.maximum(mpart[a], blk)
            if b >= 0:
                p = jnp.exp2(s_scr[b % 2][:, k0:k0 + kn] - m_b).astype(BF16)
                pv = jnp.dot(p, v_ref[0, 0, k0:k0 + kn, :], preferred_element_type=F32)
                acc[b] = pv if acc[b] is None else acc[b] + pv
        if b >= 0:
            o = _attn_finish(acc[b]).reshape(ATTN_GROUP, tq_sub, HEAD_DIM)
            o_ref[0, :, b * tq_sub:(b + 1) * tq_sub, :] = o.astype(o_ref.dtype)


def _attn_ctx_kernel(q_ref, k_ref, v_ref, o_ref):
    q = q_ref[0].reshape(ATTN_GROUP * CTX_LEN, HEAD_DIM)
    s = _dot_nt(q, k_ref[0, 0])
    p = jnp.exp2(s - jnp.max(s, axis=-1, keepdims=True)).astype(BF16)
    o = _attn_finish(jnp.dot(p, v_ref[0, 0], preferred_element_type=F32))
    o_ref[0] = o.reshape(ATTN_GROUP, CTX_LEN, HEAD_DIM).astype(o_ref.dtype)


def _attention_lat(qh, kh, vh, cast=()):
    tq_sub, n_sub = ATTN_TILING
    tq = n_sub * tq_sub
    rows = ATTN_GROUP * tq_sub
    nq = SEQ // tq
    n_steps = BATCH * ATTN_KV_HEADS * nq
    step = lambda b, g, i: ((b * ATTN_KV_HEADS + g) * nq + i, 0)
    cast_specs = [pl.BlockSpec((w.shape[0] // n_steps, w.shape[1]), step) for w in cast]
    out = pl.pallas_call(
        functools.partial(_attn_lat_kernel, n_cast=len(cast), tq_sub=tq_sub, n_sub=n_sub),
        grid=(BATCH, ATTN_KV_HEADS, nq),
        in_specs=[pl.BlockSpec((1, ATTN_GROUP, tq, HEAD_DIM), lambda b, g, i: (b, g, i, 0)),
                  pl.BlockSpec((1, 1, TT, HEAD_DIM), lambda b, g, i: (b, g, 0, 0)),
                  pl.BlockSpec((1, 1, TT, LANES), lambda b, g, i: (b, g, 0, 0))] + cast_specs,
        out_specs=[pl.BlockSpec((1, ATTN_GROUP, tq, HEAD_DIM), lambda b, g, i: (b, g, i, 0))] + cast_specs,
        out_shape=[jax.ShapeDtypeStruct((BATCH, ATTN_HEADS, SEQ, HEAD_DIM), BF16)]
        + [jax.ShapeDtypeStruct(w.shape, BF16) for w in cast],
        scratch_shapes=[pltpu.VMEM((rows, TT), F32), pltpu.VMEM((rows, TT), F32)],
        compiler_params=_cparams("arbitrary", "arbitrary", "arbitrary"),
        name="attention_lat",
    )(qh, kh, vh, *cast)
    return out[0], tuple(out[1:])


def _attention_ctx(qh, kh, vh):
    ctx_blk = SEQ // CTX_LEN
    return pl.pallas_call(
        _attn_ctx_kernel,
        grid=(BATCH, ATTN_KV_HEADS),
        in_specs=[pl.BlockSpec((1, ATTN_GROUP, CTX_LEN, HEAD_DIM), lambda b, g: (b, g, ctx_blk, 0)),
                  pl.BlockSpec((1, 1, CTX_LEN, HEAD_DIM), lambda b, g: (b, g, ctx_blk, 0)),
                  pl.BlockSpec((1, 1, CTX_LEN, LANES), lambda b, g: (b, g, ctx_blk, 0))],
        out_specs=pl.BlockSpec((1, ATTN_GROUP, CTX_LEN, HEAD_DIM), lambda b, g: (b, g, 0, 0)),
        out_shape=jax.ShapeDtypeStruct((BATCH, ATTN_HEADS, CTX_LEN, HEAD_DIM), BF16),
        compiler_params=_cparams("parallel", "arbitrary"),
        name="attention_ctx",
    )(qh, kh, vh)


def _mix_kernel(*refs, moe, split):
    n = 2 if split else 1
    x_refs, refs = refs[:n], refs[n:]
    mod_ref, o_dn_ref, zgate_ref = refs[:3]
    attn_refs, refs = refs[3:3 + n], refs[3 + n:]
    zu_ref, zv_ref, dn_g_ref, sgu_g_ref, sgu_w_ref, sgu_b_ref, w_out_ref, g2_ref = refs[:8]
    rest = refs[8:]
    if moe:
        rw_ref, rb_ref, ox, oh, ologit = rest[:5]
    else:
        ox, oh = rest[:2]
    ones = _seg_ones()
    m = mod_ref[0]
    o = o_dn_ref[0]
    dn = o * lax.rsqrt(_seg_sum(o * o, ones) * (1.0 / HEAD_DIM) + NORM_EPS) * dn_g_ref[...]
    dn = dn * jax.nn.silu(zgate_ref[0].astype(F32))
    heads = lambda ref: jnp.concatenate([ref[0, h] for h in range(ATTN_HEADS)], axis=1)
    if split:
        x_scr, at_scr = rest[-2:]
        is_ctx = pl.program_id(1) == CTX_TILE

        @pl.when(jnp.logical_not(is_ctx))
        def _():
            at_scr[...] = heads(attn_refs[0])

        @pl.when(is_ctx)
        def _():
            at_scr[...] = heads(attn_refs[1])

        at = at_scr[...]
        x_in = _tile_rows(x_refs, x_scr)
    else:
        at = heads(attn_refs[0])
        x_in = x_refs[0][0]
    u = jax.nn.gelu(zu_ref[0].astype(F32))
    v = jax.nn.gelu(zv_ref[0].astype(F32))
    v = (v * lax.rsqrt(_seg_sum(v * v, ones) * (1.0 / HEAD_DIM) + NORM_EPS) * sgu_g_ref[...]).astype(BF16)
    gd = MLP_WIDTH // MLP_GROUPS
    chunks = []
    for ci in range(TM // MLP_CHUNK):
        vc = v[ci * MLP_CHUNK:(ci + 1) * MLP_CHUNK]
        mixed = jnp.concatenate(
            [jnp.dot(sgu_w_ref[g], vc[:, g * gd:(g + 1) * gd], preferred_element_type=F32)
             for g in range(MLP_GROUPS)], axis=1)
        chunks.append(mixed + sgu_b_ref[...])
    sg = u * jnp.concatenate(chunks, axis=0)
    mixed_all = jnp.concatenate([dn.astype(BF16), at, sg.astype(BF16)], axis=1)
    y = jnp.dot(mixed_all, w_out_ref[...], preferred_element_type=F32)
    x = x_in + m[2:3] * y
    ox[0] = x
    h = _ada_norm(x, g2_ref[...], m[3:4], m[4:5])
    if moe:
        for kk in range(D_MODEL // LANES):
            oh[0, pl.ds(kk, TM, stride=SLAB), :] = h[:, kk * LANES:(kk + 1) * LANES]
    else:
        oh[0] = h.astype(oh.dtype)
    if moe:
        h_hi, h_lo = _split_bf16(h)
        r_hi, r_lo = _split_bf16(rw_ref[...])
        ologit[0] = (jnp.dot(h_hi, r_hi, preferred_element_type=F32) + jnp.dot(h_hi, r_lo, preferred_element_type=F32)
                     + jnp.dot(h_lo, r_hi, preferred_element_type=F32)) + rb_ref[...]


def _mix(xs, mod_l, o_dn, zgate, attns, zu, zv2, dn_g, sgu_g, sgu_w, sgu_b, w_out, gain2, router=None):
    moe = router is not None
    split = len(xs) == 2
    assert len(attns) == len(xs)
    nt = N_TILES - 1 if moe else N_TILES
    rows = nt * TM
    row = lambda b, i: (b, i, 0)
    orow = row
    const2 = lambda b, i: (0, 0)
    gd = MLP_WIDTH // MLP_GROUPS
    sgu_b_x = jnp.repeat(sgu_b.T, gd, axis=1)
    if split:
        attn_specs = [pl.BlockSpec((1, ATTN_HEADS, TM, HEAD_DIM), lambda b, i: (b, 0, jnp.minimum(i, CTX_TILE - 1), 0)),
                      pl.BlockSpec((1, ATTN_HEADS, CTX_LEN, HEAD_DIM), lambda b, i: (b, 0, 0, 0))]
    else:
        attn_specs = [pl.BlockSpec((1, ATTN_HEADS, TM, HEAD_DIM), lambda b, i: (b, 0, i, 0))]
    in_specs = _row_specs(split) + [
                pl.BlockSpec((1, 6, D_MODEL), _mod_index),
                pl.BlockSpec((1, TM, DN_WIDTH), row),
                pl.BlockSpec((1, TM, DN_WIDTH), row)] + attn_specs + [
                pl.BlockSpec((1, TM, MLP_WIDTH), row),
                pl.BlockSpec((1, TM, MLP_WIDTH), row),
                pl.BlockSpec((1, DN_WIDTH), const2),
                pl.BlockSpec((1, MLP_WIDTH), const2),
                pl.BlockSpec((MLP_GROUPS, MLP_CHUNK, MLP_CHUNK), lambda b, i: (0, 0, 0)),
                pl.BlockSpec((MLP_CHUNK, MLP_WIDTH), const2),
                pl.BlockSpec((D_MIX, D_MODEL), const2),
                pl.BlockSpec((1, D_MODEL), const2)]
    args = [*xs, mod_l, o_dn, zgate, *attns, zu, zv2, jnp.tile(dn_g.reshape(1, HEAD_DIM), (1, DN_HEADS)),
            sgu_g.reshape(1, MLP_WIDTH), sgu_w.astype(BF16), sgu_b_x, w_out, gain2.reshape(1, D_MODEL)]
    if moe:
        h_spec = pl.BlockSpec((1, TM * SLAB, LANES), orow)
        h_shape = jax.ShapeDtypeStruct((BATCH, rows * SLAB, LANES), F32)
    else:
        h_spec = pl.BlockSpec((1, TM, D_MODEL), orow)
        h_shape = jax.ShapeDtypeStruct((BATCH, rows, D_MODEL), BF16)
    out_specs = [pl.BlockSpec((1, TM, D_MODEL), orow), h_spec]
    out_shape = [jax.ShapeDtypeStruct((BATCH, rows, D_MODEL), F32), h_shape]
    if moe:
        rw, rb = router
        in_specs += [pl.BlockSpec((D_MODEL, LANES), const2), pl.BlockSpec((1, LANES), const2)]
        args += [jnp.pad(rw, ((0, 0), (0, LANES - MOE_EXPERTS))),
                 jnp.pad(rb.reshape(1, MOE_EXPERTS), ((0, 0), (0, LANES - MOE_EXPERTS)))]
        out_specs.append(pl.BlockSpec((1, TM, LANES), orow))
        out_shape.append(jax.ShapeDtypeStruct((BATCH, rows, LANES), F32))
    return pl.pallas_call(
        functools.partial(_mix_kernel, moe=moe, split=split),
        grid=(BATCH, nt),
        in_specs=in_specs, out_specs=out_specs, out_shape=out_shape,
        scratch_shapes=[pltpu.VMEM((TM, D_MODEL), F32), pltpu.VMEM((TM, ATTN_WIDTH), BF16)] if split else [],
        compiler_params=_cparams("parallel", "arbitrary"),
        name="mix_moe" if moe else "mix",
    )(*args)


FFN_SPLIT = 2
FFN_BLK = D_FF // FFN_SPLIT


def _ffn_kernel(x_ref, h_ref, mod_ref, w1_ref, w3_ref, w2_ref, o_ref):
    h = h_ref[0]
    y = None
    for c in range(FFN_SPLIT):
        sl = slice(c * FFN_BLK, (c + 1) * FFN_BLK)
        a = jnp.dot(h, w1_ref[:, sl], preferred_element_type=F32)
        b = jnp.dot(h, w3_ref[:, sl], preferred_element_type=F32)
        part = jnp.dot((jax.nn.silu(a) * b).astype(BF16), w2_ref[sl, :], preferred_element_type=F32)
        y = part if y is None else y + part
    o_ref[0] = x_ref[0] + mod_ref[0][5:6] * y


def _ffn(x_all, h_all, mod_l, w1, w3, w2):
    row = lambda b, i: (b, i, 0)
    const2 = lambda b, i: (0, 0)
    return pl.pallas_call(
        _ffn_kernel,
        grid=(BATCH, N_TILES),
        in_specs=[pl.BlockSpec((1, TM, D_MODEL), row),
                  pl.BlockSpec((1, TM, D_MODEL), row),
                  pl.BlockSpec((1, 6, D_MODEL), _mod_index),
                  pl.BlockSpec((D_MODEL, D_FF), const2),
                  pl.BlockSpec((D_MODEL, D_FF), const2),
                  pl.BlockSpec((D_FF, D_MODEL), const2)],
        out_specs=pl.BlockSpec((1, TM, D_MODEL), row),
        out_shape=jax.ShapeDtypeStruct((BATCH, TT, D_MODEL), F32),
        compiler_params=_cparams("parallel", "arbitrary"),
        name="ffn",
    )(x_all, h_all, mod_l, w1, w3, w2)


def _route_kernel(logit_ref, dest_ref, gate_ref, count_ref, run_scr):
    phase, t = pl.program_id(0), pl.program_id(1)

    @pl.when((phase == 0) & (t == 0))
    def _():
        run_scr[...] = jnp.zeros_like(run_scr)

    lane = lax.broadcasted_iota(jnp.int32, (ROUTE_TILE, LANES), 1).astype(F32)
    logits = jnp.where(lane < MOE_EXPERTS, logit_ref[...], -jnp.inf)
    m1 = jnp.max(logits, axis=-1, keepdims=True)
    e1 = jnp.min(jnp.where(logits == m1, lane, float(LANES)), axis=-1, keepdims=True)
    rest = jnp.where(lane == e1, -jnp.inf, logits)
    m2 = jnp.max(rest, axis=-1, keepdims=True)
    e2 = jnp.min(jnp.where(rest == m2, lane, float(LANES)), axis=-1, keepdims=True)
    hot1 = (lane == e1).astype(F32)
    hot2 = (lane == e2).astype(F32)
    hot = hot1 + hot2
    tile_count = jnp.sum(hot, axis=0, keepdims=True)

    @pl.when(phase == 0)
    def _():
        run_scr[0:1] = run_scr[0:1] + tile_count

    @pl.when((phase == 1) & (t == 0))
    def _():
        counts = jnp.broadcast_to(run_scr[0:1], (8, LANES))
        count_ref[...] = counts.astype(jnp.int32)
        padded = jnp.ceil(counts * (1.0 / MOE_BLOCK)) * MOE_BLOCK
        ei = lax.broadcasted_iota(jnp.int32, (LANES, LANES), 0)
        ej = lax.broadcasted_iota(jnp.int32, (LANES, LANES), 1)
        before = (ei < ej).astype(F32)
        run_scr[1:2] = jnp.dot(padded, before, precision=HI, preferred_element_type=F32)[0:1]

    @pl.when(phase == 1)
    def _():
        ri = lax.broadcasted_iota(jnp.int32, (ROUTE_TILE, ROUTE_TILE), 0)
        rj = lax.broadcasted_iota(jnp.int32, (ROUTE_TILE, ROUTE_TILE), 1)
        earlier = (ri > rj).astype(BF16)
        within = jnp.dot(earlier, hot.astype(BF16), preferred_element_type=F32)
        pos = within + run_scr[1:2]
        d1 = jnp.sum(pos * hot1, axis=-1, keepdims=True)
        d2 = jnp.sum(pos * hot2, axis=-1, keepdims=True)
        dest_ref[...] = jnp.concatenate([d1, d2], axis=1).astype(jnp.int32)
        w2 = jnp.exp(m2 - m1)
        gate_ref[...] = jnp.concatenate([1.0 / (1.0 + w2), w2 / (1.0 + w2)], axis=1)
        run_scr[1:2] = run_scr[1:2] + tile_count


def _route(logits):
    nt = N_LAT // ROUTE_TILE
    return pl.pallas_call(
        _route_kernel,
        grid=(2, nt),
        in_specs=[pl.BlockSpec((ROUTE_TILE, LANES), lambda p, t: (t, 0))],
        out_specs=[pl.BlockSpec((ROUTE_TILE, MOE_TOP_K), lambda p, t: (t * p, 0)),
                   pl.BlockSpec((ROUTE_TILE, MOE_TOP_K), lambda p, t: (t * p, 0)),
                   pl.BlockSpec((8, LANES), lambda p, t: (0, 0))],
        out_shape=[jax.ShapeDtypeStruct((N_LAT, MOE_TOP_K), jnp.int32),
                   jax.ShapeDtypeStruct((N_LAT, MOE_TOP_K), F32),
                   jax.ShapeDtypeStruct((8, LANES), jnp.int32)],
        scratch_shapes=[pltpu.VMEM((8, LANES), F32)],
        compiler_params=_cparams("arbitrary", "arbitrary"),
        name="moe_route",
    )(logits)


N_PAD_SLOTS = MOE_ROWS - N_ASSIGN
Y2_ROWS = N_LAT + N_PAD_SLOTS // MOE_TOP_K


def _invert_kernel(dest_ref, bounds_ref, slot_ref):
    def real(a, c):
        slot_ref[dest_ref[a]] = a
        return c

    lax.fori_loop(0, N_ASSIGN, real, 0, unroll=8)

    def pad_range(e, count):
        def pad(s, cnt):
            slot_ref[s] = N_ASSIGN + cnt
            return cnt + 1

        return lax.fori_loop(bounds_ref[2 * e], bounds_ref[2 * e + 1], pad, count)

    lax.fori_loop(0, MOE_EXPERTS + 1, pad_range, 0)


def _invert(dest_flat, pad_bounds):
    return pl.pallas_call(
        _invert_kernel,
        grid_spec=pltpu.PrefetchScalarGridSpec(
            num_scalar_prefetch=2, grid=(1,), in_specs=[],
            out_specs=pl.BlockSpec(memory_space=pltpu.SMEM)),
        out_shape=jax.ShapeDtypeStruct((MOE_ROWS,), jnp.int32),
        compiler_params=_cparams("arbitrary"),
        name="moe_invert",
    )(dest_flat, pad_bounds)


LAST_BLOCK = MOE_N_BLOCKS - 1
MOE_NBUF = 3


def _swiglu_half(x, w1_ref, w3_ref, w2_ref):
    a = jnp.dot(x, w1_ref[0], preferred_element_type=F32)
    b = jnp.dot(x, w3_ref[0], preferred_element_type=F32)
    return jnp.dot((jax.nn.silu(a) * b).astype(BF16), w2_ref[0], preferred_element_type=F32)


def _expert_gather_kernel(be_ref, slot_ref, h_hbm, w1_ref, w3_ref, w2_ref, xs_ref, yb_ref, xbuf, sem):
    del be_ref
    j = pl.program_id(0)
    cur = j % MOE_NBUF

    def gather(blk, buf):
        base = blk * MOE_BLOCK
        for r in range(MOE_BLOCK):
            row = pl.multiple_of(slot_ref[base + r], SLAB)
            pltpu.make_async_copy(h_hbm.at[pl.ds(row, SLAB)], xbuf.at[buf, pl.ds(r * SLAB, SLAB)],
                                  sem.at[buf]).start()

    def wait(buf):
        pltpu.make_async_copy(h_hbm.at[pl.ds(0, MOE_BLOCK * SLAB)], xbuf.at[buf], sem.at[buf]).wait()

    @pl.when(j == 0)
    def _():
        gather(0, 0)
        gather(1, 1)

    wait(cur)
    for kk in range(D_MODEL // LANES):
        xs_ref[:, kk * LANES:(kk + 1) * LANES] = xbuf[cur, pl.ds(kk, MOE_BLOCK, stride=SLAB), :]
    gather(jnp.minimum(j + 2, LAST_BLOCK), (j + 2) % MOE_NBUF)
    yb_ref[...] = _swiglu_half(xs_ref[...].astype(BF16), w1_ref, w3_ref, w2_ref)

    @pl.when(j == LAST_BLOCK)
    def _():
        wait((j + 1) % MOE_NBUF)
        wait((j + 2) % MOE_NBUF)


def _expert_scatter_kernel(be_ref, slot_ref, xs_ref, w1_ref, w3_ref, w2_ref, yb_ref, y2_hbm, obuf, sem):
    del be_ref
    j = pl.program_id(0)
    cur = j % MOE_NBUF
    prev = (j + MOE_NBUF - 1) % MOE_NBUF

    def scatter(blk, buf):
        base = blk * MOE_BLOCK
        for r in range(MOE_BLOCK):
            row = pl.multiple_of(slot_ref[base + r], SLAB)
            pltpu.make_async_copy(obuf.at[buf, pl.ds(r * SLAB, SLAB)], y2_hbm.at[pl.ds(row, SLAB)],
                                  sem.at[buf]).start()

    def wait(buf):
        pltpu.make_async_copy(obuf.at[buf], y2_hbm.at[pl.ds(0, MOE_BLOCK * SLAB)], sem.at[buf]).wait()

    def compute():
        out = yb_ref[...] + _swiglu_half(xs_ref[...].astype(BF16), w1_ref, w3_ref, w2_ref)
        for kk in range(D_MODEL // LANES):
            obuf[cur, pl.ds(kk, MOE_BLOCK, stride=SLAB), :] = out[:, kk * LANES:(kk + 1) * LANES]

    @pl.when(j >= MOE_NBUF)
    def _():
        wait(cur)

    @pl.when(j == 0)
    def _():
        compute()

    @pl.when(j > 0)
    def _():
        scatter(j - 1, prev)
        compute()

    @pl.when(j == LAST_BLOCK)
    def _():
        scatter(j, cur)
        for b in range(MOE_NBUF):
            wait(b)


def _experts(block_expert, gather_rows, scatter_rows, h_lat, w1, w3, w2):
    def w_specs(c):
        return [pl.BlockSpec((1, D_MODEL, MOE_FF_BLK), lambda j, be, sl: (be[j], 0, c)),
                pl.BlockSpec((1, D_MODEL, MOE_FF_BLK), lambda j, be, sl: (be[j], 0, c)),
                pl.BlockSpec((1, MOE_FF_BLK, D_MODEL), lambda j, be, sl: (be[j], c, 0))]

    blk = pl.BlockSpec((MOE_BLOCK, D_MODEL), lambda j, be, sl: (j, 0))
    xs, yb = pl.pallas_call(
        _expert_gather_kernel,
        grid_spec=pltpu.PrefetchScalarGridSpec(
            num_scalar_prefetch=2, grid=(MOE_N_BLOCKS,),
            in_specs=[pl.BlockSpec(memory_space=pl.ANY)] + w_specs(0),
            out_specs=[blk, blk],
            scratch_shapes=[pltpu.VMEM((MOE_NBUF, MOE_BLOCK * SLAB, LANES), F32),
                            pltpu.SemaphoreType.DMA((MOE_NBUF,))]),
        out_shape=[jax.ShapeDtypeStruct((MOE_ROWS, D_MODEL), F32)] * 2,
        compiler_params=_cparams("arbitrary"),
        name="moe_experts_gather",
    )(block_expert, gather_rows, h_lat, w1, w3, w2)
    return pl.pallas_call(
        _expert_scatter_kernel,
        grid_spec=pltpu.PrefetchScalarGridSpec(
            num_scalar_prefetch=2, grid=(MOE_N_BLOCKS,),
            in_specs=[blk] + w_specs(1) + [blk],
            out_specs=pl.BlockSpec(memory_space=pl.ANY),
            scratch_shapes=[pltpu.VMEM((MOE_NBUF, MOE_BLOCK * SLAB, LANES), F32),
                            pltpu.SemaphoreType.DMA((MOE_NBUF,))]),
        out_shape=jax.ShapeDtypeStruct((MOE_TOP_K * Y2_ROWS * SLAB, LANES), F32),
        compiler_params=_cparams("arbitrary"),
        name="moe_experts_scatter",
    )(block_expert, scatter_rows, xs, w1, w3, w2, yb)


def _combine_kernel(x_ref, gate_ref, mod_ref, g_ref, ya_ref, yb_ref, o_ref):
    gates = gate_ref[0]
    rows = lambda ref: jnp.concatenate(
        [ref[pl.ds(kk, TM, stride=SLAB), :] for kk in range(D_MODEL // LANES)], axis=1)
    y = rows(ya_ref) * gates[:, 0:1] + rows(yb_ref) * gates[:, 1:2]
    x = x_ref[0] + mod_ref[0][5:6] * y
    o_ref[0] = x * lax.rsqrt(jnp.mean(x * x, axis=-1, keepdims=True) + NORM_EPS) * g_ref[...]


def _combine(x_lat, gates, mod_l, final_g, y2):
    nt = SEQ // TM
    return pl.pallas_call(
        _combine_kernel,
        grid=(BATCH, nt),
        in_specs=[pl.BlockSpec((1, TM, D_MODEL), lambda b, i: (b, i, 0)),
                  pl.BlockSpec((1, TM, MOE_TOP_K), lambda b, i: (b, i, 0)),
                  pl.BlockSpec((1, 6, D_MODEL), lambda b, i: (b, 0, 0)),
                  pl.BlockSpec((1, D_MODEL), lambda b, i: (0, 0)),
                  pl.BlockSpec((TM * SLAB, LANES), lambda b, i: (b * nt + i, 0)),
                  pl.BlockSpec((TM * SLAB, LANES), lambda b, i: (Y2_ROWS // TM + b * nt + i, 0))],
        out_specs=pl.BlockSpec((1, TM, D_MODEL), lambda b, i: (b, i, 0)),
        out_shape=jax.ShapeDtypeStruct((BATCH, SEQ, D_MODEL), F32),
        compiler_params=_cparams("parallel", "arbitrary"),
        name="moe_combine",
    )(x_lat, gates.reshape(BATCH, SEQ, MOE_TOP_K), mod_l, final_g.reshape(1, D_MODEL), y2, y2)


def _reorder_w_in(w):
    s = np.cumsum((3 * DN_WIDTH, DN_WIDTH, 2 * DN_HEADS, 2 * DN_HEADS, ATTN_WIDTH, ATTN_KV_WIDTH, ATTN_KV_WIDTH,
                   MLP_WIDTH, MLP_WIDTH)).tolist()
    ba = w[:, s[1]:s[3]]
    return jnp.concatenate([w[:, :s[1]], w[:, s[3]:], ba,
                            jnp.zeros((D_MODEL, LANES - 4 * DN_HEADS), w.dtype)], axis=1).astype(BF16)


def kernel(x, c, ctx, c_ctx, mod_w, mod_b, norm1_g, norm2_g, w_in, conv_w, dn_a_log, dn_dt_bias, dn_norm_g,
           q_norm_g, k_norm_g, sgu_norm_g, sgu_w, sgu_b, w_out, ffn_w1, ffn_w3, ffn_w2, router_w, router_b,
           moe_w1, moe_w3, moe_w2, final_norm_g):
    assert DEPTH == 2 and x.shape == (BATCH, SEQ, D_MODEL) and ctx.shape == (BATCH, CTX_LEN, D_MODEL)
    cond8 = jnp.concatenate([c, c_ctx[None], jnp.zeros((8 - BATCH - 1, D_MODEL), F32)], axis=0)
    mod = _modulation(cond8, mod_w, mod_b)
    rope_c, rope_s = _rope_tables()
    xs = (x, ctx)
    for layer in range(DEPTH):
        last = layer == DEPTH - 1
        zqkv, zgate, zu, zv2, zba, qh, kh, vh = _in_proj(xs, mod[layer], norm1_g[layer], _reorder_w_in(w_in[layer]),
                                                         q_norm_g[layer], k_norm_g[layer], rope_c, rope_s)
        qkv = _dn_prep(zqkv, conv_w[layer])
        o_dn = _deltanet(qkv, zba, dn_a_log[layer], dn_dt_bias[layer])
        if not last:
            i = (layer + 1) // 2
            attn_lat, moe_w = _attention_lat(qh, kh, vh, cast=(
                moe_w1[i].reshape(MOE_EXPERTS * D_MODEL, MOE_D_FF), moe_w3[i].reshape(MOE_EXPERTS * D_MODEL, MOE_D_FF),
                moe_w2[i].reshape(MOE_EXPERTS * MOE_D_FF, D_MODEL)))
        else:
            attn_lat, _ = _attention_lat(qh, kh, vh)
        attns = (attn_lat,)
        if len(xs) == 2:
            attns += (_attention_ctx(qh, kh, vh),)
        elif not last:
            attns = (jnp.concatenate([attns[0], _attention_ctx(qh, kh, vh)], axis=2),)
        mix_args = (xs, mod[layer], o_dn, zgate, attns, zu, zv2, dn_norm_g[layer], sgu_norm_g[layer],
                    sgu_w[layer], sgu_b[layer], w_out[layer].astype(BF16), norm2_g[layer])
        if not last:
            i = layer // 2
            x_mid, h_mid = _mix(*mix_args)
            xs = (_ffn(x_mid, h_mid, mod[layer], ffn_w1[i].astype(BF16), ffn_w3[i].astype(BF16),
                       ffn_w2[i].astype(BF16)),)
        else:
            i = layer // 2
            x_lat, h_lat, logits = _mix(*mix_args, router=(router_w[i], router_b[i]))
            dest, gates, counts = _route(logits.reshape(N_LAT, LANES))
            cnt = counts[0, :MOE_EXPERTS]
            padded = (cnt + MOE_BLOCK - 1) // MOE_BLOCK * MOE_BLOCK
            pad_ends = jnp.cumsum(padded)
            block_expert = jnp.minimum(
                jnp.sum(pad_ends[None, :] <= (jnp.arange(MOE_N_BLOCKS) * MOE_BLOCK)[:, None], axis=1),
                MOE_EXPERTS - 1).astype(jnp.int32)
            lo = jnp.concatenate([pad_ends - padded + cnt, pad_ends[-1:]])
            hi = jnp.concatenate([pad_ends, jnp.full((1,), MOE_ROWS, pad_ends.dtype)])
            pad_bounds = jnp.stack([lo, hi], axis=1).reshape(-1).astype(jnp.int32)
            slot_src = _invert(dest.reshape(N_ASSIGN), pad_bounds)
            tok = lax.shift_right_logical(slot_src, 1)
            gather_rows = jnp.minimum(tok, N_LAT - 1) * SLAB
            scatter_rows = ((slot_src & 1) * Y2_ROWS + tok) * SLAB
            y2 = _experts(block_expert, gather_rows, scatter_rows, h_lat.reshape(N_LAT * SLAB, LANES),
                          moe_w[0].reshape(MOE_EXPERTS, D_MODEL, MOE_D_FF),
                          moe_w[1].reshape(MOE_EXPERTS, D_MODEL, MOE_D_FF),
                          moe_w[2].reshape(MOE_EXPERTS, MOE_D_FF, D_MODEL))
            return _combine(x_lat, gates, mod[layer], final_norm_g, y2)
```

```python
import functools

import jax
import jax.numpy as jnp
import numpy as np
from jax import lax
from jax.experimental import pallas as pl
from jax.experimental.pallas import tpu as pltpu

D_MODEL = 1024
BATCH = 4
SEQ = 4096
DEPTH = 2
GRID_W = 64
CTX_LEN = 256
HEAD_DIM = 64
DN_HEADS = 6
ATTN_HEADS = 6
ATTN_KV_HEADS = 2
ATTN_GROUP = ATTN_HEADS // ATTN_KV_HEADS
MLP_GROUPS = 4
DN_WIDTH = DN_HEADS * HEAD_DIM
ATTN_WIDTH = ATTN_HEADS * HEAD_DIM
ATTN_KV_WIDTH = ATTN_KV_HEADS * HEAD_DIM
MLP_WIDTH = MLP_GROUPS * HEAD_DIM
D_MIX = DN_WIDTH + ATTN_WIDTH + MLP_WIDTH
CONV_K = 3
DN_CHUNK = 64
ATTN_SCALE = HEAD_DIM ** -0.5
LOG2E = 1.4426950408889634
MLP_CHUNK = 128
ROPE_THETA = 10000.0
ROPE_AXIS_DIM = HEAD_DIM // 2
ROPE_PAIRS = ROPE_AXIS_DIM // 2
D_FF = 2816
MOE_EXPERTS = 8
MOE_TOP_K = 2
MOE_D_FF = 3584
MOE_BLOCK = 256
NORM_EPS = 1e-6

LANES = 128
SLAB = 8
TT = SEQ + CTX_LEN
TM = 256
N_TILES = TT // TM
CTX_TILE = N_TILES - 1
CTX_ROW = BATCH
N_DN_STEPS = TT // DN_CHUNK
N_CTX_CHUNKS = CTX_LEN // DN_CHUNK
N_LAT_CHUNKS = SEQ // DN_CHUNK
IN_PAD = 3 * DN_WIDTH + DN_WIDTH + ATTN_WIDTH + 2 * ATTN_KV_WIDTH + 2 * MLP_WIDTH + LANES
N_LAT = BATCH * SEQ
N_ASSIGN = N_LAT * MOE_TOP_K
MOE_N_BLOCKS = -(-(N_ASSIGN + MOE_EXPERTS * (MOE_BLOCK - 1)) // MOE_BLOCK)
MOE_ROWS = MOE_N_BLOCKS * MOE_BLOCK
MOE_FF_SPLIT = 2
MOE_FF_BLK = MOE_D_FF // MOE_FF_SPLIT
assert SLAB * LANES == D_MODEL
ROUTE_TILE = 512
VMEM_LIMIT = 56 * 2 ** 20

F32 = jnp.float32
BF16 = jnp.bfloat16
HI = lax.Precision.HIGHEST


def _cparams(*sem):
    return pltpu.CompilerParams(dimension_semantics=sem, vmem_limit_bytes=VMEM_LIMIT)


def _bdot(a, b):
    return jnp.dot(a.astype(BF16), b.astype(BF16), preferred_element_type=F32)


def _seg_ones():
    r = lax.shift_right_logical(lax.broadcasted_iota(jnp.int32, (LANES, LANES), 0), 6)
    c = lax.shift_right_logical(lax.broadcasted_iota(jnp.int32, (LANES, LANES), 1), 6)
    return (r == c).astype(F32).astype(BF16)


def _split_bf16(x):
    hi = x.astype(BF16)
    return hi, (x - hi.astype(F32)).astype(BF16)


def _seg_sum(y, ones):
    parts = [jnp.dot(y[:, i:i + LANES].astype(BF16), ones, preferred_element_type=F32)
             for i in range(0, y.shape[-1], LANES)]
    return parts[0] if len(parts) == 1 else jnp.concatenate(parts, axis=-1)


def _softplus(x):
    return jnp.maximum(x, 0.0) + jnp.log1p(jnp.exp(-jnp.abs(x)))


def _ada_norm(x, gain, shift, scale):
    y = x * lax.rsqrt(jnp.mean(x * x, axis=-1, keepdims=True) + NORM_EPS) * gain
    return y * (1.0 + scale) + shift


def _mod_index(b, i):
    return (jnp.where(i == CTX_TILE, CTX_ROW, b), 0, 0)


def _mod_kernel(c_ref, w_ref, b_ref, o_ref):
    cond = jax.nn.silu(c_ref[...])
    o_ref[0] = jnp.dot(cond, w_ref[0], precision=HI, preferred_element_type=F32) + b_ref[0]


def _modulation(cond8, mod_w, mod_b):
    nblk = 4
    bn = 6 * D_MODEL // nblk
    out = pl.pallas_call(
        _mod_kernel,
        grid=(DEPTH, nblk),
        in_specs=[pl.BlockSpec((8, D_MODEL), lambda l, j: (0, 0)),
                  pl.BlockSpec((1, D_MODEL, bn), lambda l, j: (l, 0, j)),
                  pl.BlockSpec((1, 1, bn), lambda l, j: (l, 0, j))],
        out_specs=pl.BlockSpec((1, 8, bn), lambda l, j: (l, 0, j)),
        out_shape=jax.ShapeDtypeStruct((DEPTH, 8, 6 * D_MODEL), F32),
        compiler_params=_cparams("arbitrary", "arbitrary"),
        name="modulation",
    )(cond8, mod_w, mod_b.reshape(DEPTH, 1, 6 * D_MODEL))
    return out.reshape(DEPTH, 8, 6, D_MODEL)


_IN_SPLITS = (3 * DN_WIDTH, DN_WIDTH, ATTN_WIDTH, ATTN_KV_WIDTH, ATTN_KV_WIDTH, MLP_WIDTH, MLP_WIDTH, LANES)


def _tile_rows(refs, x_scr):
    if len(refs) == 1:
        return refs[0][0]
    x_ref, ctx_ref = refs
    i = pl.program_id(1)

    @pl.when(i != CTX_TILE)
    def _():
        x_scr[...] = x_ref[0]

    @pl.when(i == CTX_TILE)
    def _():
        x_scr[...] = ctx_ref[0]

    return x_scr[...]


def _row_specs(split):
    if not split:
        return [pl.BlockSpec((1, TM, D_MODEL), lambda b, i: (b, i, 0))]
    return [pl.BlockSpec((1, TM, D_MODEL), lambda b, i: (b, jnp.minimum(i, CTX_TILE - 1), 0)),
            pl.BlockSpec((1, CTX_LEN, D_MODEL), lambda b, i: (b, 0, 0))]


def _in_proj_kernel(*refs, split):
    n_x = 2 if split else 1
    mod_ref, g_ref, w_ref, gq_ref, gk_ref, c_ref, s_ref = refs[n_x:n_x + 7]
    oqkv, ogate, ou, ov2, oba, oq, ok, ov = refs[n_x + 7:n_x + 15]
    m = mod_ref[0]
    x = _tile_rows(refs[:n_x], refs[-1] if split else None)
    h = _ada_norm(x, g_ref[...], m[0:1], m[1:2]).astype(BF16)
    z = jnp.dot(h, w_ref[...], preferred_element_type=F32)
    cols = {}
    off = 0
    for name, width in zip(("qkv", "gate", "q", "k", "v", "u", "v2", "ba"), _IN_SPLITS):
        cols[name] = z[:, off:off + width]
        off += width
    for ref, name in ((oqkv, "qkv"), (ogate, "gate"), (ou, "u"), (ov2, "v2"), (oba, "ba")):
        ref[0] = cols[name].astype(ref.dtype)

    ones = _seg_ones()
    cs, sn = c_ref[...], s_ref[...]
    lane = lax.broadcasted_iota(jnp.int32, (1, LANES), 1)
    first_half = (lane & (2 * ROPE_PAIRS - 1)) < ROPE_PAIRS

    def norm_rope(t, gain):
        y = t * lax.rsqrt(_seg_sum(t * t, ones) * (1.0 / HEAD_DIM) + NORM_EPS) * gain
        partner = jnp.where(first_half, pltpu.roll(y, LANES - ROPE_PAIRS, 1), pltpu.roll(y, ROPE_PAIRS, 1))
        return y * cs + partner * sn

    for pair in range(ATTN_HEADS // 2):
        q2 = norm_rope(cols["q"][:, pair * LANES:(pair + 1) * LANES], gq_ref[...]) * (ATTN_SCALE * LOG2E)
        oq[0, 2 * pair] = q2[:, :HEAD_DIM].astype(oq.dtype)
        oq[0, 2 * pair + 1] = q2[:, HEAD_DIM:].astype(oq.dtype)
    k2 = norm_rope(cols["k"], gk_ref[...])
    v2 = cols["v"].astype(ov.dtype)
    one_cols = jnp.ones((TM, HEAD_DIM), ov.dtype)
    for hd in range(ATTN_KV_HEADS):
        ok[0, hd] = k2[:, hd * HEAD_DIM:(hd + 1) * HEAD_DIM].astype(ok.dtype)
        ov[0, hd] = jnp.concatenate([v2[:, hd * HEAD_DIM:(hd + 1) * HEAD_DIM], one_cols], axis=1)


def _in_proj(xs, mod_l, gain, w_in_r, q_gain, k_gain, rope_c, rope_s):
    split = len(xs) == 2
    row = lambda b, i: (b, i, 0)
    hrow = lambda b, i: (b, 0, i, 0)
    const2 = lambda b, i: (0, 0)
    tile2 = lambda g: jnp.tile(g.reshape(1, HEAD_DIM), (1, 2))
    widths = (3 * DN_WIDTH, DN_WIDTH, MLP_WIDTH, MLP_WIDTH, LANES)
    dts = (BF16,) * 4 + (F32,)
    return pl.pallas_call(
        functools.partial(_in_proj_kernel, split=split),
        grid=(BATCH, N_TILES),
        in_specs=_row_specs(split) + [
            pl.BlockSpec((1, 6, D_MODEL), _mod_index),
            pl.BlockSpec((1, D_MODEL), const2),
            pl.BlockSpec((D_MODEL, IN_PAD), const2),
            pl.BlockSpec((1, LANES), const2),
            pl.BlockSpec((1, LANES), const2),
            pl.BlockSpec((TM, LANES), lambda b, i: (i, 0)),
            pl.BlockSpec((TM, LANES), lambda b, i: (i, 0))],
        out_specs=[pl.BlockSpec((1, TM, w), row) for w in widths] + [
            pl.BlockSpec((1, ATTN_HEADS, TM, HEAD_DIM), hrow),
            pl.BlockSpec((1, ATTN_KV_HEADS, TM, HEAD_DIM), hrow),
            pl.BlockSpec((1, ATTN_KV_HEADS, TM, LANES), hrow)],
        out_shape=[jax.ShapeDtypeStruct((BATCH, TT, w), dt) for w, dt in zip(widths, dts)] + [
            jax.ShapeDtypeStruct((BATCH, ATTN_HEADS, TT, HEAD_DIM), BF16),
            jax.ShapeDtypeStruct((BATCH, ATTN_KV_HEADS, TT, HEAD_DIM), BF16),
            jax.ShapeDtypeStruct((BATCH, ATTN_KV_HEADS, TT, LANES), BF16)],
        scratch_shapes=[pltpu.VMEM((TM, D_MODEL), F32)] if split else [],
        compiler_params=_cparams("parallel", "arbitrary"),
        name="in_proj",
    )(*xs, mod_l, gain.reshape(1, D_MODEL), w_in_r, tile2(q_gain), tile2(k_gain), rope_c, rope_s)


def _dn_prep_kernel(z_ref, w_ref, o_ref):
    j = pl.program_id(1)
    z = z_ref[0].astype(F32)
    w = w_ref[...]
    row = lax.broadcasted_iota(jnp.int32, (TT, 1), 0)
    first = (row == 0) | (row == SEQ)
    last = (row == SEQ - 1) | (row == TT - 1)
    zp = jnp.where(first, 0.0, pltpu.roll(z, 1, 0))
    zn = jnp.where(last, 0.0, pltpu.roll(z, TT - 1, 0))
    y = jax.nn.silu(w[0:1] * zp + w[1:2] * z + w[2:3] * zn)
    n_qk = 2 * DN_WIDTH // LANES
    n_q = DN_WIDTH // LANES

    @pl.when(j < n_qk)
    def _():
        inv = lax.rsqrt(_seg_sum(y * y, _seg_ones()) + NORM_EPS)
        o_ref[0] = (y * inv * jnp.where(j < n_q, HEAD_DIM ** -0.5, 1.0)).astype(o_ref.dtype)

    @pl.when(j >= n_qk)
    def _():
        o_ref[0] = y.astype(o_ref.dtype)


def _dn_prep(zqkv, conv_w):
    nb = 3 * DN_WIDTH // LANES
    return pl.pallas_call(
        _dn_prep_kernel,
        grid=(BATCH, nb),
        in_specs=[pl.BlockSpec((1, TT, LANES), lambda b, j: (b, 0, j)),
                  pl.BlockSpec((CONV_K, LANES), lambda b, j: (0, j))],
        out_specs=pl.BlockSpec((1, TT, LANES), lambda b, j: (b, 0, j)),
        out_shape=jax.ShapeDtypeStruct((BATCH, TT, 3 * DN_WIDTH), BF16),
        compiler_params=_cparams("parallel", "arbitrary"),
        name="dn_prep",
    )(zqkv, conv_w)


DN_PAIRS = DN_HEADS // 2
DN_SUB = 4
LOG2_CHUNK = DN_CHUNK.bit_length() - 1


def _dn_kernel(qkv_ref, zba_ref, a_ref, dt_ref, o_ref, s_scr):
    C, P2 = DN_CHUNK, 2 * DN_CHUNK
    o_ref[...] = jnp.zeros_like(o_ref)
    s_scr[...] = jnp.zeros_like(s_scr)
    neg_decay_rate = -jnp.exp(a_ref[...])
    dt_bias = dt_ref[...]
    lo = lax.broadcasted_iota(jnp.int32, (1, P2), 1) < C

    def stack(a):
        return jnp.concatenate([jnp.where(lo, a, 0.0), jnp.where(lo, 0.0, a)], axis=0)

    def step(i, carry):
        ri = lax.broadcasted_iota(jnp.int32, (P2, P2), 0)
        ci = lax.broadcasted_iota(jnp.int32, (P2, P2), 1)
        ti = lax.broadcasted_iota(jnp.int32, (C, C), 0)
        tj = lax.broadcasted_iota(jnp.int32, (C, C), 1)
        same = lambda sh: lax.shift_right_logical(ri, sh) == lax.shift_right_logical(ci, sh)
        same_head = same(LOG2_CHUNK)
        eye = (ri == ci).astype(F32)

        ch = []
        for sub in range(DN_SUB):
            s = i * DN_SUB + sub
            chunk_of = (jnp.where(s < N_CTX_CHUNKS, N_LAT_CHUNKS + s, s - N_CTX_CHUNKS), N_DN_STEPS - 1 - s)
            for d in range(2):
                r0 = pl.multiple_of(chunk_of[d] * C, C)
                zba = zba_ref[0, pl.ds(r0, C), :]
                beta_all = jax.nn.sigmoid(zba)
                g_all = neg_decay_rate * _softplus(zba + dt_bias)
                tri = (ti >= tj) if d == 0 else (ti <= tj)
                gc_all = jnp.dot(tri.astype(F32), g_all, precision=HI, preferred_element_type=F32)
                gc_t = jnp.concatenate([gc_all, gc_all], axis=0).T
                last = C - 1 if d == 0 else 0
                for p in range(DN_PAIRS):
                    lb = (d * DN_HEADS + 2 * p, d * DN_HEADS + 2 * p + 1)
                    la = (2 * DN_HEADS + lb[0], 2 * DN_HEADS + lb[1])
                    col = lambda t, l: t[:, l:l + 1]
                    q2 = qkv_ref[0, pl.ds(r0, C), p * LANES:(p + 1) * LANES].astype(F32)
                    k2 = qkv_ref[0, pl.ds(r0, C), DN_WIDTH + p * LANES:DN_WIDTH + (p + 1) * LANES].astype(F32)
                    v2 = qkv_ref[0, pl.ds(r0, C), 2 * DN_WIDTH + p * LANES:2 * DN_WIDTH + (p + 1) * LANES].astype(F32)
                    beta2 = jnp.where(lo, col(beta_all, lb[0]), col(beta_all, lb[1]))
                    gc2 = jnp.where(lo, col(gc_all, la[0]), col(gc_all, la[1]))
                    gcol = jnp.concatenate([col(gc_all, la[0]), col(gc_all, la[1])], axis=0)
                    grow = jnp.where(lo, gc_t[la[0]:la[0] + 1, :], gc_t[la[1]:la[1] + 1, :])
                    g_last = jnp.where(lo, gc_all[last:last + 1, la[0]:la[0] + 1],
                                       gc_all[last:last + 1, la[1]:la[1] + 1])
                    e_diff = jnp.exp(gcol - grow)
                    order = (ri >= ci) if d == 0 else (ri <= ci)
                    kb2 = k2 * beta2
                    e_gc = jnp.exp(gc2)
                    ch.append(dict(
                        r0=r0, sub=sub, d=d, p=p, g_last=g_last,
                        dec_incl=jnp.where(same_head & order, e_diff, 0.0),
                        dec_strict=jnp.where(same_head & order & (ri != ci), e_diff, 0.0),
                        kq=jnp.concatenate([stack(kb2), stack(q2)], axis=0).astype(BF16),
                        k=stack(k2).astype(BF16),
                        rhs=jnp.concatenate([stack(v2 * beta2), stack(kb2 * e_gc)], axis=1).astype(BF16),
                        kd=stack(k2 * jnp.exp(g_last - gc2)).astype(BF16),
                        qe=stack(q2 * e_gc)))
        for c in ch:
            kk_qk = lax.dot_general(c["kq"], c["k"], (((1,), (1,)), ((), ())), preferred_element_type=F32)
            c["a"] = kk_qk[:P2] * c["dec_strict"]
            c["attn"] = (kk_qk[P2:] * c["dec_incl"]).astype(BF16)
            c["t"] = eye - jnp.where(same(1), c["a"], 0.0)
        for lvl in range(1, LOG2_CHUNK):
            joins = same(lvl + 1) & ~same(lvl)
            for c in ch:
                c["m"] = _bdot(jnp.where(joins, c["a"], 0.0), c["t"])
            for c in ch:
                c["t"] = c["t"] - _bdot(c["t"], c["m"])
        for c in ch:
            c["uw"] = _bdot(c["t"], c["rhs"]).astype(BF16)
        for c in ch:
            oa = jnp.dot(c["attn"], c["uw"], preferred_element_type=F32)
            c["o0"] = oa[:, :LANES]
            c["qp"] = c["qe"] - oa[:, LANES:]
            c["np"] = lax.dot_general(c["kd"], c["uw"], (((0,), (0,)), ((), ())), preferred_element_type=F32)

        states = [s_scr[j] for j in range(2 * DN_PAIRS)]
        for sub in range(DN_SUB):
            cur = [c for c in ch if c["sub"] == sub]
            for c in cur:
                j = c["d"] * DN_PAIRS + c["p"]
                c["r"] = _bdot(jnp.concatenate([c["qp"], c["np"][:, LANES:]], axis=0), states[j])
            for d in range(2):
                outs = []
                for c in cur:
                    if c["d"] != d:
                        continue
                    j = d * DN_PAIRS + c["p"]
                    o_st = c["o0"] + c["r"][:P2]
                    outs.append(o_st[:C] + o_st[C:])
                    states[j] = states[j] * jnp.exp(c["g_last"]) + c["np"][:, :LANES] - c["r"][P2:]
                    r0 = c["r0"]
                o_ref[0, pl.ds(r0, C), :] += jnp.concatenate(outs, axis=1)
        for j in range(2 * DN_PAIRS):
            s_scr[j] = states[j]
        return carry

    lax.fori_loop(0, N_DN_STEPS // DN_SUB, step, 0)


def _deltanet(qkv, zba, a_log, dt_bias):
    pad = lambda t: jnp.zeros((1, LANES), F32).at[0, 2 * DN_HEADS:4 * DN_HEADS].set(t.reshape(-1))
    return pl.pallas_call(
        _dn_kernel,
        grid=(BATCH,),
        in_specs=[pl.BlockSpec((1, TT, 3 * DN_WIDTH), lambda b: (b, 0, 0)),
                  pl.BlockSpec((1, TT, LANES), lambda b: (b, 0, 0)),
                  pl.BlockSpec((1, LANES), lambda b: (0, 0)),
                  pl.BlockSpec((1, LANES), lambda b: (0, 0))],
        out_specs=pl.BlockSpec((1, TT, DN_WIDTH), lambda b: (b, 0, 0)),
        out_shape=jax.ShapeDtypeStruct((BATCH, TT, DN_WIDTH), F32),
        scratch_shapes=[pltpu.VMEM((2 * DN_PAIRS, LANES, LANES), F32)],
        compiler_params=_cparams("parallel"),
        name="deltanet",
    )(qkv, zba, pad(a_log), pad(dt_bias))


def _rope_tables():
    rows = SEQ // GRID_W
    row = jnp.repeat(jnp.arange(rows, dtype=F32), GRID_W)
    col = jnp.tile(jnp.arange(GRID_W, dtype=F32), rows)
    inv = ROPE_THETA ** (-2.0 * jnp.arange(ROPE_PAIRS, dtype=F32) / ROPE_AXIS_DIM)
    ang = jnp.stack([row[:, None] * inv, col[:, None] * inv], axis=1)
    cos, sin = jnp.cos(ang), jnp.sin(ang)
    c = jnp.concatenate([cos[:, 0], cos[:, 0], cos[:, 1], cos[:, 1]], axis=-1)
    s = jnp.concatenate([-sin[:, 0], sin[:, 0], -sin[:, 1], sin[:, 1]], axis=-1)
    c = jnp.concatenate([c, jnp.ones((CTX_LEN, HEAD_DIM), F32)], axis=0)
    s = jnp.concatenate([s, jnp.zeros((CTX_LEN, HEAD_DIM), F32)], axis=0)
    return jnp.tile(c, (1, 2)), jnp.tile(s, (1, 2))


ATTN_TILING = (128, 8)
ATTN_KB = 512
ATTN_KEY_BLOCKS = tuple((j * ATTN_KB, ATTN_KB) for j in range(SEQ // ATTN_KB)) + ((SEQ, CTX_LEN),)


def _dot_nt(a, b):
    return lax.dot_general(a, b, (((1,), (1,)), ((), ())), preferred_element_type=F32)


def _attn_finish(acc):
    return acc[:, :HEAD_DIM] / acc[:, HEAD_DIM:HEAD_DIM + 1]


def _attn_lat_kernel(q_ref, k_ref, v_ref, *rest, n_cast, tq_sub, n_sub):
    cast_in, o_ref, cast_out = rest[:n_cast], rest[n_cast], rest[n_cast + 1:2 * n_cast + 1]
    s0_scr, s1_scr = rest[2 * n_cast + 1:]
    for src, dst in zip(cast_in, cast_out):
        dst[...] = src[...].astype(dst.dtype)
    rows = ATTN_GROUP * tq_sub
    s_scr = (s0_scr, s1_scr)
    mpart = [None] * n_sub
    acc = [None] * n_sub
    for stage in range(n_sub + 1):
        a, b = stage, stage - 1
        if a < n_sub:
            qa = q_ref[0, :, a * tq_sub:(a + 1) * tq_sub, :].reshape(rows, HEAD_DIM)
        if b >= 0:
            m_b = jnp.max(mpart[b], axis=-1, keepdims=True)
        for k0, kn in ATTN_KEY_BLOCKS:
            if a < n_sub:
                s = _dot_nt(qa, k_ref[0, 0, k0:k0 + kn, :])
                s_scr[a % 2][:, k0:k0 + kn] = s
                blk = functools.reduce(jnp.maximum, [s[:, i:i + LANES] for i in range(0, kn, LANES)])
                mpart[a] = blk if mpart[a] is None else jnp.maximum(mpart[a], blk)
            if b >= 0:
                p = jnp.exp2(s_scr[b % 2][:, k0:k0 + kn] - m_b).astype(BF16)
                pv = jnp.dot(p, v_ref[0, 0, k0:k0 + kn, :], preferred_element_type=F32)
                acc[b] = pv if acc[b] is None else acc[b] + pv
        if b >= 0:
            o = _attn_finish(acc[b]).reshape(ATTN_GROUP, tq_sub, HEAD_DIM)
            o_ref[0, :, b * tq_sub:(b + 1) * tq_sub, :] = o.astype(o_ref.dtype)


def _attn_ctx_kernel(q_ref, k_ref, v_ref, o_ref):
    q = q_ref[0].reshape(ATTN_GROUP * CTX_LEN, HEAD_DIM)
    s = _dot_nt(q, k_ref[0, 0])
    p = jnp.exp2(s - jnp.max(s, axis=-1, keepdims=True)).astype(BF16)
    o = _attn_finish(jnp.dot(p, v_ref[0, 0], preferred_element_type=F32))
    o_ref[0] = o.reshape(ATTN_GROUP, CTX_LEN, HEAD_DIM).astype(o_ref.dtype)


def _attention_lat(qh, kh, vh, cast=()):
    tq_sub, n_sub = ATTN_TILING
    tq = n_sub * tq_sub
    rows = ATTN_GROUP * tq_sub
    nq = SEQ // tq
    n_steps = BATCH * ATTN_KV_HEADS * nq
    step = lambda b, g, i: ((b * ATTN_KV_HEADS + g) * nq + i, 0)
    cast_specs = [pl.BlockSpec((w.shape[0] // n_steps, w.shape[1]), step) for w in cast]
    out = pl.pallas_call(
        functools.partial(_attn_lat_kernel, n_cast=len(cast), tq_sub=tq_sub, n_sub=n_sub),
        grid=(BATCH, ATTN_KV_HEADS, nq),
        in_specs=[pl.BlockSpec((1, ATTN_GROUP, tq, HEAD_DIM), lambda b, g, i: (b, g, i, 0)),
                  pl.BlockSpec((1, 1, TT, HEAD_DIM), lambda b, g, i: (b, g, 0, 0)),
                  pl.BlockSpec((1, 1, TT, LANES), lambda b, g, i: (b, g, 0, 0))] + cast_specs,
        out_specs=[pl.BlockSpec((1, ATTN_GROUP, tq, HEAD_DIM), lambda b, g, i: (b, g, i, 0))] + cast_specs,
        out_shape=[jax.ShapeDtypeStruct((BATCH, ATTN_HEADS, SEQ, HEAD_DIM), BF16)]
        + [jax.ShapeDtypeStruct(w.shape, BF16) for w in cast],
        scratch_shapes=[pltpu.VMEM((rows, TT), F32), pltpu.VMEM((rows, TT), F32)],
        compiler_params=_cparams("arbitrary", "arbitrary", "arbitrary"),
        name="attention_lat",
    )(qh, kh, vh, *cast)
    return out[0], tuple(out[1:])


def _attention_ctx(qh, kh, vh):
    ctx_blk = SEQ // CTX_LEN
    return pl.pallas_call(
        _attn_ctx_kernel,
        grid=(BATCH, ATTN_KV_HEADS),
        in_specs=[pl.BlockSpec((1, ATTN_GROUP, CTX_LEN, HEAD_DIM), lambda b, g: (b, g, ctx_blk, 0)),
                  pl.BlockSpec((1, 1, CTX_LEN, HEAD_DIM), lambda b, g: (b, g, ctx_blk, 0)),
                  pl.BlockSpec((1, 1, CTX_LEN, LANES), lambda b, g: (b, g, ctx_blk, 0))],
        out_specs=pl.BlockSpec((1, ATTN_GROUP, CTX_LEN, HEAD_DIM), lambda b, g: (b, g, 0, 0)),
        out_shape=jax.ShapeDtypeStruct((BATCH, ATTN_HEADS, CTX_LEN, HEAD_DIM), BF16),
        compiler_params=_cparams("parallel", "arbitrary"),
        name="attention_ctx",
    )(qh, kh, vh)


def _mix_kernel(*refs, moe, split):
    n = 2 if split else 1
    x_refs, refs = refs[:n], refs[n:]
    mod_ref, o_dn_ref, zgate_ref = refs[:3]
    attn_refs, refs = refs[3:3 + n], refs[3 + n:]
    zu_ref, zv_ref, dn_g_ref, sgu_g_ref, sgu_w_ref, sgu_b_ref, w_out_ref, g2_ref = refs[:8]
    rest = refs[8:]
    if moe:
        rw_ref, rb_ref, ox, oh, ologit = rest[:5]
    else:
        ox, oh = rest[:2]
    ones = _seg_ones()
    m = mod_ref[0]
    o = o_dn_ref[0]
    dn = o * lax.rsqrt(_seg_sum(o * o, ones) * (1.0 / HEAD_DIM) + NORM_EPS) * dn_g_ref[...]
    dn = dn * jax.nn.silu(zgate_ref[0].astype(F32))
    heads = lambda ref: jnp.concatenate([ref[0, h] for h in range(ATTN_HEADS)], axis=1)
    if split:
        x_scr, at_scr = rest[-2:]
        is_ctx = pl.program_id(1) == CTX_TILE

        @pl.when(jnp.logical_not(is_ctx))
        def _():
            at_scr[...] = heads(attn_refs[0])

        @pl.when(is_ctx)
        def _():
            at_scr[...] = heads(attn_refs[1])

        at = at_scr[...]
        x_in = _tile_rows(x_refs, x_scr)
    else:
        at = heads(attn_refs[0])
        x_in = x_refs[0][0]
    u = jax.nn.gelu(zu_ref[0].astype(F32))
    v = jax.nn.gelu(zv_ref[0].astype(F32))
    v = (v * lax.rsqrt(_seg_sum(v * v, ones) * (1.0 / HEAD_DIM) + NORM_EPS) * sgu_g_ref[...]).astype(BF16)
    gd = MLP_WIDTH // MLP_GROUPS
    chunks = []
    for ci in range(TM // MLP_CHUNK):
        vc = v[ci * MLP_CHUNK:(ci + 1) * MLP_CHUNK]
        mixed = jnp.concatenate(
            [jnp.dot(sgu_w_ref[g], vc[:, g * gd:(g + 1) * gd], preferred_element_type=F32)
             for g in range(MLP_GROUPS)], axis=1)
        chunks.append(mixed + sgu_b_ref[...])
    sg = u * jnp.concatenate(chunks, axis=0)
    mixed_all = jnp.concatenate([dn.astype(BF16), at, sg.astype(BF16)], axis=1)
    y = jnp.dot(mixed_all, w_out_ref[...], preferred_element_type=F32)
    x = x_in + m[2:3] * y
    ox[0] = x
    h = _ada_norm(x, g2_ref[...], m[3:4], m[4:5])
    if moe:
        for kk in range(D_MODEL // LANES):
            oh[0, pl.ds(kk, TM, stride=SLAB), :] = h[:, kk * LANES:(kk + 1) * LANES]
    else:
        oh[0] = h.astype(oh.dtype)
    if moe:
        h_hi, h_lo = _split_bf16(h)
        r_hi, r_lo = _split_bf16(rw_ref[...])
        ologit[0] = (jnp.dot(h_hi, r_hi, preferred_element_type=F32) + jnp.dot(h_hi, r_lo, preferred_element_type=F32)
                     + jnp.dot(h_lo, r_hi, preferred_element_type=F32)) + rb_ref[...]


def _mix(xs, mod_l, o_dn, zgate, attns, zu, zv2, dn_g, sgu_g, sgu_w, sgu_b, w_out, gain2, router=None):
    moe = router is not None
    split = len(xs) == 2
    assert len(attns) == len(xs)
    nt = N_TILES - 1 if moe else N_TILES
    rows = nt * TM
    row = lambda b, i: (b, i, 0)
    orow = row
    const2 = lambda b, i: (0, 0)
    gd = MLP_WIDTH // MLP_GROUPS
    sgu_b_x = jnp.repeat(sgu_b.T, gd, axis=1)
    if split:
        attn_specs = [pl.BlockSpec((1, ATTN_HEADS, TM, HEAD_DIM), lambda b, i: (b, 0, jnp.minimum(i, CTX_TILE - 1), 0)),
                      pl.BlockSpec((1, ATTN_HEADS, CTX_LEN, HEAD_DIM), lambda b, i: (b, 0, 0, 0))]
    else:
        attn_specs = [pl.BlockSpec((1, ATTN_HEADS, TM, HEAD_DIM), lambda b, i: (b, 0, i, 0))]
    in_specs = _row_specs(split) + [
                pl.BlockSpec((1, 6, D_MODEL), _mod_index),
                pl.BlockSpec((1, TM, DN_WIDTH), row),
                pl.BlockSpec((1, TM, DN_WIDTH), row)] + attn_specs + [
                pl.BlockSpec((1, TM, MLP_WIDTH), row),
                pl.BlockSpec((1, TM, MLP_WIDTH), row),
                pl.BlockSpec((1, DN_WIDTH), const2),
                pl.BlockSpec((1, MLP_WIDTH), const2),
                pl.BlockSpec((MLP_GROUPS, MLP_CHUNK, MLP_CHUNK), lambda b, i: (0, 0, 0)),
                pl.BlockSpec((MLP_CHUNK, MLP_WIDTH), const2),
                pl.BlockSpec((D_MIX, D_MODEL), const2),
                pl.BlockSpec((1, D_MODEL), const2)]
    args = [*xs, mod_l, o_dn, zgate, *attns, zu, zv2, jnp.tile(dn_g.reshape(1, HEAD_DIM), (1, DN_HEADS)),
            sgu_g.reshape(1, MLP_WIDTH), sgu_w.astype(BF16), sgu_b_x, w_out, gain2.reshape(1, D_MODEL)]
    if moe:
        h_spec = pl.BlockSpec((1, TM * SLAB, LANES), orow)
        h_shape = jax.ShapeDtypeStruct((BATCH, rows * SLAB, LANES), F32)
    else:
        h_spec = pl.BlockSpec((1, TM, D_MODEL), orow)
        h_shape = jax.ShapeDtypeStruct((BATCH, rows, D_MODEL), BF16)
    out_specs = [pl.BlockSpec((1, TM, D_MODEL), orow), h_spec]
    out_shape = [jax.ShapeDtypeStruct((BATCH, rows, D_MODEL), F32), h_shape]
    if moe:
        rw, rb = router
        in_specs += [pl.BlockSpec((D_MODEL, LANES), const2), pl.BlockSpec((1, LANES), const2)]
        args += [jnp.pad(rw, ((0, 0), (0, LANES - MOE_EXPERTS))),
                 jnp.pad(rb.reshape(1, MOE_EXPERTS), ((0, 0), (0, LANES - MOE_EXPERTS)))]
        out_specs.append(pl.BlockSpec((1, TM, LANES), orow))
        out_shape.append(jax.ShapeDtypeStruct((BATCH, rows, LANES), F32))
    return pl.pallas_call(
        functools.partial(_mix_kernel, moe=moe, split=split),
        grid=(BATCH, nt),
        in_specs=in_specs, out_specs=out_specs, out_shape=out_shape,
        scratch_shapes=[pltpu.VMEM((TM, D_MODEL), F32), pltpu.VMEM((TM, ATTN_WIDTH), BF16)] if split else [],
        compiler_params=_cparams("parallel", "arbitrary"),
        name="mix_moe" if moe else "mix",
    )(*args)


FFN_SPLIT = 1
FFN_BLK = D_FF // FFN_SPLIT


def _ffn_kernel(x_ref, h_ref, mod_ref, w1_ref, w3_ref, w2_ref, o_ref):
    h = h_ref[0]
    y = None
    for c in range(FFN_SPLIT):
        sl = slice(c * FFN_BLK, (c + 1) * FFN_BLK)
        a = jnp.dot(h, w1_ref[:, sl], preferred_element_type=F32)
        b = jnp.dot(h, w3_ref[:, sl], preferred_element_type=F32)
        part = jnp.dot((jax.nn.silu(a) * b).astype(BF16), w2_ref[sl, :], preferred_element_type=F32)
        y = part if y is None else y + part
    o_ref[0] = x_ref[0] + mod_ref[0][5:6] * y


def _ffn(x_all, h_all, mod_l, w1, w3, w2):
    row = lambda b, i: (b, i, 0)
    const2 = lambda b, i: (0, 0)
    return pl.pallas_call(
        _ffn_kernel,
        grid=(BATCH, N_TILES),
        in_specs=[pl.BlockSpec((1, TM, D_MODEL), row),
                  pl.BlockSpec((1, TM, D_MODEL), row),
                  pl.BlockSpec((1, 6, D_MODEL), _mod_index),
                  pl.BlockSpec((D_MODEL, D_FF), const2),
                  pl.BlockSpec((D_MODEL, D_FF), const2),
                  pl.BlockSpec((D_FF, D_MODEL), const2)],
        out_specs=pl.BlockSpec((1, TM, D_MODEL), row),
        out_shape=jax.ShapeDtypeStruct((BATCH, TT, D_MODEL), F32),
        compiler_params=_cparams("parallel", "arbitrary"),
        name="ffn",
    )(x_all, h_all, mod_l, w1, w3, w2)


def _route_kernel(logit_ref, dest_ref, gate_ref, count_ref, run_scr):
    phase, t = pl.program_id(0), pl.program_id(1)

    @pl.when((phase == 0) & (t == 0))
    def _():
        run_scr[...] = jnp.zeros_like(run_scr)

    lane = lax.broadcasted_iota(jnp.int32, (ROUTE_TILE, LANES), 1).astype(F32)
    logits = jnp.where(lane < MOE_EXPERTS, logit_ref[...], -jnp.inf)
    m1 = jnp.max(logits, axis=-1, keepdims=True)
    e1 = jnp.min(jnp.where(logits == m1, lane, float(LANES)), axis=-1, keepdims=True)
    rest = jnp.where(lane == e1, -jnp.inf, logits)
    m2 = jnp.max(rest, axis=-1, keepdims=True)
    e2 = jnp.min(jnp.where(rest == m2, lane, float(LANES)), axis=-1, keepdims=True)
    hot1 = (lane == e1).astype(F32)
    hot2 = (lane == e2).astype(F32)
    hot = hot1 + hot2
    tile_count = jnp.sum(hot, axis=0, keepdims=True)

    @pl.when(phase == 0)
    def _():
        run_scr[0:1] = run_scr[0:1] + tile_count

    @pl.when((phase == 1) & (t == 0))
    def _():
        counts = jnp.broadcast_to(run_scr[0:1], (8, LANES))
        count_ref[...] = counts.astype(jnp.int32)
        padded = jnp.ceil(counts * (1.0 / MOE_BLOCK)) * MOE_BLOCK
        ei = lax.broadcasted_iota(jnp.int32, (LANES, LANES), 0)
        ej = lax.broadcasted_iota(jnp.int32, (LANES, LANES), 1)
        before = (ei < ej).astype(F32)
        run_scr[1:2] = jnp.dot(padded, before, precision=HI, preferred_element_type=F32)[0:1]

    @pl.when(phase == 1)
    def _():
        ri = lax.broadcasted_iota(jnp.int32, (ROUTE_TILE, ROUTE_TILE), 0)
        rj = lax.broadcasted_iota(jnp.int32, (ROUTE_TILE, ROUTE_TILE), 1)
        earlier = (ri > rj).astype(BF16)
        within = jnp.dot(earlier, hot.astype(BF16), preferred_element_type=F32)
        pos = within + run_scr[1:2]
        d1 = jnp.sum(pos * hot1, axis=-1, keepdims=True)
        d2 = jnp.sum(pos * hot2, axis=-1, keepdims=True)
        dest_ref[...] = jnp.concatenate([d1, d2], axis=1).astype(jnp.int32)
        w2 = jnp.exp(m2 - m1)
        gate_ref[...] = jnp.concatenate([1.0 / (1.0 + w2), w2 / (1.0 + w2)], axis=1)
        run_scr[1:2] = run_scr[1:2] + tile_count


def _route(logits):
    nt = N_LAT // ROUTE_TILE
    return pl.pallas_call(
        _route_kernel,
        grid=(2, nt),
        in_specs=[pl.BlockSpec((ROUTE_TILE, LANES), lambda p, t: (t, 0))],
        out_specs=[pl.BlockSpec((ROUTE_TILE, MOE_TOP_K), lambda p, t: (t * p, 0)),
                   pl.BlockSpec((ROUTE_TILE, MOE_TOP_K), lambda p, t: (t * p, 0)),
                   pl.BlockSpec((8, LANES), lambda p, t: (0, 0))],
        out_shape=[jax.ShapeDtypeStruct((N_LAT, MOE_TOP_K), jnp.int32),
                   jax.ShapeDtypeStruct((N_LAT, MOE_TOP_K), F32),
                   jax.ShapeDtypeStruct((8, LANES), jnp.int32)],
        scratch_shapes=[pltpu.VMEM((8, LANES), F32)],
        compiler_params=_cparams("arbitrary", "arbitrary"),
        name="moe_route",
    )(logits)


N_PAD_SLOTS = MOE_ROWS - N_ASSIGN
Y2_ROWS = N_LAT + N_PAD_SLOTS // MOE_TOP_K


def _invert_kernel(dest_ref, bounds_ref, slot_ref):
    def real(a, c):
        slot_ref[dest_ref[a]] = a
        return c

    lax.fori_loop(0, N_ASSIGN, real, 0, unroll=8)

    def pad_range(e, count):
        def pad(s, cnt):
            slot_ref[s] = N_ASSIGN + cnt
            return cnt + 1

        return lax.fori_loop(bounds_ref[2 * e], bounds_ref[2 * e + 1], pad, count)

    lax.fori_loop(0, MOE_EXPERTS + 1, pad_range, 0)


def _invert(dest_flat, pad_bounds):
    return pl.pallas_call(
        _invert_kernel,
        grid_spec=pltpu.PrefetchScalarGridSpec(
            num_scalar_prefetch=2, grid=(1,), in_specs=[],
            out_specs=pl.BlockSpec(memory_space=pltpu.SMEM)),
        out_shape=jax.ShapeDtypeStruct((MOE_ROWS,), jnp.int32),
        compiler_params=_cparams("arbitrary"),
        name="moe_invert",
    )(dest_flat, pad_bounds)


LAST_BLOCK = MOE_N_BLOCKS - 1
MOE_NBUF = 3


def _swiglu_half(x, w1_ref, w3_ref, w2_ref):
    a = jnp.dot(x, w1_ref[0], preferred_element_type=F32)
    b = jnp.dot(x, w3_ref[0], preferred_element_type=F32)
    return jnp.dot((jax.nn.silu(a) * b).astype(BF16), w2_ref[0], preferred_element_type=F32)


def _expert_gather_kernel(be_ref, slot_ref, h_hbm, w1_ref, w3_ref, w2_ref, xs_ref, yb_ref, xbuf, sem):
    del be_ref
    j = pl.program_id(0)
    cur = j % MOE_NBUF

    def gather(blk, buf):
        base = blk * MOE_BLOCK
        for r in range(MOE_BLOCK):
            row = pl.multiple_of(slot_ref[base + r], SLAB)
            pltpu.make_async_copy(h_hbm.at[pl.ds(row, SLAB)], xbuf.at[buf, pl.ds(r * SLAB, SLAB)],
                                  sem.at[buf]).start()

    def wait(buf):
        pltpu.make_async_copy(h_hbm.at[pl.ds(0, MOE_BLOCK * SLAB)], xbuf.at[buf], sem.at[buf]).wait()

    @pl.when(j == 0)
    def _():
        gather(0, 0)
        gather(1, 1)

    wait(cur)
    for kk in range(D_MODEL // LANES):
        xs_ref[:, kk * LANES:(kk + 1) * LANES] = xbuf[cur, pl.ds(kk, MOE_BLOCK, stride=SLAB), :]
    gather(jnp.minimum(j + 2, LAST_BLOCK), (j + 2) % MOE_NBUF)
    yb_ref[...] = _swiglu_half(xs_ref[...].astype(BF16), w1_ref, w3_ref, w2_ref)

    @pl.when(j == LAST_BLOCK)
    def _():
        wait((j + 1) % MOE_NBUF)
        wait((j + 2) % MOE_NBUF)


def _expert_scatter_kernel(be_ref, slot_ref, xs_ref, w1_ref, w3_ref, w2_ref, yb_ref, y2_hbm, obuf, sem):
    del be_ref
    j = pl.program_id(0)
    cur = j % MOE_NBUF
    prev = (j + MOE_NBUF - 1) % MOE_NBUF

    def scatter(blk, buf):
        base = blk * MOE_BLOCK
        for r in range(MOE_BLOCK):
            row = pl.multiple_of(slot_ref[base + r], SLAB)
            pltpu.make_async_copy(obuf.at[buf, pl.ds(r * SLAB, SLAB)], y2_hbm.at[pl.ds(row, SLAB)],
                                  sem.at[buf]).start()

    def wait(buf):
        pltpu.make_async_copy(obuf.at[buf], y2_hbm.at[pl.ds(0, MOE_BLOCK * SLAB)], sem.at[buf]).wait()

    def compute():
        out = yb_ref[...] + _swiglu_half(xs_ref[...].astype(BF16), w1_ref, w3_ref, w2_ref)
        for kk in range(D_MODEL // LANES):
            obuf[cur, pl.ds(kk, MOE_BLOCK, stride=SLAB), :] = out[:, kk * LANES:(kk + 1) * LANES]

    @pl.when(j >= MOE_NBUF)
    def _():
        wait(cur)

    @pl.when(j == 0)
    def _():
        compute()

    @pl.when(j > 0)
    def _():
        scatter(j - 1, prev)
        compute()

    @pl.when(j == LAST_BLOCK)
    def _():
        scatter(j, cur)
        for b in range(MOE_NBUF):
            wait(b)


def _experts(block_expert, gather_rows, scatter_rows, h_lat, w1, w3, w2):
    def w_specs(c):
        return [pl.BlockSpec((1, D_MODEL, MOE_FF_BLK), lambda j, be, sl: (be[j], 0, c)),
                pl.BlockSpec((1, D_MODEL, MOE_FF_BLK), lambda j, be, sl: (be[j], 0, c)),
                pl.BlockSpec((1, MOE_FF_BLK, D_MODEL), lambda j, be, sl: (be[j], c, 0))]

    blk = pl.BlockSpec((MOE_BLOCK, D_MODEL), lambda j, be, sl: (j, 0))
    xs, yb = pl.pallas_call(
        _expert_gather_kernel,
        grid_spec=pltpu.PrefetchScalarGridSpec(
            num_scalar_prefetch=2, grid=(MOE_N_BLOCKS,),
            in_specs=[pl.BlockSpec(memory_space=pl.ANY)] + w_specs(0),
            out_specs=[blk, blk],
            scratch_shapes=[pltpu.VMEM((MOE_NBUF, MOE_BLOCK * SLAB, LANES), F32),
                            pltpu.SemaphoreType.DMA((MOE_NBUF,))]),
        out_shape=[jax.ShapeDtypeStruct((MOE_ROWS, D_MODEL), F32)] * 2,
        compiler_params=_cparams("arbitrary"),
        name="moe_experts_gather",
    )(block_expert, gather_rows, h_lat, w1, w3, w2)
    return pl.pallas_call(
        _expert_scatter_kernel,
        grid_spec=pltpu.PrefetchScalarGridSpec(
            num_scalar_prefetch=2, grid=(MOE_N_BLOCKS,),
            in_specs=[blk] + w_specs(1) + [blk],
            out_specs=pl.BlockSpec(memory_space=pl.ANY),
            scratch_shapes=[pltpu.VMEM((MOE_NBUF, MOE_BLOCK * SLAB, LANES), F32),
                            pltpu.SemaphoreType.DMA((MOE_NBUF,))]),
        out_shape=jax.ShapeDtypeStruct((MOE_TOP_K * Y2_ROWS * SLAB, LANES), F32),
        compiler_params=_cparams("arbitrary"),
        name="moe_experts_scatter",
    )(block_expert, scatter_rows, xs, w1, w3, w2, yb)


def _combine_kernel(x_ref, gate_ref, mod_ref, g_ref, ya_ref, yb_ref, o_ref):
    gates = gate_ref[0]
    rows = lambda ref: jnp.concatenate(
        [ref[pl.ds(kk, TM, stride=SLAB), :] for kk in range(D_MODEL // LANES)], axis=1)
    y = rows(ya_ref) * gates[:, 0:1] + rows(yb_ref) * gates[:, 1:2]
    x = x_ref[0] + mod_ref[0][5:6] * y
    o_ref[0] = x * lax.rsqrt(jnp.mean(x * x, axis=-1, keepdims=True) + NORM_EPS) * g_ref[...]


def _combine(x_lat, gates, mod_l, final_g, y2):
    nt = SEQ // TM
    return pl.pallas_call(
        _combine_kernel,
        grid=(BATCH, nt),
        in_specs=[pl.BlockSpec((1, TM, D_MODEL), lambda b, i: (b, i, 0)),
                  pl.BlockSpec((1, TM, MOE_TOP_K), lambda b, i: (b, i, 0)),
                  pl.BlockSpec((1, 6, D_MODEL), lambda b, i: (b, 0, 0)),
                  pl.BlockSpec((1, D_MODEL), lambda b, i: (0, 0)),
                  pl.BlockSpec((TM * SLAB, LANES), lambda b, i: (b * nt + i, 0)),
                  pl.BlockSpec((TM * SLAB, LANES), lambda b, i: (Y2_ROWS // TM + b * nt + i, 0))],
        out_specs=pl.BlockSpec((1, TM, D_MODEL), lambda b, i: (b, i, 0)),
        out_shape=jax.ShapeDtypeStruct((BATCH, SEQ, D_MODEL), F32),
        compiler_params=_cparams("parallel", "arbitrary"),
        name="moe_combine",
    )(x_lat, gates.reshape(BATCH, SEQ, MOE_TOP_K), mod_l, final_g.reshape(1, D_MODEL), y2, y2)


def _reorder_w_in(w):
    s = np.cumsum((3 * DN_WIDTH, DN_WIDTH, 2 * DN_HEADS, 2 * DN_HEADS, ATTN_WIDTH, ATTN_KV_WIDTH, ATTN_KV_WIDTH,
                   MLP_WIDTH, MLP_WIDTH)).tolist()
    ba = w[:, s[1]:s[3]]
    return jnp.concatenate([w[:, :s[1]], w[:, s[3]:], ba,
                            jnp.zeros((D_MODEL, LANES - 4 * DN_HEADS), w.dtype)], axis=1).astype(BF16)


def kernel(x, c, ctx, c_ctx, mod_w, mod_b, norm1_g, norm2_g, w_in, conv_w, dn_a_log, dn_dt_bias, dn_norm_g,
           q_norm_g, k_norm_g, sgu_norm_g, sgu_w, sgu_b, w_out, ffn_w1, ffn_w3, ffn_w2, router_w, router_b,
           moe_w1, moe_w3, moe_w2, final_norm_g):
    assert DEPTH == 2 and x.shape == (BATCH, SEQ, D_MODEL) and ctx.shape == (BATCH, CTX_LEN, D_MODEL)
    cond8 = jnp.concatenate([c, c_ctx[None], jnp.zeros((8 - BATCH - 1, D_MODEL), F32)], axis=0)
    mod = _modulation(cond8, mod_w, mod_b)
    rope_c, rope_s = _rope_tables()
    xs = (x, ctx)
    for layer in range(DEPTH):
        last = layer == DEPTH - 1
        zqkv, zgate, zu, zv2, zba, qh, kh, vh = _in_proj(xs, mod[layer], norm1_g[layer], _reorder_w_in(w_in[layer]),
                                                         q_norm_g[layer], k_norm_g[layer], rope_c, rope_s)
        qkv = _dn_prep(zqkv, conv_w[layer])
        o_dn = _deltanet(qkv, zba, dn_a_log[layer], dn_dt_bias[layer])
        if not last:
            i = (layer + 1) // 2
            attn_lat, moe_w = _attention_lat(qh, kh, vh, cast=(
                moe_w1[i].reshape(MOE_EXPERTS * D_MODEL, MOE_D_FF), moe_w3[i].reshape(MOE_EXPERTS * D_MODEL, MOE_D_FF),
                moe_w2[i].reshape(MOE_EXPERTS * MOE_D_FF, D_MODEL)))
        else:
            attn_lat, _ = _attention_lat(qh, kh, vh)
        attns = (attn_lat,)
        if len(xs) == 2:
            attns += (_attention_ctx(qh, kh, vh),)
        elif not last:
            attns = (jnp.concatenate([attns[0], _attention_ctx(qh, kh, vh)], axis=2),)
        mix_args = (xs, mod[layer], o_dn, zgate, attns, zu, zv2, dn_norm_g[layer], sgu_norm_g[layer],
                    sgu_w[layer], sgu_b[layer], w_out[layer].astype(BF16), norm2_g[layer])
        if not last:
            i = layer // 2
            x_mid, h_mid = _mix(*mix_args)
            xs = (_ffn(x_mid, h_mid, mod[layer], ffn_w1[i].astype(BF16), ffn_w3[i].astype(BF16),
                       ffn_w2[i].astype(BF16)),)
        else:
            i = layer // 2
            x_lat, h_lat, logits = _mix(*mix_args, router=(router_w[i], router_b[i]))
            dest, gates, counts = _route(logits.reshape(N_LAT, LANES))
            cnt = counts[0, :MOE_EXPERTS]
            padded = (cnt + MOE_BLOCK - 1) // MOE_BLOCK * MOE_BLOCK
            pad_ends = jnp.cumsum(padded)
            block_expert = jnp.minimum(
                jnp.sum(pad_ends[None, :] <= (jnp.arange(MOE_N_BLOCKS) * MOE_BLOCK)[:, None], axis=1),
                MOE_EXPERTS - 1).astype(jnp.int32)
            lo = jnp.concatenate([pad_ends - padded + cnt, pad_ends[-1:]])
            hi = jnp.concatenate([pad_ends, jnp.full((1,), MOE_ROWS, pad_ends.dtype)])
            pad_bounds = jnp.stack([lo, hi], axis=1).reshape(-1).astype(jnp.int32)
            slot_src = _invert(dest.reshape(N_ASSIGN), pad_bounds)
            tok = lax.shift_right_logical(slot_src, 1)
            gather_rows = jnp.minimum(tok, N_LAT - 1) * SLAB
            scatter_rows = ((slot_src & 1) * Y2_ROWS + tok) * SLAB
            y2 = _experts(block_expert, gather_rows, scatter_rows, h_lat.reshape(N_LAT * SLAB, LANES),
                          moe_w[0].reshape(MOE_EXPERTS, D_MODEL, MOE_D_FF),
                          moe_w[1].reshape(MOE_EXPERTS, D_MODEL, MOE_D_FF),
                          moe_w[2].reshape(MOE_EXPERTS, MOE_D_FF, D_MODEL))
            return _combine(x_lat, gates, mod[layer], final_norm_g, y2)
```

```python
import functools

import jax
import jax.numpy as jnp
import numpy as np
from jax import lax
from jax.experimental import pallas as pl
from jax.experimental.pallas import tpu as pltpu

D_MODEL = 1024
BATCH = 4
SEQ = 4096
DEPTH = 2
GRID_W = 64
CTX_LEN = 256
HEAD_DIM = 64
DN_HEADS = 6
ATTN_HEADS = 6
ATTN_KV_HEADS = 2
ATTN_GROUP = ATTN_HEADS // ATTN_KV_HEADS
MLP_GROUPS = 4
DN_WIDTH = DN_HEADS * HEAD_DIM
ATTN_WIDTH = ATTN_HEADS * HEAD_DIM
ATTN_KV_WIDTH = ATTN_KV_HEADS * HEAD_DIM
MLP_WIDTH = MLP_GROUPS * HEAD_DIM
D_MIX = DN_WIDTH + ATTN_WIDTH + MLP_WIDTH
CONV_K = 3
DN_CHUNK = 64
ATTN_SCALE = HEAD_DIM ** -0.5
LOG2E = 1.4426950408889634
MLP_CHUNK = 128
ROPE_THETA = 10000.0
ROPE_AXIS_DIM = HEAD_DIM // 2
ROPE_PAIRS = ROPE_AXIS_DIM // 2
D_FF = 2816
MOE_EXPERTS = 8
MOE_TOP_K = 2
MOE_D_FF = 3584
MOE_BLOCK = 256
NORM_EPS = 1e-6

LANES = 128
SLAB = 8
TT = SEQ + CTX_LEN
TM = 256
N_TILES = TT // TM
CTX_TILE = N_TILES - 1
CTX_ROW = BATCH
N_DN_STEPS = TT // DN_CHUNK
N_CTX_CHUNKS = CTX_LEN // DN_CHUNK
N_LAT_CHUNKS = SEQ // DN_CHUNK
IN_PAD = 3 * DN_WIDTH + DN_WIDTH + ATTN_WIDTH + 2 * ATTN_KV_WIDTH + 2 * MLP_WIDTH + LANES
N_LAT = BATCH * SEQ
N_ASSIGN = N_LAT * MOE_TOP_K
MOE_N_BLOCKS = -(-(N_ASSIGN + MOE_EXPERTS * (MOE_BLOCK - 1)) // MOE_BLOCK)
MOE_ROWS = MOE_N_BLOCKS * MOE_BLOCK
MOE_FF_SPLIT = 2
MOE_FF_BLK = MOE_D_FF // MOE_FF_SPLIT
assert SLAB * LANES == D_MODEL
ROUTE_TILE = 512
VMEM_LIMIT = 56 * 2 ** 20

F32 = jnp.float32
BF16 = jnp.bfloat16
HI = lax.Precision.HIGHEST


def _cparams(*sem):
    return pltpu.CompilerParams(dimension_semantics=sem, vmem_limit_bytes=VMEM_LIMIT)


def _bdot(a, b):
    return jnp.dot(a.astype(BF16), b.astype(BF16), preferred_element_type=F32)


def _seg_ones():
    r = lax.shift_right_logical(lax.broadcasted_iota(jnp.int32, (LANES, LANES), 0), 6)
    c = lax.shift_right_logical(lax.broadcasted_iota(jnp.int32, (LANES, LANES), 1), 6)
    return (r == c).astype(F32).astype(BF16)


def _split_bf16(x):
    hi = x.astype(BF16)
    return hi, (x - hi.astype(F32)).astype(BF16)


def _seg_sum(y, ones):
    parts = [jnp.dot(y[:, i:i + LANES].astype(BF16), ones, preferred_element_type=F32)
             for i in range(0, y.shape[-1], LANES)]
    return parts[0] if len(parts) == 1 else jnp.concatenate(parts, axis=-1)


def _softplus(x):
    return jnp.maximum(x, 0.0) + jnp.log1p(jnp.exp(-jnp.abs(x)))


def _ada_norm(x, gain, shift, scale):
    y = x * lax.rsqrt(jnp.mean(x * x, axis=-1, keepdims=True) + NORM_EPS) * gain
    return y * (1.0 + scale) + shift


def _mod_index(b, i):
    return (jnp.where(i == CTX_TILE, CTX_ROW, b), 0, 0)


def _mod_kernel(c_ref, w_ref, b_ref, o_ref):
    cond = jax.nn.silu(c_ref[...])
    o_ref[0] = jnp.dot(cond, w_ref[0], precision=HI, preferred_element_type=F32) + b_ref[0]


def _modulation(cond8, mod_w, mod_b):
    nblk = 4
    bn = 6 * D_MODEL // nblk
    out = pl.pallas_call(
        _mod_kernel,
        grid=(DEPTH, nblk),
        in_specs=[pl.BlockSpec((8, D_MODEL), lambda l, j: (0, 0)),
                  pl.BlockSpec((1, D_MODEL, bn), lambda l, j: (l, 0, j)),
                  pl.BlockSpec((1, 1, bn), lambda l, j: (l, 0, j))],
        out_specs=pl.BlockSpec((1, 8, bn), lambda l, j: (l, 0, j)),
        out_shape=jax.ShapeDtypeStruct((DEPTH, 8, 6 * D_MODEL), F32),
        compiler_params=_cparams("arbitrary", "arbitrary"),
        name="modulation",
    )(cond8, mod_w, mod_b.reshape(DEPTH, 1, 6 * D_MODEL))
    return out.reshape(DEPTH, 8, 6, D_MODEL)


_IN_SPLITS = (3 * DN_WIDTH, DN_WIDTH, ATTN_WIDTH, ATTN_KV_WIDTH, ATTN_KV_WIDTH, MLP_WIDTH, MLP_WIDTH, LANES)


def _tile_rows(refs, x_scr):
    if len(refs) == 1:
        return refs[0][0]
    x_ref, ctx_ref = refs
    i = pl.program_id(1)

    @pl.when(i != CTX_TILE)
    def _():
        x_scr[...] = x_ref[0]

    @pl.when(i == CTX_TILE)
    def _():
        x_scr[...] = ctx_ref[0]

    return x_scr[...]


def _row_specs(split):
    if not split:
        return [pl.BlockSpec((1, TM, D_MODEL), lambda b, i: (b, i, 0))]
    return [pl.BlockSpec((1, TM, D_MODEL), lambda b, i: (b, jnp.minimum(i, CTX_TILE - 1), 0)),
            pl.BlockSpec((1, CTX_LEN, D_MODEL), lambda b, i: (b, 0, 0))]


def _in_proj_kernel(*refs, split):
    n_x = 2 if split else 1
    mod_ref, g_ref, w_ref, gq_ref, gk_ref, c_ref, s_ref = refs[n_x:n_x + 7]
    oqkv, ogate, ou, ov2, oba, oq, ok, ov = refs[n_x + 7:n_x + 15]
    m = mod_ref[0]
    x = _tile_rows(refs[:n_x], refs[-1] if split else None)
    h = _ada_norm(x, g_ref[...], m[0:1], m[1:2]).astype(BF16)
    z = jnp.dot(h, w_ref[...], preferred_element_type=F32)
    cols = {}
    off = 0
    for name, width in zip(("qkv", "gate", "q", "k", "v", "u", "v2", "ba"), _IN_SPLITS):
        cols[name] = z[:, off:off + width]
        off += width
    for ref, name in ((oqkv, "qkv"), (ogate, "gate"), (ou, "u"), (ov2, "v2"), (oba, "ba")):
        ref[0] = cols[name].astype(ref.dtype)

    ones = _seg_ones()
    cs, sn = c_ref[...], s_ref[...]
    lane = lax.broadcasted_iota(jnp.int32, (1, LANES), 1)
    first_half = (lane & (2 * ROPE_PAIRS - 1)) < ROPE_PAIRS

    def norm_rope(t, gain):
        y = t * lax.rsqrt(_seg_sum(t * t, ones) * (1.0 / HEAD_DIM) + NORM_EPS) * gain
        partner = jnp.where(first_half, pltpu.roll(y, LANES - ROPE_PAIRS, 1), pltpu.roll(y, ROPE_PAIRS, 1))
        return y * cs + partner * sn

    for pair in range(ATTN_HEADS // 2):
        q2 = norm_rope(cols["q"][:, pair * LANES:(pair + 1) * LANES], gq_ref[...]) * (ATTN_SCALE * LOG2E)
        oq[0, 2 * pair] = q2[:, :HEAD_DIM].astype(oq.dtype)
        oq[0, 2 * pair + 1] = q2[:, HEAD_DIM:].astype(oq.dtype)
    k2 = norm_rope(cols["k"], gk_ref[...])
    v2 = cols["v"].astype(ov.dtype)
    one_cols = jnp.ones((TM, HEAD_DIM), ov.dtype)
    for hd in range(ATTN_KV_HEADS):
        ok[0, hd] = k2[:, hd * HEAD_DIM:(hd + 1) * HEAD_DIM].astype(ok.dtype)
        ov[0, hd] = jnp.concatenate([v2[:, hd * HEAD_DIM:(hd + 1) * HEAD_DIM], one_cols], axis=1)


def _in_proj(xs, mod_l, gain, w_in_r, q_gain, k_gain, rope_c, rope_s):
    split = len(xs) == 2
    row = lambda b, i: (b, i, 0)
    hrow = lambda b, i: (b, 0, i, 0)
    const2 = lambda b, i: (0, 0)
    tile2 = lambda g: jnp.tile(g.reshape(1, HEAD_DIM), (1, 2))
    widths = (3 * DN_WIDTH, DN_WIDTH, MLP_WIDTH, MLP_WIDTH, LANES)
    dts = (BF16,) * 4 + (F32,)
    return pl.pallas_call(
        functools.partial(_in_proj_kernel, split=split),
        grid=(BATCH, N_TILES),
        in_specs=_row_specs(split) + [
            pl.BlockSpec((1, 6, D_MODEL), _mod_index),
            pl.BlockSpec((1, D_MODEL), const2),
            pl.BlockSpec((D_MODEL, IN_PAD), const2),
            pl.BlockSpec((1, LANES), const2),
            pl.BlockSpec((1, LANES), const2),
            pl.BlockSpec((TM, LANES), lambda b, i: (i, 0)),
            pl.BlockSpec((TM, LANES), lambda b, i: (i, 0))],
        out_specs=[pl.BlockSpec((1, TM, w), row) for w in widths] + [
            pl.BlockSpec((1, ATTN_HEADS, TM, HEAD_DIM), hrow),
            pl.BlockSpec((1, ATTN_KV_HEADS, TM, HEAD_DIM), hrow),
            pl.BlockSpec((1, ATTN_KV_HEADS, TM, LANES), hrow)],
        out_shape=[jax.ShapeDtypeStruct((BATCH, TT, w), dt) for w, dt in zip(widths, dts)] + [
            jax.ShapeDtypeStruct((BATCH, ATTN_HEADS, TT, HEAD_DIM), BF16),
            jax.ShapeDtypeStruct((BATCH, ATTN_KV_HEADS, TT, HEAD_DIM), BF16),
            jax.ShapeDtypeStruct((BATCH, ATTN_KV_HEADS, TT, LANES), BF16)],
        scratch_shapes=[pltpu.VMEM((TM, D_MODEL), F32)] if split else [],
        compiler_params=_cparams("parallel", "arbitrary"),
        name="in_proj",
    )(*xs, mod_l, gain.reshape(1, D_MODEL), w_in_r, tile2(q_gain), tile2(k_gain), rope_c, rope_s)


def _dn_prep_kernel(z_ref, w_ref, o_ref):
    j = pl.program_id(1)
    z = z_ref[0].astype(F32)
    w = w_ref[...]
    row = lax.broadcasted_iota(jnp.int32, (TT, 1), 0)
    first = (row == 0) | (row == SEQ)
    last = (row == SEQ - 1) | (row == TT - 1)
    zp = jnp.where(first, 0.0, pltpu.roll(z, 1, 0))
    zn = jnp.where(last, 0.0, pltpu.roll(z, TT - 1, 0))
    y = jax.nn.silu(w[0:1] * zp + w[1:2] * z + w[2:3] * zn)
    n_qk = 2 * DN_WIDTH // LANES
    n_q = DN_WIDTH // LANES

    @pl.when(j < n_qk)
    def _():
        inv = lax.rsqrt(_seg_sum(y * y, _seg_ones()) + NORM_EPS)
        o_ref[0] = (y * inv * jnp.where(j < n_q, HEAD_DIM ** -0.5, 1.0)).astype(o_ref.dtype)

    @pl.when(j >= n_qk)
    def _():
        o_ref[0] = y.astype(o_ref.dtype)


def _dn_prep(zqkv, conv_w):
    nb = 3 * DN_WIDTH // LANES
    return pl.pallas_call(
        _dn_prep_kernel,
        grid=(BATCH, nb),
        in_specs=[pl.BlockSpec((1, TT, LANES), lambda b, j: (b, 0, j)),
                  pl.BlockSpec((CONV_K, LANES), lambda b, j: (0, j))],
        out_specs=pl.BlockSpec((1, TT, LANES), lambda b, j: (b, 0, j)),
        out_shape=jax.ShapeDtypeStruct((BATCH, TT, 3 * DN_WIDTH), BF16),
        compiler_params=_cparams("parallel", "arbitrary"),
        name="dn_prep",
    )(zqkv, conv_w)


DN_PAIRS = DN_HEADS // 2
DN_SUB = 4
LOG2_CHUNK = DN_CHUNK.bit_length() - 1


def _dn_kernel(qkv_ref, zba_ref, a_ref, dt_ref, o_ref, s_scr):
    C, P2 = DN_CHUNK, 2 * DN_CHUNK
    o_ref[...] = jnp.zeros_like(o_ref)
    s_scr[...] = jnp.zeros_like(s_scr)
    neg_decay_rate = -jnp.exp(a_ref[...])
    dt_bias = dt_ref[...]
    lo = lax.broadcasted_iota(jnp.int32, (1, P2), 1) < C

    def stack(a):
        return jnp.concatenate([jnp.where(lo, a, 0.0), jnp.where(lo, 0.0, a)], axis=0)

    def step(i, carry):
        ri = lax.broadcasted_iota(jnp.int32, (P2, P2), 0)
        ci = lax.broadcasted_iota(jnp.int32, (P2, P2), 1)
        ti = lax.broadcasted_iota(jnp.int32, (C, C), 0)
        tj = lax.broadcasted_iota(jnp.int32, (C, C), 1)
        same = lambda sh: lax.shift_right_logical(ri, sh) == lax.shift_right_logical(ci, sh)
        same_head = same(LOG2_CHUNK)
        eye = (ri == ci).astype(F32)

        ch = []
        for sub in range(DN_SUB):
            s = i * DN_SUB + sub
            chunk_of = (jnp.where(s < N_CTX_CHUNKS, N_LAT_CHUNKS + s, s - N_CTX_CHUNKS), N_DN_STEPS - 1 - s)
            for d in range(2):
                r0 = pl.multiple_of(chunk_of[d] * C, C)
                zba = zba_ref[0, pl.ds(r0, C), :]
                beta_all = jax.nn.sigmoid(zba)
                g_all = neg_decay_rate * _softplus(zba + dt_bias)
                tri = (ti >= tj) if d == 0 else (ti <= tj)
                gc_all = jnp.dot(tri.astype(F32), g_all, precision=HI, preferred_element_type=F32)
                gc_t = jnp.concatenate([gc_all, gc_all], axis=0).T
                last = C - 1 if d == 0 else 0
                for p in range(DN_PAIRS):
                    lb = (d * DN_HEADS + 2 * p, d * DN_HEADS + 2 * p + 1)
                    la = (2 * DN_HEADS + lb[0], 2 * DN_HEADS + lb[1])
                    col = lambda t, l: t[:, l:l + 1]
                    q2 = qkv_ref[0, pl.ds(r0, C), p * LANES:(p + 1) * LANES].astype(F32)
                    k2 = qkv_ref[0, pl.ds(r0, C), DN_WIDTH + p * LANES:DN_WIDTH + (p + 1) * LANES].astype(F32)
                    v2 = qkv_ref[0, pl.ds(r0, C), 2 * DN_WIDTH + p * LANES:2 * DN_WIDTH + (p + 1) * LANES].astype(F32)
                    beta2 = jnp.where(lo, col(beta_all, lb[0]), col(beta_all, lb[1]))
                    gc2 = jnp.where(lo, col(gc_all, la[0]), col(gc_all, la[1]))
                    gcol = jnp.concatenate([col(gc_all, la[0]), col(gc_all, la[1])], axis=0)
                    grow = jnp.where(lo, gc_t[la[0]:la[0] + 1, :], gc_t[la[1]:la[1] + 1, :])
                    g_last = jnp.where(lo, gc_all[last:last + 1, la[0]:la[0] + 1],
                                       gc_all[last:last + 1, la[1]:la[1] + 1])
                    e_diff = jnp.exp(gcol - grow)
                    order = (ri >= ci) if d == 0 else (ri <= ci)
                    kb2 = k2 * beta2
                    e_gc = jnp.exp(gc2)
                    ch.append(dict(
                        r0=r0, sub=sub, d=d, p=p, g_last=g_last,
                        dec_incl=jnp.where(same_head & order, e_diff, 0.0),
                        dec_strict=jnp.where(same_head & order & (ri != ci), e_diff, 0.0),
                        kq=jnp.concatenate([stack(kb2), stack(q2)], axis=0).astype(BF16),
                        k=stack(k2).astype(BF16),
                        rhs=jnp.concatenate([stack(v2 * beta2), stack(kb2 * e_gc)], axis=1).astype(BF16),
                        kd=stack(k2 * jnp.exp(g_last - gc2)).astype(BF16),
                        qe=stack(q2 * e_gc)))
        for c in ch:
            kk_qk = lax.dot_general(c["kq"], c["k"], (((1,), (1,)), ((), ())), preferred_element_type=F32)
            c["a"] = kk_qk[:P2] * c["dec_strict"]
            c["attn"] = (kk_qk[P2:] * c["dec_incl"]).astype(BF16)
            c["t"] = eye - jnp.where(same(1), c["a"], 0.0)
        for lvl in range(1, LOG2_CHUNK):
            joins = same(lvl + 1) & ~same(lvl)
            for c in ch:
                c["m"] = _bdot(jnp.where(joins, c["a"], 0.0), c["t"])
            for c in ch:
                c["t"] = c["t"] - _bdot(c["t"], c["m"])
        for c in ch:
            c["uw"] = _bdot(c["t"], c["rhs"]).astype(BF16)
        for c in ch:
            oa = jnp.dot(c["attn"], c["uw"], preferred_element_type=F32)
            c["o0"] = oa[:, :LANES]
            c["qp"] = c["qe"] - oa[:, LANES:]
            c["np"] = lax.dot_general(c["kd"], c["uw"], (((0,), (0,)), ((), ())), preferred_element_type=F32)

        states = [s_scr[j] for j in range(2 * DN_PAIRS)]
        for sub in range(DN_SUB):
            cur = [c for c in ch if c["sub"] == sub]
            for c in cur:
                j = c["d"] * DN_PAIRS + c["p"]
                c["r"] = _bdot(jnp.concatenate([c["qp"], c["np"][:, LANES:]], axis=0), states[j])
            for d in range(2):
                outs = []
                for c in cur:
                    if c["d"] != d:
                        continue
                    j = d * DN_PAIRS + c["p"]
                    o_st = c["o0"] + c["r"][:P2]
                    outs.append(o_st[:C] + o_st[C:])
                    states[j] = states[j] * jnp.exp(c["g_last"]) + c["np"][:, :LANES] - c["r"][P2:]
                    r0 = c["r0"]
                o_ref[0, pl.ds(r0, C), :] += jnp.concatenate(outs, axis=1)
        for j in range(2 * DN_PAIRS):
            s_scr[j] = states[j]
        return carry

    lax.fori_loop(0, N_DN_STEPS // DN_SUB, step, 0)


def _deltanet(qkv, zba, a_log, dt_bias):
    pad = lambda t: jnp.zeros((1, LANES), F32).at[0, 2 * DN_HEADS:4 * DN_HEADS].set(t.reshape(-1))
    return pl.pallas_call(
        _dn_kernel,
        grid=(BATCH,),
        in_specs=[pl.BlockSpec((1, TT, 3 * DN_WIDTH), lambda b: (b, 0, 0)),
                  pl.BlockSpec((1, TT, LANES), lambda b: (b, 0, 0)),
                  pl.BlockSpec((1, LANES), lambda b: (0, 0)),
                  pl.BlockSpec((1, LANES), lambda b: (0, 0))],
        out_specs=pl.BlockSpec((1, TT, DN_WIDTH), lambda b: (b, 0, 0)),
        out_shape=jax.ShapeDtypeStruct((BATCH, TT, DN_WIDTH), F32),
        scratch_shapes=[pltpu.VMEM((2 * DN_PAIRS, LANES, LANES), F32)],
        compiler_params=_cparams("parallel"),
        name="deltanet",
    )(qkv, zba, pad(a_log), pad(dt_bias))


def _rope_tables():
    rows = SEQ // GRID_W
    row = jnp.repeat(jnp.arange(rows, dtype=F32), GRID_W)
    col = jnp.tile(jnp.arange(GRID_W, dtype=F32), rows)
    inv = ROPE_THETA ** (-2.0 * jnp.arange(ROPE_PAIRS, dtype=F32) / ROPE_AXIS_DIM)
    ang = jnp.stack([row[:, None] * inv, col[:, None] * inv], axis=1)
    cos, sin = jnp.cos(ang), jnp.sin(ang)
    c = jnp.concatenate([cos[:, 0], cos[:, 0], cos[:, 1], cos[:, 1]], axis=-1)
    s = jnp.concatenate([-sin[:, 0], sin[:, 0], -sin[:, 1], sin[:, 1]], axis=-1)
    c = jnp.concatenate([c, jnp.ones((CTX_LEN, HEAD_DIM), F32)], axis=0)
    s = jnp.concatenate([s, jnp.zeros((CTX_LEN, HEAD_DIM), F32)], axis=0)
    return jnp.tile(c, (1, 2)), jnp.tile(s, (1, 2))


ATTN_TILING = (128, 8)
ATTN_KB = 512
ATTN_KEY_BLOCKS = tuple((j * ATTN_KB, ATTN_KB) for j in range(SEQ // ATTN_KB)) + ((SEQ, CTX_LEN),)


def _dot_nt(a, b):
    return lax.dot_general(a, b, (((1,), (1,)), ((), ())), preferred_element_type=F32)


def _attn_finish(acc):
    return acc[:, :HEAD_DIM] / acc[:, HEAD_DIM:HEAD_DIM + 1]


def _attn_lat_kernel(q_ref, k_ref, v_ref, *rest, n_cast, tq_sub, n_sub):
    cast_in, o_ref, cast_out = rest[:n_cast], rest[n_cast], rest[n_cast + 1:2 * n_cast + 1]
    s0_scr, s1_scr = rest[2 * n_cast + 1:]
    for src, dst in zip(cast_in, cast_out):
        dst[...] = src[...].astype(dst.dtype)
    rows = ATTN_GROUP * tq_sub
    s_scr = (s0_scr, s1_scr)
    mpart = [None] * n_sub
    acc = [None] * n_sub
    for stage in range(n_sub + 1):
        a, b = stage, stage - 1
        if a < n_sub:
            qa = q_ref[0, :, a * tq_sub:(a + 1) * tq_sub, :].reshape(rows, HEAD_DIM)
        if b >= 0:
            m_b = jnp.max(mpart[b], axis=-1, keepdims=True)
        for k0, kn in ATTN_KEY_BLOCKS:
            if a < n_sub:
                s = _dot_nt(qa, k_ref[0, 0, k0:k0 + kn, :])
                s_scr[a % 2][:, k0:k0 + kn] = s
                blk = functools.reduce(jnp.maximum, [s[:, i:i + LANES] for i in range(0, kn, LANES)])
                mpart[a] = blk if mpart[a] is None else jnp.maximum(mpart[a], blk)
            if b >= 0:
                p = jnp.exp2(s_scr[b % 2][:, k0:k0 + kn] - m_b).astype(BF16)
                pv = jnp.dot(p, v_ref[0, 0, k0:k0 + kn, :], preferred_element_type=F32)
                acc[b] = pv if acc[b] is None else acc[b] + pv
        if b >= 0:
            o = _attn_finish(acc[b]).reshape(ATTN_GROUP, tq_sub, HEAD_DIM)
            o_ref[0, :, b * tq_sub:(b + 1) * tq_sub, :] = o.astype(o_ref.dtype)


def _attn_ctx_kernel(q_ref, k_ref, v_ref, o_ref):
    q = q_ref[0].reshape(ATTN_GROUP * CTX_LEN, HEAD_DIM)
    s = _dot_nt(q, k_ref[0, 0])
    p = jnp.exp2(s - jnp.max(s, axis=-1, keepdims=True)).astype(BF16)
    o = _attn_finish(jnp.dot(p, v_ref[0, 0], preferred_element_type=F32))
    o_ref[0] = o.reshape(ATTN_GROUP, CTX_LEN, HEAD_DIM).astype(o_ref.dtype)


def _attention_lat(qh, kh, vh, cast=()):
    tq_sub, n_sub = ATTN_TILING
    tq = n_sub * tq_sub
    rows = ATTN_GROUP * tq_sub
    nq = SEQ // tq
    n_steps = BATCH * ATTN_KV_HEADS * nq
    step = lambda b, g, i: ((b * ATTN_KV_HEADS + g) * nq + i, 0)
    cast_specs = [pl.BlockSpec((w.shape[0] // n_steps, w.shape[1]), step) for w in cast]
    out = pl.pallas_call(
        functools.partial(_attn_lat_kernel, n_cast=len(cast), tq_sub=tq_sub, n_sub=n_sub),
        grid=(BATCH, ATTN_KV_HEADS, nq),
        in_specs=[pl.BlockSpec((1, ATTN_GROUP, tq, HEAD_DIM), lambda b, g, i: (b, g, i, 0)),
                  pl.BlockSpec((1, 1, TT, HEAD_DIM), lambda b, g, i: (b, g, 0, 0)),
                  pl.BlockSpec((1, 1, TT, LANES), lambda b, g, i: (b, g, 0, 0))] + cast_specs,
        out_specs=[pl.BlockSpec((1, ATTN_GROUP, tq, HEAD_DIM), lambda b, g, i: (b, g, i, 0))] + cast_specs,
        out_shape=[jax.ShapeDtypeStruct((BATCH, ATTN_HEADS, SEQ, HEAD_DIM), BF16)]
        + [jax.ShapeDtypeStruct(w.shape, BF16) for w in cast],
        scratch_shapes=[pltpu.VMEM((rows, TT), F32), pltpu.VMEM((rows, TT), F32)],
        compiler_params=_cparams("arbitrary", "arbitrary", "arbitrary"),
        name="attention_lat",
    )(qh, kh, vh, *cast)
    return out[0], tuple(out[1:])


def _attention_ctx(qh, kh, vh):
    ctx_blk = SEQ // CTX_LEN
    return pl.pallas_call(
        _attn_ctx_kernel,
        grid=(BATCH, ATTN_KV_HEADS),
        in_specs=[pl.BlockSpec((1, ATTN_GROUP, CTX_LEN, HEAD_DIM), lambda b, g: (b, g, ctx_blk, 0)),
                  pl.BlockSpec((1, 1, CTX_LEN, HEAD_DIM), lambda b, g: (b, g, ctx_blk, 0)),
                  pl.BlockSpec((1, 1, CTX_LEN, LANES), lambda b, g: (b, g, ctx_blk, 0))],
        out_specs=pl.BlockSpec((1, ATTN_GROUP, CTX_LEN, HEAD_DIM), lambda b, g: (b, g, 0, 0)),
        out_shape=jax.ShapeDtypeStruct((BATCH, ATTN_HEADS, CTX_LEN, HEAD_DIM), BF16),
        compiler_params=_cparams("parallel", "arbitrary"),
        name="attention_ctx",
    )(qh, kh, vh)


def _mix_kernel(*refs, moe, split):
    n = 2 if split else 1
    x_refs, refs = refs[:n], refs[n:]
    mod_ref, o_dn_ref, zgate_ref = refs[:3]
    attn_refs, refs = refs[3:3 + n], refs[3 + n:]
    zu_ref, zv_ref, dn_g_ref, sgu_g_ref, sgu_w_ref, sgu_b_ref, w_out_ref, g2_ref = refs[:8]
    rest = refs[8:]
    if moe:
        rw_ref, rb_ref, ox, oh, ologit = rest[:5]
    else:
        ox, oh = rest[:2]
    ones = _seg_ones()
    m = mod_ref[0]
    o = o_dn_ref[0]
    dn = o * lax.rsqrt(_seg_sum(o * o, ones) * (1.0 / HEAD_DIM) + NORM_EPS) * dn_g_ref[...]
    dn = dn * jax.nn.silu(zgate_ref[0].astype(F32))
    heads = lambda ref: jnp.concatenate([ref[0, h] for h in range(ATTN_HEADS)], axis=1)
    if split:
        x_scr, at_scr = rest[-2:]
        is_ctx = pl.program_id(1) == CTX_TILE

        @pl.when(jnp.logical_not(is_ctx))
        def _():
            at_scr[...] = heads(attn_refs[0])

        @pl.when(is_ctx)
        def _():
            at_scr[...] = heads(attn_refs[1])

        at = at_scr[...]
        x_in = _tile_rows(x_refs, x_scr)
    else:
        at = heads(attn_refs[0])
        x_in = x_refs[0][0]
    u = jax.nn.gelu(zu_ref[0].astype(F32))
    v = jax.nn.gelu(zv_ref[0].astype(F32))
    v = (v * lax.rsqrt(_seg_sum(v * v, ones) * (1.0 / HEAD_DIM) + NORM_EPS) * sgu_g_ref[...]).astype(BF16)
    gd = MLP_WIDTH // MLP_GROUPS
    chunks = []
    for ci in range(TM // MLP_CHUNK):
        vc = v[ci * MLP_CHUNK:(ci + 1) * MLP_CHUNK]
        mixed = jnp.concatenate(
            [jnp.dot(sgu_w_ref[g], vc[:, g * gd:(g + 1) * gd], preferred_element_type=F32)
             for g in range(MLP_GROUPS)], axis=1)
        chunks.append(mixed + sgu_b_ref[...])
    sg = u * jnp.concatenate(chunks, axis=0)
    mixed_all = jnp.concatenate([dn.astype(BF16), at, sg.astype(BF16)], axis=1)
    y = jnp.dot(mixed_all, w_out_ref[...], preferred_element_type=F32)
    x = x_in + m[2:3] * y
    ox[0] = x
    h = _ada_norm(x, g2_ref[...], m[3:4], m[4:5])
    if moe:
        for kk in range(D_MODEL // LANES):
            oh[0, pl.ds(kk, TM, stride=SLAB), :] = h[:, kk * LANES:(kk + 1) * LANES]
    else:
        oh[0] = h.astype(oh.dtype)
    if moe:
        h_hi, h_lo = _split_bf16(h)
        r_hi, r_lo = _split_bf16(rw_ref[...])
        ologit[0] = (jnp.dot(h_hi, r_hi, preferred_element_type=F32) + jnp.dot(h_hi, r_lo, preferred_element_type=F32)
                     + jnp.dot(h_lo, r_hi, preferred_element_type=F32)) + rb_ref[...]


def _mix(xs, mod_l, o_dn, zgate, attns, zu, zv2, dn_g, sgu_g, sgu_w, sgu_b, w_out, gain2, router=None):
    moe = router is not None
    split = len(xs) == 2
    assert len(attns) == len(xs)
    nt = N_TILES - 1 if moe else N_TILES
    rows = nt * TM
    row = lambda b, i: (b, i, 0)
    orow = row
    const2 = lambda b, i: (0, 0)
    gd = MLP_WIDTH // MLP_GROUPS
    sgu_b_x = jnp.repeat(sgu_b.T, gd, axis=1)
    if split:
        attn_specs = [pl.BlockSpec((1, ATTN_HEADS, TM, HEAD_DIM), lambda b, i: (b, 0, jnp.minimum(i, CTX_TILE - 1), 0)),
                      pl.BlockSpec((1, ATTN_HEADS, CTX_LEN, HEAD_DIM), lambda b, i: (b, 0, 0, 0))]
    else:
        attn_specs = [pl.BlockSpec((1, ATTN_HEADS, TM, HEAD_DIM), lambda b, i: (b, 0, i, 0))]
    in_specs = _row_specs(split) + [
                pl.BlockSpec((1, 6, D_MODEL), _mod_index),
                pl.BlockSpec((1, TM, DN_WIDTH), row),
                pl.BlockSpec((1, TM, DN_WIDTH), row)] + attn_specs + [
                pl.BlockSpec((1, TM, MLP_WIDTH), row),
                pl.BlockSpec((1, TM, MLP_WIDTH), row),
                pl.BlockSpec((1, DN_WIDTH), const2),
                pl.BlockSpec((1, MLP_WIDTH), const2),
                pl.BlockSpec((MLP_GROUPS, MLP_CHUNK, MLP_CHUNK), lambda b, i: (0, 0, 0)),
                pl.BlockSpec((MLP_CHUNK, MLP_WIDTH), const2),
                pl.BlockSpec((D_MIX, D_MODEL), const2),
                pl.BlockSpec((1, D_MODEL), const2)]
    args = [*xs, mod_l, o_dn, zgate, *attns, zu, zv2, jnp.tile(dn_g.reshape(1, HEAD_DIM), (1, DN_HEADS)),
            sgu_g.reshape(1, MLP_WIDTH), sgu_w.astype(BF16), sgu_b_x, w_out, gain2.reshape(1, D_MODEL)]
    if moe:
        h_spec = pl.BlockSpec((1, TM * SLAB, LANES), orow)
        h_shape = jax.ShapeDtypeStruct((BATCH, rows * SLAB, LANES), F32)
    else:
        h_spec = pl.BlockSpec((1, TM, D_MODEL), orow)
        h_shape = jax.ShapeDtypeStruct((BATCH, rows, D_MODEL), BF16)
    out_specs = [pl.BlockSpec((1, TM, D_MODEL), orow), h_spec]
    out_shape = [jax.ShapeDtypeStruct((BATCH, rows, D_MODEL), F32), h_shape]
    if moe:
        rw, rb = router
        in_specs += [pl.BlockSpec((D_MODEL, LANES), const2), pl.BlockSpec((1, LANES), const2)]
        args += [jnp.pad(rw, ((0, 0), (0, LANES - MOE_EXPERTS))),
                 jnp.pad(rb.reshape(1, MOE_EXPERTS), ((0, 0), (0, LANES - MOE_EXPERTS)))]
        out_specs.append(pl.BlockSpec((1, TM, LANES), orow))
        out_shape.append(jax.ShapeDtypeStruct((BATCH, rows, LANES), F32))
    return pl.pallas_call(
        functools.partial(_mix_kernel, moe=moe, split=split),
        grid=(BATCH, nt),
        in_specs=in_specs, out_specs=out_specs, out_shape=out_shape,
        scratch_shapes=[pltpu.VMEM((TM, D_MODEL), F32), pltpu.VMEM((TM, ATTN_WIDTH), BF16)] if split else [],
        compiler_params=_cparams("parallel", "arbitrary"),
        name="mix_moe" if moe else "mix",
    )(*args)


FFN_SPLIT = 1
FFN_BLK = D_FF // FFN_SPLIT


def _ffn_kernel(x_ref, h_ref, mod_ref, w1_ref, w3_ref, w2_ref, o_ref):
    h = h_ref[0]
    y = None
    for c in range(FFN_SPLIT):
        sl = slice(c * FFN_BLK, (c + 1) * FFN_BLK)
        a = jnp.dot(h, w1_ref[:, sl], preferred_element_type=F32)
        b = jnp.dot(h, w3_ref[:, sl], preferred_element_type=F32)
        part = jnp.dot((jax.nn.silu(a) * b).astype(BF16), w2_ref[sl, :], preferred_element_type=F32)
        y = part if y is None else y + part
    o_ref[0] = x_ref[0] + mod_ref[0][5:6] * y


def _ffn(x_all, h_all, mod_l, w1, w3, w2):
    row = lambda b, i: (b, i, 0)
    const2 = lambda b, i: (0, 0)
    return pl.pallas_call(
        _ffn_kernel,
        grid=(BATCH, N_TILES),
        in_specs=[pl.BlockSpec((1, TM, D_MODEL), row),
                  pl.BlockSpec((1, TM, D_MODEL), row),
                  pl.BlockSpec((1, 6, D_MODEL), _mod_index),
                  pl.BlockSpec((D_MODEL, D_FF), const2),
                  pl.BlockSpec((D_MODEL, D_FF), const2),
                  pl.BlockSpec((D_FF, D_MODEL), const2)],
        out_specs=pl.BlockSpec((1, TM, D_MODEL), row),
        out_shape=jax.ShapeDtypeStruct((BATCH, TT, D_MODEL), F32),
        compiler_params=_cparams("parallel", "arbitrary"),
        name="ffn",
    )(x_all, h_all, mod_l, w1, w3, w2)


def _route_kernel(logit_ref, dest_ref, gate_ref, count_ref, run_scr):
    phase, t = pl.program_id(0), pl.program_id(1)

    @pl.when((phase == 0) & (t == 0))
    def _():
        run_scr[...] = jnp.zeros_like(run_scr)

    lane = lax.broadcasted_iota(jnp.int32, (ROUTE_TILE, LANES), 1).astype(F32)
    logits = jnp.where(lane < MOE_EXPERTS, logit_ref[...], -jnp.inf)
    m1 = jnp.max(logits, axis=-1, keepdims=True)
    e1 = jnp.min(jnp.where(logits == m1, lane, float(LANES)), axis=-1, keepdims=True)
    rest = jnp.where(lane == e1, -jnp.inf, logits)
    m2 = jnp.max(rest, axis=-1, keepdims=True)
    e2 = jnp.min(jnp.where(rest == m2, lane, float(LANES)), axis=-1, keepdims=True)
    hot1 = (lane == e1).astype(F32)
    hot2 = (lane == e2).astype(F32)
    hot = hot1 + hot2
    tile_count = jnp.sum(hot, axis=0, keepdims=True)

    @pl.when(phase == 0)
    def _():
        run_scr[0:1] = run_scr[0:1] + tile_count

    @pl.when((phase == 1) & (t == 0))
    def _():
        counts = jnp.broadcast_to(run_scr[0:1], (8, LANES))
        count_ref[...] = counts.astype(jnp.int32)
        padded = jnp.ceil(counts * (1.0 / MOE_BLOCK)) * MOE_BLOCK
        ei = lax.broadcasted_iota(jnp.int32, (LANES, LANES), 0)
        ej = lax.broadcasted_iota(jnp.int32, (LANES, LANES), 1)
        before = (ei < ej).astype(F32)
        run_scr[1:2] = jnp.dot(padded, before, precision=HI, preferred_element_type=F32)[0:1]

    @pl.when(phase == 1)
    def _():
        ri = lax.broadcasted_iota(jnp.int32, (ROUTE_TILE, ROUTE_TILE), 0)
        rj = lax.broadcasted_iota(jnp.int32, (ROUTE_TILE, ROUTE_TILE), 1)
        earlier = (ri > rj).astype(BF16)
        within = jnp.dot(earlier, hot.astype(BF16), preferred_element_type=F32)
        pos = within + run_scr[1:2]
        d1 = jnp.sum(pos * hot1, axis=-1, keepdims=True)
        d2 = jnp.sum(pos * hot2, axis=-1, keepdims=True)
        dest_ref[...] = jnp.concatenate([d1, d2], axis=1).astype(jnp.int32)
        w2 = jnp.exp(m2 - m1)
        gate_ref[...] = jnp.concatenate([1.0 / (1.0 + w2), w2 / (1.0 + w2)], axis=1)
        run_scr[1:2] = run_scr[1:2] + tile_count


def _route(logits):
    nt = N_LAT // ROUTE_TILE
    return pl.pallas_call(
        _route_kernel,
        grid=(2, nt),
        in_specs=[pl.BlockSpec((ROUTE_TILE, LANES), lambda p, t: (t, 0))],
        out_specs=[pl.BlockSpec((ROUTE_TILE, MOE_TOP_K), lambda p, t: (t * p, 0)),
                   pl.BlockSpec((ROUTE_TILE, MOE_TOP_K), lambda p, t: (t * p, 0)),
                   pl.BlockSpec((8, LANES), lambda p, t: (0, 0))],
        out_shape=[jax.ShapeDtypeStruct((N_LAT, MOE_TOP_K), jnp.int32),
                   jax.ShapeDtypeStruct((N_LAT, MOE_TOP_K), F32),
                   jax.ShapeDtypeStruct((8, LANES), jnp.int32)],
        scratch_shapes=[pltpu.VMEM((8, LANES), F32)],
        compiler_params=_cparams("arbitrary", "arbitrary"),
        name="moe_route",
    )(logits)


N_PAD_SLOTS = MOE_ROWS - N_ASSIGN
Y2_ROWS = N_LAT + N_PAD_SLOTS // MOE_TOP_K


def _invert_kernel(dest_ref, bounds_ref, slot_ref):
    def real(a, c):
        slot_ref[dest_ref[a]] = a
        return c

    lax.fori_loop(0, N_ASSIGN, real, 0, unroll=8)

    def pad_range(e, count):
        def pad(s, cnt):
            slot_ref[s] = N_ASSIGN + cnt
            return cnt + 1

        return lax.fori_loop(bounds_ref[2 * e], bounds_ref[2 * e + 1], pad, count)

    lax.fori_loop(0, MOE_EXPERTS + 1, pad_range, 0)


def _invert(dest_flat, pad_bounds):
    return pl.pallas_call(
        _invert_kernel,
        grid_spec=pltpu.PrefetchScalarGridSpec(
            num_scalar_prefetch=2, grid=(1,), in_specs=[],
            out_specs=pl.BlockSpec(memory_space=pltpu.SMEM)),
        out_shape=jax.ShapeDtypeStruct((MOE_ROWS,), jnp.int32),
        compiler_params=_cparams("arbitrary"),
        name="moe_invert",
    )(dest_flat, pad_bounds)


LAST_BLOCK = MOE_N_BLOCKS - 1
MOE_NBUF = 3


def _swiglu_half(x, w1_ref, w3_ref, w2_ref):
    a = jnp.dot(x, w1_ref[0], preferred_element_type=F32)
    b = jnp.dot(x, w3_ref[0], preferred_element_type=F32)
    return jnp.dot((jax.nn.silu(a) * b).astype(BF16), w2_ref[0], preferred_element_type=F32)


def _expert_gather_kernel(be_ref, slot_ref, h_hbm, w1_ref, w3_ref, w2_ref, xs_ref, yb_ref, xbuf, sem):
    del be_ref
    j = pl.program_id(0)
    cur = j % MOE_NBUF

    def gather(blk, buf):
        base = blk * MOE_BLOCK
        for r in range(MOE_BLOCK):
            row = pl.multiple_of(slot_ref[base + r], SLAB)
            pltpu.make_async_copy(h_hbm.at[pl.ds(row, SLAB)], xbuf.at[buf, pl.ds(r * SLAB, SLAB)],
                                  sem.at[buf]).start(priority=r % 2)

    def wait(buf):
        pltpu.make_async_copy(h_hbm.at[pl.ds(0, MOE_BLOCK * SLAB)], xbuf.at[buf], sem.at[buf]).wait()

    @pl.when(j == 0)
    def _():
        gather(0, 0)
        gather(1, 1)

    wait(cur)
    for kk in range(D_MODEL // LANES):
        xs_ref[:, kk * LANES:(kk + 1) * LANES] = xbuf[cur, pl.ds(kk, MOE_BLOCK, stride=SLAB), :]
    gather(jnp.minimum(j + 2, LAST_BLOCK), (j + 2) % MOE_NBUF)
    yb_ref[...] = _swiglu_half(xs_ref[...].astype(BF16), w1_ref, w3_ref, w2_ref)

    @pl.when(j == LAST_BLOCK)
    def _():
        wait((j + 1) % MOE_NBUF)
        wait((j + 2) % MOE_NBUF)


def _expert_scatter_kernel(be_ref, slot_ref, xs_ref, w1_ref, w3_ref, w2_ref, yb_ref, y2_hbm, obuf, sem):
    del be_ref
    j = pl.program_id(0)
    cur = j % MOE_NBUF
    prev = (j + MOE_NBUF - 1) % MOE_NBUF

    def scatter(blk, buf):
        base = blk * MOE_BLOCK
        for r in range(MOE_BLOCK):
            row = pl.multiple_of(slot_ref[base + r], SLAB)
            pltpu.make_async_copy(obuf.at[buf, pl.ds(r * SLAB, SLAB)], y2_hbm.at[pl.ds(row, SLAB)],
                                  sem.at[buf]).start(priority=r % 2)

    def wait(buf):
        pltpu.make_async_copy(obuf.at[buf], y2_hbm.at[pl.ds(0, MOE_BLOCK * SLAB)], sem.at[buf]).wait()

    def compute():
        out = yb_ref[...] + _swiglu_half(xs_ref[...].astype(BF16), w1_ref, w3_ref, w2_ref)
        for kk in range(D_MODEL // LANES):
            obuf[cur, pl.ds(kk, MOE_BLOCK, stride=SLAB), :] = out[:, kk * LANES:(kk + 1) * LANES]

    @pl.when(j >= MOE_NBUF)
    def _():
        wait(cur)

    @pl.when(j == 0)
    def _():
        compute()

    @pl.when(j > 0)
    def _():
        scatter(j - 1, prev)
        compute()

    @pl.when(j == LAST_BLOCK)
    def _():
        scatter(j, cur)
        for b in range(MOE_NBUF):
            wait(b)


def _experts(block_expert, gather_rows, scatter_rows, h_lat, w1, w3, w2):
    def w_specs(c):
        return [pl.BlockSpec((1, D_MODEL, MOE_FF_BLK), lambda j, be, sl: (be[j], 0, c)),
                pl.BlockSpec((1, D_MODEL, MOE_FF_BLK), lambda j, be, sl: (be[j], 0, c)),
                pl.BlockSpec((1, MOE_FF_BLK, D_MODEL), lambda j, be, sl: (be[j], c, 0))]

    blk = pl.BlockSpec((MOE_BLOCK, D_MODEL), lambda j, be, sl: (j, 0))
    xs, yb = pl.pallas_call(
        _expert_gather_kernel,
        grid_spec=pltpu.PrefetchScalarGridSpec(
            num_scalar_prefetch=2, grid=(MOE_N_BLOCKS,),
            in_specs=[pl.BlockSpec(memory_space=pl.ANY)] + w_specs(0),
            out_specs=[blk, blk],
            scratch_shapes=[pltpu.VMEM((MOE_NBUF, MOE_BLOCK * SLAB, LANES), F32),
                            pltpu.SemaphoreType.DMA((MOE_NBUF,))]),
        out_shape=[jax.ShapeDtypeStruct((MOE_ROWS, D_MODEL), F32)] * 2,
        compiler_params=_cparams("arbitrary"),
        name="moe_experts_gather",
    )(block_expert, gather_rows, h_lat, w1, w3, w2)
    return pl.pallas_call(
        _expert_scatter_kernel,
        grid_spec=pltpu.PrefetchScalarGridSpec(
            num_scalar_prefetch=2, grid=(MOE_N_BLOCKS,),
            in_specs=[blk] + w_specs(1) + [blk],
            out_specs=pl.BlockSpec(memory_space=pl.ANY),
            scratch_shapes=[pltpu.VMEM((MOE_NBUF, MOE_BLOCK * SLAB, LANES), F32),
                            pltpu.SemaphoreType.DMA((MOE_NBUF,))]),
        out_shape=jax.ShapeDtypeStruct((MOE_TOP_K * Y2_ROWS * SLAB, LANES), F32),
        compiler_params=_cparams("arbitrary"),
        name="moe_experts_scatter",
    )(block_expert, scatter_rows, xs, w1, w3, w2, yb)


def _combine_kernel(x_ref, gate_ref, mod_ref, g_ref, ya_ref, yb_ref, o_ref):
    gates = gate_ref[0]
    rows = lambda ref: jnp.concatenate(
        [ref[pl.ds(kk, TM, stride=SLAB), :] for kk in range(D_MODEL // LANES)], axis=1)
    y = rows(ya_ref) * gates[:, 0:1] + rows(yb_ref) * gates[:, 1:2]
    x = x_ref[0] + mod_ref[0][5:6] * y
    o_ref[0] = x * lax.rsqrt(jnp.mean(x * x, axis=-1, keepdims=True) + NORM_EPS) * g_ref[...]


def _combine(x_lat, gates, mod_l, final_g, y2):
    nt = SEQ // TM
    return pl.pallas_call(
        _combine_kernel,
        grid=(BATCH, nt),
        in_specs=[pl.BlockSpec((1, TM, D_MODEL), lambda b, i: (b, i, 0)),
                  pl.BlockSpec((1, TM, MOE_TOP_K), lambda b, i: (b, i, 0)),
                  pl.BlockSpec((1, 6, D_MODEL), lambda b, i: (b, 0, 0)),
                  pl.BlockSpec((1, D_MODEL), lambda b, i: (0, 0)),
                  pl.BlockSpec((TM * SLAB, LANES), lambda b, i: (b * nt + i, 0)),
                  pl.BlockSpec((TM * SLAB, LANES), lambda b, i: (Y2_ROWS // TM + b * nt + i, 0))],
        out_specs=pl.BlockSpec((1, TM, D_MODEL), lambda b, i: (b, i, 0)),
        out_shape=jax.ShapeDtypeStruct((BATCH, SEQ, D_MODEL), F32),
        compiler_params=_cparams("parallel", "arbitrary"),
        name="moe_combine",
    )(x_lat, gates.reshape(BATCH, SEQ, MOE_TOP_K), mod_l, final_g.reshape(1, D_MODEL), y2, y2)


def _reorder_w_in(w):
    s = np.cumsum((3 * DN_WIDTH, DN_WIDTH, 2 * DN_HEADS, 2 * DN_HEADS, ATTN_WIDTH, ATTN_KV_WIDTH, ATTN_KV_WIDTH,
                   MLP_WIDTH, MLP_WIDTH)).tolist()
    ba = w[:, s[1]:s[3]]
    return jnp.concatenate([w[:, :s[1]], w[:, s[3]:], ba,
                            jnp.zeros((D_MODEL, LANES - 4 * DN_HEADS), w.dtype)], axis=1).astype(BF16)


def kernel(x, c, ctx, c_ctx, mod_w, mod_b, norm1_g, norm2_g, w_in, conv_w, dn_a_log, dn_dt_bias, dn_norm_g,
           q_norm_g, k_norm_g, sgu_norm_g, sgu_w, sgu_b, w_out, ffn_w1, ffn_w3, ffn_w2, router_w, router_b,
           moe_w1, moe_w3, moe_w2, final_norm_g):
    assert DEPTH == 2 and x.shape == (BATCH, SEQ, D_MODEL) and ctx.shape == (BATCH, CTX_LEN, D_MODEL)
    cond8 = jnp.concatenate([c, c_ctx[None], jnp.zeros((8 - BATCH - 1, D_MODEL), F32)], axis=0)
    mod = _modulation(cond8, mod_w, mod_b)
    rope_c, rope_s = _rope_tables()
    xs = (x, ctx)
    for layer in range(DEPTH):
        last = layer == DEPTH - 1
        zqkv, zgate, zu, zv2, zba, qh, kh, vh = _in_proj(xs, mod[layer], norm1_g[layer], _reorder_w_in(w_in[layer]),
                                                         q_norm_g[layer], k_norm_g[layer], rope_c, rope_s)
        qkv = _dn_prep(zqkv, conv_w[layer])
        o_dn = _deltanet(qkv, zba, dn_a_log[layer], dn_dt_bias[layer])
        if not last:
            i = (layer + 1) // 2
            attn_lat, moe_w = _attention_lat(qh, kh, vh, cast=(
                moe_w1[i].reshape(MOE_EXPERTS * D_MODEL, MOE_D_FF), moe_w3[i].reshape(MOE_EXPERTS * D_MODEL, MOE_D_FF),
                moe_w2[i].reshape(MOE_EXPERTS * MOE_D_FF, D_MODEL)))
        else:
            attn_lat, _ = _attention_lat(qh, kh, vh)
        attns = (attn_lat,)
        if len(xs) == 2:
            attns += (_attention_ctx(qh, kh, vh),)
        elif not last:
            attns = (jnp.concatenate([attns[0], _attention_ctx(qh, kh, vh)], axis=2),)
        mix_args = (xs, mod[layer], o_dn, zgate, attns, zu, zv2, dn_norm_g[layer], sgu_norm_g[layer],
                    sgu_w[layer], sgu_b[layer], w_out[layer].astype(BF16), norm2_g[layer])
        if not last:
            i = layer // 2
            x_mid, h_mid = _mix(*mix_args)
            xs = (_ffn(x_mid, h_mid, mod[layer], ffn_w1[i].astype(BF16), ffn_w3[i].astype(BF16),
                       ffn_w2[i].astype(BF16)),)
        else:
            i = layer // 2
            x_lat, h_lat, logits = _mix(*mix_args, router=(router_w[i], router_b[i]))
            dest, gates, counts = _route(logits.reshape(N_LAT, LANES))
            cnt = counts[0, :MOE_EXPERTS]
            padded = (cnt + MOE_BLOCK - 1) // MOE_BLOCK * MOE_BLOCK
            pad_ends = jnp.cumsum(padded)
            block_expert = jnp.minimum(
                jnp.sum(pad_ends[None, :] <= (jnp.arange(MOE_N_BLOCKS) * MOE_BLOCK)[:, None], axis=1),
                MOE_EXPERTS - 1).astype(jnp.int32)
            lo = jnp.concatenate([pad_ends - padded + cnt, pad_ends[-1:]])
            hi = jnp.concatenate([pad_ends, jnp.full((1,), MOE_ROWS, pad_ends.dtype)])
            pad_bounds = jnp.stack([lo, hi], axis=1).reshape(-1).astype(jnp.int32)
            slot_src = _invert(dest.reshape(N_ASSIGN), pad_bounds)
            tok = lax.shift_right_logical(slot_src, 1)
            gather_rows = jnp.minimum(tok, N_LAT - 1) * SLAB
            scatter_rows = ((slot_src & 1) * Y2_ROWS + tok) * SLAB
            y2 = _experts(block_expert, gather_rows, scatter_rows, h_lat.reshape(N_LAT * SLAB, LANES),
                          moe_w[0].reshape(MOE_EXPERTS, D_MODEL, MOE_D_FF),
                          moe_w[1].reshape(MOE_EXPERTS, D_MODEL, MOE_D_FF),
                          moe_w[2].reshape(MOE_EXPERTS, MOE_D_FF, D_MODEL))
            return _combine(x_lat, gates, mod[layer], final_norm_g, y2)
```
